```python
import jax, jax.numpy as jnp
from jax import lax
import numpy as np

D_MODEL = 2048
BATCH = 2
SEQ = 8192
DEPTH = 2

CHUNK = 64
N_MIXERS = 2
N_A_LAYERS = (DEPTH + 1) // 2
N_B_LAYERS = DEPTH // 2
EPS = 1e-6

D_FF = 5632

SGU_BLOCK = 128
SGU_WIDTH = 2 * D_MODEL
SGU_GROUPS = 8
SGU_GROUP_DIM = SGU_WIDTH // SGU_GROUPS

MLA_HEADS = 16
Q_LORA = 512
KV_LORA = 512
QK_NOPE = 128
QK_ROPE = 64
V_DIM = 128
QK_DIM = QK_NOPE + QK_ROPE
ROPE_THETA = 10000.0
Q_BLOCK = 128

kernel_name = "hybrid_sgu_mla_macaron_encoder"


def rmsnorm(x, g):
    xf = x.astype(jnp.float32)
    y = xf * lax.rsqrt(jnp.mean(xf * xf, axis=-1, keepdims=True) + EPS)
    return (y * g.astype(jnp.float32)).astype(x.dtype)


def layernorm(x, g, b):
    xf = x.astype(jnp.float32)
    mu = jnp.mean(xf, axis=-1, keepdims=True)
    var = jnp.mean(jnp.square(xf - mu), axis=-1, keepdims=True)
    y = (xf - mu) * lax.rsqrt(var + EPS)
    return (y * g.astype(jnp.float32) + b.astype(jnp.float32)).astype(x.dtype)


def swiglu(h, w_in, w_out):
    gate, up = jnp.split(h @ w_in, 2, axis=-1)
    return (jax.nn.silu(gate) * up) @ w_out


def rope(x, positions):
    half = x.shape[-1] // 2
    inv_freq = 1.0 / (ROPE_THETA ** (jnp.arange(half, dtype=jnp.float32) / half))
    ang = positions.astype(jnp.float32)[..., None] * inv_freq
    cos = jnp.cos(ang)[:, :, None, :]
    sin = jnp.sin(ang)[:, :, None, :]
    xf = x.astype(jnp.float32)
    x1, x2 = xf[..., :half], xf[..., half:]
    out = jnp.concatenate([x1 * cos - x2 * sin, x1 * sin + x2 * cos], axis=-1)
    return out.astype(x.dtype)


def sgu_mixer(h, w_in, v_gain, v_bias, w_spatial, b_spatial, w_out):
    B, S, _ = h.shape
    uv = jax.nn.gelu(h @ w_in)
    u, v = jnp.split(uv, 2, axis=-1)
    v = layernorm(v, v_gain, v_bias)
    nb = S // SGU_BLOCK
    v = v.reshape(B, nb, SGU_BLOCK, SGU_GROUPS, SGU_GROUP_DIM)
    pos_chunk = jnp.arange(SGU_BLOCK) // CHUNK
    mask = pos_chunk[:, None] >= pos_chunk[None, :]
    ws = jnp.where(mask[None], w_spatial, jnp.zeros_like(w_spatial))
    mixed = jnp.einsum('gij,bnjgc->bnigc', ws, v)
    mixed = mixed + b_spatial.T[None, None, :, :, None]
    gated = u * mixed.reshape(B, S, SGU_WIDTH)
    return gated @ w_out


def mla_mixer(h, positions, w_in, q_norm_g, w_q_up, kv_norm_g, w_kv_up, w_out):
    B, S, _ = h.shape
    proj = h @ w_in
    q_lat, kv_lat, k_rope = jnp.split(proj, [Q_LORA, Q_LORA + KV_LORA], axis=-1)
    q = (rmsnorm(q_lat, q_norm_g) @ w_q_up).reshape(B, S, MLA_HEADS, QK_DIM)
    q = jnp.concatenate([q[..., :QK_NOPE], rope(q[..., QK_NOPE:], positions)], axis=-1)
    q = q * (QK_DIM ** -0.5)
    k_rope = rope(k_rope[:, :, None, :], positions)
    kv = (rmsnorm(kv_lat, kv_norm_g) @ w_kv_up).reshape(B, S, MLA_HEADS, QK_NOPE + V_DIM)
    k_nope, v = kv[..., :QK_NOPE], kv[..., QK_NOPE:]
    k = jnp.concatenate(
        [k_nope, jnp.broadcast_to(k_rope, (B, S, MLA_HEADS, QK_ROPE))], axis=-1)

    nq = S // Q_BLOCK
    q_blocks = q.reshape(B, nq, Q_BLOCK, MLA_HEADS, QK_DIM).transpose(1, 0, 2, 3, 4)
    key_chunk = jnp.arange(S) // CHUNK

    def attend(args):
        qb, idx = args
        q_chunk = (idx * Q_BLOCK + jnp.arange(Q_BLOCK)) // CHUNK
        mask = key_chunk[None, :] <= q_chunk[:, None]
        s = jnp.einsum('bqhd,bkhd->bhqk', qb, k).astype(jnp.float32)
        s = jnp.where(mask[None, None], s, -jnp.inf)
        p = jax.nn.softmax(s, axis=-1).astype(v.dtype)
        return jnp.einsum('bhqk,bkhd->bqhd', p, v)

    o = lax.map(attend, (q_blocks, jnp.arange(nq)))
    o = o.transpose(1, 0, 2, 3, 4).reshape(B, S, MLA_HEADS * V_DIM)
    return o @ w_out


def setup_inputs(seed: int = 0) -> dict:
    key = jax.random.key(seed)
    ks = jax.random.split(key, 24)

    def nrm(k, shape, scale):
        return jax.random.normal(k, shape, jnp.float32) * scale

    def gain(k, shape):
        return 1.0 + 0.02 * jax.random.normal(k, shape, jnp.float32)

    x = jax.random.normal(ks[0], (BATCH, SEQ, D_MODEL), jnp.float32)
    offset = jax.random.randint(ks[1], (BATCH, 1), 0, 4096, dtype=jnp.int32)
    positions = offset + jnp.arange(SEQ, dtype=jnp.int32)[None, :]
    return {
        "x": x,
        "positions": positions,
        "ln_ffn1": gain(ks[2], (DEPTH, D_MODEL)),
        "ffn1_w_in": nrm(ks[3], (DEPTH, D_MODEL, 2 * D_FF), D_MODEL ** -0.5),
        "ffn1_w_out": nrm(ks[4], (DEPTH, D_FF, D_MODEL), D_FF ** -0.5),
        "ln_mix": gain(ks[5], (DEPTH, D_MODEL)),
        "ln_ffn2": gain(ks[6], (DEPTH, D_MODEL)),
        "ffn2_w_in": nrm(ks[7], (DEPTH, D_MODEL, 2 * D_FF), D_MODEL ** -0.5),
        "ffn2_w_out": nrm(ks[8], (DEPTH, D_FF, D_MODEL), D_FF ** -0.5),
        "sgu_w_in": nrm(ks[9], (N_A_LAYERS, D_MODEL, 2 * SGU_WIDTH), D_MODEL ** -0.5),
        "sgu_v_gain": gain(ks[10], (N_A_LAYERS, SGU_WIDTH)),
        "sgu_v_bias": nrm(ks[11], (N_A_LAYERS, SGU_WIDTH), 0.02),
        "sgu_w_spatial": nrm(ks[12], (N_A_LAYERS, SGU_GROUPS, SGU_BLOCK, SGU_BLOCK), SGU_BLOCK ** -0.5),
        "sgu_b_spatial": gain(ks[13], (N_A_LAYERS, SGU_GROUPS, SGU_BLOCK)),
        "sgu_w_out": nrm(ks[14], (N_A_LAYERS, SGU_WIDTH, D_MODEL), SGU_WIDTH ** -0.5),
        "mla_w_in": nrm(ks[15], (N_B_LAYERS, D_MODEL, Q_LORA + KV_LORA + QK_ROPE), D_MODEL ** -0.5),
        "mla_q_norm": gain(ks[16], (N_B_LAYERS, Q_LORA)),
        "mla_w_q_up": nrm(ks[17], (N_B_LAYERS, Q_LORA, MLA_HEADS * QK_DIM), Q_LORA ** -0.5),
        "mla_kv_norm": gain(ks[18], (N_B_LAYERS, KV_LORA)),
        "mla_w_kv_up": nrm(ks[19], (N_B_LAYERS, KV_LORA, MLA_HEADS * (QK_NOPE + V_DIM)), KV_LORA ** -0.5),
        "mla_w_out": nrm(ks[20], (N_B_LAYERS, MLA_HEADS * V_DIM, D_MODEL), (MLA_HEADS * V_DIM) ** -0.5),
        "ln_final": gain(ks[21], (D_MODEL,)),
    }


def reference(x, positions, ln_ffn1, ffn1_w_in, ffn1_w_out, ln_mix, ln_ffn2, ffn2_w_in, ffn2_w_out,
              sgu_w_in, sgu_v_gain, sgu_v_bias, sgu_w_spatial, sgu_b_spatial, sgu_w_out,
              mla_w_in, mla_q_norm, mla_w_q_up, mla_kv_norm, mla_w_kv_up, mla_w_out, ln_final):
    for i in range(DEPTH):
        x = x + 0.5 * swiglu(rmsnorm(x, ln_ffn1[i]), ffn1_w_in[i], ffn1_w_out[i])
        h = rmsnorm(x, ln_mix[i])
        j = i // N_MIXERS
        if i % N_MIXERS == 0:
            x = x + sgu_mixer(h, sgu_w_in[j], sgu_v_gain[j], sgu_v_bias[j],
                              sgu_w_spatial[j], sgu_b_spatial[j], sgu_w_out[j])
        else:
            x = x + mla_mixer(h, positions, mla_w_in[j], mla_q_norm[j], mla_w_q_up[j],
                              mla_kv_norm[j], mla_w_kv_up[j], mla_w_out[j])
        x = x + 0.5 * swiglu(rmsnorm(x, ln_ffn2[i]), ffn2_w_in[i], ffn2_w_out[i])
    return rmsnorm(x, ln_final)
```

```python
import functools

import jax
import jax.numpy as jnp
from jax import lax
from jax.experimental import pallas as pl
from jax.experimental.pallas import tpu as pltpu

F32 = jnp.float32
BF16 = jnp.bfloat16

EPS = 1e-6
CHUNK = 64
SGU_BLOCK = 128
SGU_GROUPS = 8
MLA_HEADS = 16
Q_LORA = 512
KV_LORA = 512
QK_NOPE = 128
QK_ROPE = 64
V_DIM = 128
QK_DIM = QK_NOPE + QK_ROPE
ROPE_THETA = 10000.0

LANES = 128
HEAD_PAD = 2 * LANES
VMEM_LIMIT = 56 * 1024 * 1024

FFN_TM = 512
FFN_TF = 512
SGU_TM = 512
MLA_P_TM = 256
MLA_O_TM = 512
ATT_TQ = 256


def _rms_normalize(x, gain):
    return x * lax.rsqrt(jnp.mean(x * x, axis=-1, keepdims=True) + EPS) * gain


def _params(*sem):
    return pltpu.CompilerParams(dimension_semantics=sem, vmem_limit_bytes=VMEM_LIMIT)


def _ffn_kernel(x_ref, g_ref, wg_ref, wu_ref, wo_ref, *rest, final_norm):
    if final_norm:
        gf_ref, o_ref, xn_ref, acc_ref = rest
    else:
        o_ref, xn_ref, acc_ref = rest
    j = pl.program_id(1)

    @pl.when(j == 0)
    def _():
        xn_ref[...] = _rms_normalize(x_ref[...], g_ref[...]).astype(BF16)

    xn = xn_ref[...]
    gate = jnp.dot(xn, wg_ref[...], preferred_element_type=F32)
    up = jnp.dot(xn, wu_ref[...], preferred_element_type=F32)
    act = (gate * jax.nn.sigmoid(gate) * up).astype(BF16)
    part = jnp.dot(act, wo_ref[...], preferred_element_type=F32)

    @pl.when(j == 0)
    def _():
        acc_ref[...] = part

    @pl.when(j > 0)
    def _():
        acc_ref[...] += part

    @pl.when(j == pl.num_programs(1) - 1)
    def _():
        y = x_ref[...] + 0.5 * acc_ref[...]
        if final_norm:
            y = _rms_normalize(y, gf_ref[...])
        o_ref[...] = y


def _ffn(x, gain, w_in, w_out, final_gain=None):
    m, d = x.shape
    f = w_out.shape[0]
    nf = f // FFN_TF
    final_norm = final_gain is not None
    in_specs = [
        pl.BlockSpec((FFN_TM, d), lambda i, j: (i, 0)),
        pl.BlockSpec((1, d), lambda i, j: (0, 0)),
        pl.BlockSpec((d, FFN_TF), lambda i, j: (0, j)),
        pl.BlockSpec((d, FFN_TF), lambda i, j: (0, j + nf)),
        pl.BlockSpec((FFN_TF, d), lambda i, j: (j, 0)),
    ]
    args = [x, gain.reshape(1, d), w_in, w_in, w_out]
    if final_norm:
        in_specs.append(pl.BlockSpec((1, d), lambda i, j: (0, 0)))
        args.append(final_gain.reshape(1, d))
    return pl.pallas_call(
        functools.partial(_ffn_kernel, final_norm=final_norm),
        grid=(m // FFN_TM, nf),
        in_specs=in_specs,
        out_specs=pl.BlockSpec((FFN_TM, d), lambda i, j: (i, 0)),
        out_shape=jax.ShapeDtypeStruct((m, d), F32),
        scratch_shapes=[pltpu.VMEM((FFN_TM, d), BF16), pltpu.VMEM((FFN_TM, d), F32)],
        compiler_params=_params("parallel", "arbitrary"),
        name="ffn_final" if final_norm else "ffn",
    )(*args)


def _sgu_kernel(x_ref, g_ref, wu_ref, wv_ref, vg_ref, vb_ref, ws_ref, bs_ref, wo_ref,
                o_ref, xn_ref, u_ref, v_ref, mu_ref, rstd_ref, acc_ref):
    j = pl.program_id(1)
    gw = wu_ref.shape[1]
    tm = x_ref.shape[0]

    @pl.when(j == 0)
    def _():
        xn_ref[...] = _rms_normalize(x_ref[...], g_ref[...]).astype(BF16)

    @pl.when(j < SGU_GROUPS)
    def _():
        xn = xn_ref[...]
        u = jax.nn.gelu(jnp.dot(xn, wu_ref[...], preferred_element_type=F32))
        v = jax.nn.gelu(jnp.dot(xn, wv_ref[...], preferred_element_type=F32))
        u_ref[j] = u.astype(BF16)
        v_ref[j] = v

    @pl.when(j == SGU_GROUPS)
    def _():
        width = SGU_GROUPS * gw
        total = jnp.zeros((tm, 1), F32)
        for g in range(SGU_GROUPS):
            total += jnp.sum(v_ref[g], axis=-1, keepdims=True)
        mu = total / width
        sq = jnp.zeros((tm, 1), F32)
        for g in range(SGU_GROUPS):
            dv = v_ref[g] - mu
            sq += jnp.sum(dv * dv, axis=-1, keepdims=True)
        mu_ref[...] = mu
        rstd_ref[...] = lax.rsqrt(sq / width + EPS)

    @pl.when(j >= SGU_GROUPS)
    def _():
        g = j - SGU_GROUPS
        vn = ((v_ref[g] - mu_ref[...]) * rstd_ref[...] * vg_ref[0] + vb_ref[0]).astype(BF16)
        row_chunk = lax.broadcasted_iota(jnp.int32, (SGU_BLOCK, SGU_BLOCK), 0) // CHUNK
        col_chunk = lax.broadcasted_iota(jnp.int32, (SGU_BLOCK, SGU_BLOCK), 1) // CHUNK
        ws = jnp.where(row_chunk >= col_chunk, ws_ref[0], 0.0).astype(BF16)
        bias = bs_ref[0]
        u = u_ref[g]
        gated = []
        for b in range(tm // SGU_BLOCK):
            rows = slice(b * SGU_BLOCK, (b + 1) * SGU_BLOCK)
            mixed = jnp.dot(ws, vn[rows], preferred_element_type=F32) + bias
            gated.append((u[rows].astype(F32) * mixed).astype(BF16))
        gated = jnp.concatenate(gated, axis=0)
        part = jnp.dot(gated, wo_ref[...], preferred_element_type=F32)

        @pl.when(g == 0)
        def _():
            acc_ref[...] = part

        @pl.when(g > 0)
        def _():
            acc_ref[...] += part

    @pl.when(j == pl.num_programs(1) - 1)
    def _():
        o_ref[...] = x_ref[...] + acc_ref[...]


def _sgu(x, gain, w_in, v_gain, v_bias, w_spatial, b_spatial, w_out):
    m, d = x.shape
    width = w_out.shape[0]
    gw = width // SGU_GROUPS
    ng = SGU_GROUPS
    tm = SGU_TM

    def phase1(j):
        return jnp.minimum(j, ng - 1)

    def phase2(j):
        return jnp.maximum(j - ng, 0)

    return pl.pallas_call(
        _sgu_kernel,
        grid=(m // tm, 2 * ng),
        in_specs=[
            pl.BlockSpec((tm, d), lambda i, j: (i, 0)),
            pl.BlockSpec((1, d), lambda i, j: (0, 0)),
            pl.BlockSpec((d, gw), lambda i, j: (0, phase1(j))),
            pl.BlockSpec((d, gw), lambda i, j: (0, ng + phase1(j))),
            pl.BlockSpec((1, 1, gw), lambda i, j: (phase2(j), 0, 0)),
            pl.BlockSpec((1, 1, gw), lambda i, j: (phase2(j), 0, 0)),
            pl.BlockSpec((1, SGU_BLOCK, SGU_BLOCK), lambda i, j: (phase2(j), 0, 0)),
            pl.BlockSpec((1, SGU_BLOCK, 1), lambda i, j: (phase2(j), 0, 0)),
            pl.BlockSpec((gw, d), lambda i, j: (phase2(j), 0)),
        ],
        out_specs=pl.BlockSpec((tm, d), lambda i, j: (i, 0)),
        out_shape=jax.ShapeDtypeStruct((m, d), F32),
        scratch_shapes=[
            pltpu.VMEM((tm, d), BF16),
            pltpu.VMEM((ng, tm, gw), BF16),
            pltpu.VMEM((ng, tm, gw), F32),
            pltpu.VMEM((tm, 1), F32),
            pltpu.VMEM((tm, 1), F32),
            pltpu.VMEM((tm, d), F32),
        ],
        compiler_params=_params("parallel", "arbitrary"),
        name="sgu",
    )(x, gain.reshape(1, d), w_in, w_in,
      v_gain.reshape(ng, 1, gw), v_bias.reshape(ng, 1, gw),
      w_spatial, b_spatial.reshape(ng, SGU_BLOCK, 1), w_out)


def _mla_proj_kernel(x_ref, pos_ref, freq_ref, g_ref, wi_ref, gq_ref, wq_ref, gkv_ref, wkv_ref,
                     q_ref, kv_ref, kr_ref):
    hn = _rms_normalize(x_ref[...], g_ref[...]).astype(BF16)
    proj = jnp.dot(hn, wi_ref[...], preferred_element_type=F32)
    qn = _rms_normalize(proj[:, :Q_LORA], gq_ref[...]).astype(BF16)
    kvn = _rms_normalize(proj[:, Q_LORA:Q_LORA + KV_LORA], gkv_ref[...]).astype(BF16)
    kr = proj[:, Q_LORA + KV_LORA:]

    ang = pos_ref[...].astype(F32) * freq_ref[...]
    lane = lax.broadcasted_iota(jnp.int32, ang.shape, 1)
    half = QK_ROPE // 2
    cos = jnp.cos(ang)
    sin = jnp.sin(ang)
    sin_lo = jnp.where(lane < half, -sin, 0.0)
    sin_hi = jnp.where((lane >= half) & (lane < QK_ROPE), sin, 0.0)

    def rope(t):
        return (t * cos + pltpu.roll(t, LANES - half, 1) * sin_lo
                + pltpu.roll(t, half, 1) * sin_hi)

    kr_ref[...] = rope(kr).astype(BF16)
    kv_ref[...] = jnp.dot(kvn, wkv_ref[...], preferred_element_type=F32).astype(BF16)

    scale = QK_DIM ** -0.5
    for h in range(MLA_HEADS):
        lo = h * HEAD_PAD
        qh = jnp.dot(qn, wq_ref[:, lo:lo + HEAD_PAD], preferred_element_type=F32)
        q_ref[:, lo:lo + LANES] = (qh[:, :LANES] * scale).astype(BF16)
        q_ref[:, lo + LANES:lo + HEAD_PAD] = (rope(qh[:, LANES:]) * scale).astype(BF16)


def _mla_proj(x, pos, freq, gain, w_in, gq, wq, gkv, wkv):
    m, d = x.shape
    tm = MLA_P_TM
    const = lambda i: (0, 0)
    nq = wq.shape[1]
    nkv = wkv.shape[1]
    return pl.pallas_call(
        _mla_proj_kernel,
        grid=(m // tm,),
        in_specs=[
            pl.BlockSpec((tm, d), lambda i: (i, 0)),
            pl.BlockSpec((tm, 1), lambda i: (i, 0)),
            pl.BlockSpec((1, LANES), const),
            pl.BlockSpec((1, d), const),
            pl.BlockSpec(w_in.shape, const),
            pl.BlockSpec((1, Q_LORA), const),
            pl.BlockSpec(wq.shape, const),
            pl.BlockSpec((1, KV_LORA), const),
            pl.BlockSpec(wkv.shape, const),
        ],
        out_specs=[
            pl.BlockSpec((tm, nq), lambda i: (i, 0)),
            pl.BlockSpec((tm, nkv), lambda i: (i, 0)),
            pl.BlockSpec((tm, LANES), lambda i: (i, 0)),
        ],
        out_shape=[
            jax.ShapeDtypeStruct((m, nq), BF16),
            jax.ShapeDtypeStruct((m, nkv), BF16),
            jax.ShapeDtypeStruct((m, LANES), BF16),
        ],
        compiler_params=_params("parallel"),
        name="mla_proj",
    )(x, pos, freq, gain.reshape(1, d), w_in, gq.reshape(1, -1), wq, gkv.reshape(1, -1), wkv)


def _mla_attn_kernel(q_ref, kn_ref, kr_ref, v_ref, o_ref):
    i = pl.program_id(2)
    tq = q_ref.shape[0]
    tk = tq
    q = q_ref[...]

    def scores(j):
        rows = pl.ds(pl.multiple_of(j * tk, tk), tk)
        k = jnp.concatenate([kn_ref[rows, :], kr_ref[rows, :]], axis=1)
        s = lax.dot_general(q, k, (((1,), (1,)), ((), ())), preferred_element_type=F32)
        return s, v_ref[rows, :]

    def update(carry, s, v):
        m_prev, l_prev, acc = carry
        m_new = jnp.maximum(m_prev, jnp.max(s, axis=-1, keepdims=True))
        alpha = jnp.exp(m_prev - m_new)
        p = jnp.exp(s - m_new)
        l_new = alpha * l_prev + jnp.sum(p, axis=-1, keepdims=True)
        acc = alpha * acc + jnp.dot(p.astype(BF16), v, preferred_element_type=F32)
        return m_new, l_new, acc

    def body(j, carry):
        s, v = scores(j)
        return update(carry, s, v)

    init = (jnp.full((tq, 1), -jnp.inf, F32), jnp.zeros((tq, 1), F32),
            jnp.zeros((tq, V_DIM), F32))
    carry = lax.fori_loop(0, i, body, init)

    s, v = scores(i)
    row_chunk = lax.broadcasted_iota(jnp.int32, s.shape, 0) // CHUNK
    col_chunk = lax.broadcasted_iota(jnp.int32, s.shape, 1) // CHUNK
    s = jnp.where(col_chunk <= row_chunk, s, -jnp.inf)
    _, l, acc = update(carry, s, v)
    o_ref[...] = (acc / l).astype(o_ref.dtype)


def _mla_attn(q, kv, kr):
    b, s, _ = q.shape
    tq = ATT_TQ
    return pl.pallas_call(
        _mla_attn_kernel,
        grid=(b, MLA_HEADS, s // tq),
        in_specs=[
            pl.BlockSpec((None, tq, HEAD_PAD), lambda b_, h, i: (b_, i, h)),
            pl.BlockSpec((None, s, QK_NOPE), lambda b_, h, i: (b_, 0, 2 * h)),
            pl.BlockSpec((None, s, LANES), lambda b_, h, i: (b_, 0, 0)),
            pl.BlockSpec((None, s, V_DIM), lambda b_, h, i: (b_, 0, 2 * h + 1)),
        ],
        out_specs=pl.BlockSpec((None, tq, V_DIM), lambda b_, h, i: (b_, i, h)),
        out_shape=jax.ShapeDtypeStruct((b, s, MLA_HEADS * V_DIM), BF16),
        compiler_params=_params("parallel", "parallel", "arbitrary"),
        name="mla_attn",
    )(q, kv, kr, kv)


def _mla_out_kernel(x_ref, o_ref, w_ref, y_ref):
    y_ref[...] = x_ref[...] + jnp.dot(o_ref[...], w_ref[...], preferred_element_type=F32)


def _mla_out(x, o, w):
    m, d = x.shape
    tm = MLA_O_TM
    return pl.pallas_call(
        _mla_out_kernel,
        grid=(m // tm,),
        in_specs=[
            pl.BlockSpec((tm, d), lambda i: (i, 0)),
            pl.BlockSpec((tm, o.shape[1]), lambda i: (i, 0)),
            pl.BlockSpec(w.shape, lambda i: (0, 0)),
        ],
        out_specs=pl.BlockSpec((tm, d), lambda i: (i, 0)),
        out_shape=jax.ShapeDtypeStruct((m, d), F32),
        compiler_params=_params("parallel"),
        name="mla_out",
    )(x, o, w)


def _mla(x, positions, gain, w_in, gq, w_q_up, gkv, w_kv_up, w_out):
    b, s = positions.shape
    m, d = x.shape
    w_in_p = jnp.pad(w_in, ((0, 0), (0, LANES - QK_ROPE))).astype(BF16)
    wq_p = jnp.pad(w_q_up.reshape(Q_LORA, MLA_HEADS, QK_DIM),
                   ((0, 0), (0, 0), (0, HEAD_PAD - QK_DIM))).reshape(Q_LORA, MLA_HEADS * HEAD_PAD)
    half = QK_ROPE // 2
    inv_freq = 1.0 / (ROPE_THETA ** (jnp.arange(half, dtype=F32) / half))
    freq = jnp.concatenate([inv_freq, inv_freq, jnp.zeros((LANES - QK_ROPE,), F32)]).reshape(1, LANES)
    q, kv, kr = _mla_proj(x, positions.reshape(m, 1), freq, gain, w_in_p, gq, wq_p.astype(BF16),
                          gkv, w_kv_up.astype(BF16))
    o = _mla_attn(q.reshape(b, s, -1), kv.reshape(b, s, -1), kr.reshape(b, s, LANES))
    return _mla_out(x, o.reshape(m, -1), w_out.astype(BF16))


def kernel(x, positions, ln_ffn1, ffn1_w_in, ffn1_w_out, ln_mix, ln_ffn2, ffn2_w_in, ffn2_w_out,
           sgu_w_in, sgu_v_gain, sgu_v_bias, sgu_w_spatial, sgu_b_spatial, sgu_w_out,
           mla_w_in, mla_q_norm, mla_w_q_up, mla_kv_norm, mla_w_kv_up, mla_w_out, ln_final):
    b, s, d = x.shape
    depth = ln_ffn1.shape[0]
    h = x.reshape(b * s, d)
    for i in range(depth):
        h = _ffn(h, ln_ffn1[i], ffn1_w_in[i].astype(BF16), ffn1_w_out[i].astype(BF16))
        j = i // 2
        if i % 2 == 0:
            h = _sgu(h, ln_mix[i], sgu_w_in[j].astype(BF16), sgu_v_gain[j], sgu_v_bias[j],
                     sgu_w_spatial[j], sgu_b_spatial[j], sgu_w_out[j].astype(BF16))
        else:
            h = _mla(h, positions, ln_mix[i], mla_w_in[j], mla_q_norm[j], mla_w_q_up[j],
                     mla_kv_norm[j], mla_w_kv_up[j], mla_w_out[j])
        last = i == depth - 1
        h = _ffn(h, ln_ffn2[i], ffn2_w_in[i].astype(BF16), ffn2_w_out[i].astype(BF16),
                 final_gain=ln_final if last else None)
    return h.reshape(b, s, d)
```

```python
import functools
import math

import jax
import jax.numpy as jnp
from jax import lax
from jax.experimental import pallas as pl
from jax.experimental.pallas import tpu as pltpu

F32 = jnp.float32
BF16 = jnp.bfloat16

EPS = 1e-6
CHUNK = 64
SGU_BLOCK = 128
SGU_GROUPS = 8
MLA_HEADS = 16
Q_LORA = 512
KV_LORA = 512
QK_NOPE = 128
QK_ROPE = 64
V_DIM = 128
QK_DIM = QK_NOPE + QK_ROPE
ROPE_THETA = 10000.0

LANES = 128
HEAD_PAD = 2 * LANES
VMEM_LIMIT = 56 * 1024 * 1024

FFN_TM = 512
FFN_TF = 512
SGU_TM = 512
MLA_P_TM = 256
MLA_O_TM = 512
ATT_T = 512
ATT_HEADS = 2


def _rms_normalize(x, gain):
    return x * lax.rsqrt(jnp.mean(x * x, axis=-1, keepdims=True) + EPS) * gain


def _params(*sem):
    return pltpu.CompilerParams(dimension_semantics=sem, vmem_limit_bytes=VMEM_LIMIT)


def _dot_nt(a, b):
    return lax.dot_general(a, b, (((1,), (1,)), ((), ())), preferred_element_type=F32)


def _ffn_kernel(x_ref, g_ref, wg_ref, wu_ref, wo_ref, *rest, final_norm):
    if final_norm:
        gf_ref, o_ref, xn_ref, acc_ref = rest
    else:
        o_ref, xn_ref, acc_ref = rest
    j = pl.program_id(1)

    @pl.when(j == 0)
    def _():
        xn_ref[...] = _rms_normalize(x_ref[...], g_ref[...]).astype(BF16)

    xn = xn_ref[...]
    gate = jnp.dot(xn, wg_ref[...], preferred_element_type=F32)
    up = jnp.dot(xn, wu_ref[...], preferred_element_type=F32)
    act = (gate * jax.nn.sigmoid(gate) * up).astype(BF16)
    part = jnp.dot(act, wo_ref[...], preferred_element_type=F32)

    @pl.when(j == 0)
    def _():
        acc_ref[...] = part

    @pl.when(j > 0)
    def _():
        acc_ref[...] += part

    @pl.when(j == pl.num_programs(1) - 1)
    def _():
        y = x_ref[...] + 0.5 * acc_ref[...]
        if final_norm:
            y = _rms_normalize(y, gf_ref[...])
        o_ref[...] = y


def _ffn(x, gain, w_in, w_out, final_gain=None):
    m, d = x.shape
    f = w_out.shape[0]
    nf = f // FFN_TF
    final_norm = final_gain is not None
    in_specs = [
        pl.BlockSpec((FFN_TM, d), lambda i, j: (i, 0)),
        pl.BlockSpec((1, d), lambda i, j: (0, 0)),
        pl.BlockSpec((d, FFN_TF), lambda i, j: (0, j)),
        pl.BlockSpec((d, FFN_TF), lambda i, j: (0, j + nf)),
        pl.BlockSpec((FFN_TF, d), lambda i, j: (j, 0)),
    ]
    args = [x, gain.reshape(1, d), w_in, w_in, w_out]
    if final_norm:
        in_specs.append(pl.BlockSpec((1, d), lambda i, j: (0, 0)))
        args.append(final_gain.reshape(1, d))
    return pl.pallas_call(
        functools.partial(_ffn_kernel, final_norm=final_norm),
        grid=(m // FFN_TM, nf),
        in_specs=in_specs,
        out_specs=pl.BlockSpec((FFN_TM, d), lambda i, j: (i, 0)),
        out_shape=jax.ShapeDtypeStruct((m, d), F32),
        scratch_shapes=[pltpu.VMEM((FFN_TM, d), BF16), pltpu.VMEM((FFN_TM, d), F32)],
        compiler_params=_params("parallel", "arbitrary"),
        name="ffn_final" if final_norm else "ffn",
    )(*args)


def _sgu_kernel(x_ref, g_ref, wu_ref, wv_ref, vg_ref, vb_ref, ws_ref, bs_ref, wo_ref,
                o_ref, xn_ref, u_ref, v_ref, mu_ref, rstd_ref, acc_ref):
    j = pl.program_id(1)
    gw = wu_ref.shape[1]
    tm = x_ref.shape[0]

    @pl.when(j == 0)
    def _():
        xn_ref[...] = _rms_normalize(x_ref[...], g_ref[...]).astype(BF16)

    @pl.when(j < SGU_GROUPS)
    def _():
        xn = xn_ref[...]
        u = jax.nn.gelu(jnp.dot(xn, wu_ref[...], preferred_element_type=F32))
        v = jax.nn.gelu(jnp.dot(xn, wv_ref[...], preferred_element_type=F32))
        u_ref[j] = u.astype(BF16)
        v_ref[j] = v

    @pl.when(j == SGU_GROUPS)
    def _():
        width = SGU_GROUPS * gw
        total = jnp.zeros((tm, 1), F32)
        for g in range(SGU_GROUPS):
            total += jnp.sum(v_ref[g], axis=-1, keepdims=True)
        mu = total / width
        sq = jnp.zeros((tm, 1), F32)
        for g in range(SGU_GROUPS):
            dv = v_ref[g] - mu
            sq += jnp.sum(dv * dv, axis=-1, keepdims=True)
        mu_ref[...] = mu
        rstd_ref[...] = lax.rsqrt(sq / width + EPS)

    @pl.when(j >= SGU_GROUPS)
    def _():
        g = j - SGU_GROUPS
        vn = ((v_ref[g] - mu_ref[...]) * rstd_ref[...] * vg_ref[0] + vb_ref[0]).astype(BF16)
        row_chunk = lax.broadcasted_iota(jnp.int32, (SGU_BLOCK, SGU_BLOCK), 0) // CHUNK
        col_chunk = lax.broadcasted_iota(jnp.int32, (SGU_BLOCK, SGU_BLOCK), 1) // CHUNK
        ws = jnp.where(row_chunk >= col_chunk, ws_ref[0], 0.0).astype(BF16)
        bias = bs_ref[0]
        u = u_ref[g]
        gated = []
        for b in range(tm // SGU_BLOCK):
            rows = slice(b * SGU_BLOCK, (b + 1) * SGU_BLOCK)
            mixed = jnp.dot(ws, vn[rows], preferred_element_type=F32) + bias
            gated.append((u[rows].astype(F32) * mixed).astype(BF16))
        gated = jnp.concatenate(gated, axis=0)
        part = jnp.dot(gated, wo_ref[...], preferred_element_type=F32)

        @pl.when(g == 0)
        def _():
            acc_ref[...] = part

        @pl.when(g > 0)
        def _():
            acc_ref[...] += part

    @pl.when(j == pl.num_programs(1) - 1)
    def _():
        o_ref[...] = x_ref[...] + acc_ref[...]


def _sgu(x, gain, w_in, v_gain, v_bias, w_spatial, b_spatial, w_out):
    m, d = x.shape
    width = w_out.shape[0]
    gw = width // SGU_GROUPS
    ng = SGU_GROUPS
    tm = SGU_TM

    def phase1(j):
        return jnp.minimum(j, ng - 1)

    def phase2(j):
        return jnp.maximum(j - ng, 0)

    return pl.pallas_call(
        _sgu_kernel,
        grid=(m // tm, 2 * ng),
        in_specs=[
            pl.BlockSpec((tm, d), lambda i, j: (i, 0)),
            pl.BlockSpec((1, d), lambda i, j: (0, 0)),
            pl.BlockSpec((d, gw), lambda i, j: (0, phase1(j))),
            pl.BlockSpec((d, gw), lambda i, j: (0, ng + phase1(j))),
            pl.BlockSpec((1, 1, gw), lambda i, j: (phase2(j), 0, 0)),
            pl.BlockSpec((1, 1, gw), lambda i, j: (phase2(j), 0, 0)),
            pl.BlockSpec((1, SGU_BLOCK, SGU_BLOCK), lambda i, j: (phase2(j), 0, 0)),
            pl.BlockSpec((1, SGU_BLOCK, 1), lambda i, j: (phase2(j), 0, 0)),
            pl.BlockSpec((gw, d), lambda i, j: (phase2(j), 0)),
        ],
        out_specs=pl.BlockSpec((tm, d), lambda i, j: (i, 0)),
        out_shape=jax.ShapeDtypeStruct((m, d), F32),
        scratch_shapes=[
            pltpu.VMEM((tm, d), BF16),
            pltpu.VMEM((ng, tm, gw), BF16),
            pltpu.VMEM((ng, tm, gw), F32),
            pltpu.VMEM((tm, 1), F32),
            pltpu.VMEM((tm, 1), F32),
            pltpu.VMEM((tm, d), F32),
        ],
        compiler_params=_params("parallel", "arbitrary"),
        name="sgu",
    )(x, gain.reshape(1, d), w_in, w_in,
      v_gain.reshape(ng, 1, gw), v_bias.reshape(ng, 1, gw),
      w_spatial, b_spatial.reshape(ng, SGU_BLOCK, 1), w_out)


def _mla_proj_kernel(x_ref, posc_ref, posr_ref, freqr_ref, freqc_ref, g_ref, wi_ref, gq_ref,
                     wqt_ref, gkv_ref, wkn_ref, wvt_ref, qt_ref, kn_ref, kr_ref, vt_ref):
    half = QK_ROPE // 2
    hn = _rms_normalize(x_ref[...], g_ref[...]).astype(BF16)
    proj = jnp.dot(hn, wi_ref[...], preferred_element_type=F32)
    qn = _rms_normalize(proj[:, :Q_LORA], gq_ref[...]).astype(BF16)
    kvn = _rms_normalize(proj[:, Q_LORA:Q_LORA + KV_LORA], gkv_ref[...]).astype(BF16)
    kr = proj[:, Q_LORA + KV_LORA:]

    ang = posc_ref[...].astype(F32) * freqr_ref[...]
    lane = lax.broadcasted_iota(jnp.int32, ang.shape, 1)
    sin = jnp.sin(ang)
    sin_lo = jnp.where(lane < half, -sin, 0.0)
    sin_hi = jnp.where((lane >= half) & (lane < QK_ROPE), sin, 0.0)
    kr_ref[...] = (kr * jnp.cos(ang) + pltpu.roll(kr, LANES - half, 1) * sin_lo
                   + pltpu.roll(kr, half, 1) * sin_hi).astype(BF16)

    kn_ref[...] = jnp.dot(kvn, wkn_ref[...], preferred_element_type=F32).astype(BF16)
    vt_ref[...] = _dot_nt(wvt_ref[...], kvn).astype(BF16)

    ang_t = freqc_ref[...] * posr_ref[...].astype(F32)
    cos_t = jnp.cos(ang_t)
    sin_t = jnp.sin(ang_t)
    scale = QK_DIM ** -0.5 * math.log2(math.e)
    q_t = _dot_nt(wqt_ref[...], qn) * scale
    for h in range(MLA_HEADS):
        src = h * QK_DIM
        dst = h * HEAD_PAD
        x1 = q_t[src + QK_NOPE:src + QK_NOPE + half]
        x2 = q_t[src + QK_NOPE + half:src + QK_DIM]
        qt_ref[dst:dst + QK_NOPE, :] = q_t[src:src + QK_NOPE].astype(BF16)
        qt_ref[dst + QK_NOPE:dst + QK_NOPE + half, :] = (x1 * cos_t - x2 * sin_t).astype(BF16)
        qt_ref[dst + QK_NOPE + half:dst + QK_DIM, :] = (x1 * sin_t + x2 * cos_t).astype(BF16)
        qt_ref[dst + QK_DIM:dst + HEAD_PAD, :] = jnp.zeros((HEAD_PAD - QK_DIM, q_t.shape[1]), BF16)


def _mla_proj(x, positions, gain, w_in, gq, wqt, gkv, wkn, wvt):
    m, d = x.shape
    tm = MLA_P_TM
    r = ATT_T // tm
    nt = m // ATT_T
    half = QK_ROPE // 2
    inv_freq = 1.0 / (ROPE_THETA ** (jnp.arange(half, dtype=F32) / half))
    freq_row = jnp.concatenate([inv_freq, inv_freq, jnp.zeros((LANES - QK_ROPE,), F32)])
    const = lambda i: (0, 0)

    def resident(shape):
        return pl.BlockSpec(shape, const, pipeline_mode=pl.Buffered(1))

    return pl.pallas_call(
        _mla_proj_kernel,
        grid=(m // tm,),
        in_specs=[
            pl.BlockSpec((tm, d), lambda i: (i, 0)),
            pl.BlockSpec((tm, 1), lambda i: (i, 0)),
            pl.BlockSpec((None, 1, tm), lambda i: (i, 0, 0)),
            resident((1, LANES)),
            resident((half, 1)),
            resident((1, d)),
            resident(w_in.shape),
            resident((1, Q_LORA)),
            resident(wqt.shape),
            resident((1, KV_LORA)),
            resident(wkn.shape),
            resident(wvt.shape),
        ],
        out_specs=[
            pl.BlockSpec((None, MLA_HEADS * HEAD_PAD, tm), lambda i: (i // r, 0, i % r)),
            pl.BlockSpec((tm, MLA_HEADS * QK_NOPE), lambda i: (i, 0)),
            pl.BlockSpec((tm, LANES), lambda i: (i, 0)),
            pl.BlockSpec((None, MLA_HEADS * V_DIM, tm), lambda i: (i // r, 0, i % r)),
        ],
        out_shape=[
            jax.ShapeDtypeStruct((nt, MLA_HEADS * HEAD_PAD, ATT_T), BF16),
            jax.ShapeDtypeStruct((m, MLA_HEADS * QK_NOPE), BF16),
            jax.ShapeDtypeStruct((m, LANES), BF16),
            jax.ShapeDtypeStruct((nt, MLA_HEADS * V_DIM, ATT_T), BF16),
        ],
        compiler_params=_params("parallel"),
        name="mla_proj",
    )(x, positions.reshape(m, 1), positions.reshape(m // tm, 1, tm), freq_row.reshape(1, LANES),
      inv_freq.reshape(half, 1), gain.reshape(1, d), w_in, gq.reshape(1, -1), wqt,
      gkv.reshape(1, -1), wkn, wvt)


def _mla_attn_kernel(qt_ref, kn_ref, kr_ref, vt_ref, o_ref, m_ref, l_ref, acc_ref):
    i = pl.program_id(2)
    t = qt_ref.shape[1]

    m_ref[...] = jnp.full(m_ref.shape, -jnp.inf, F32)
    l_ref[...] = jnp.zeros(l_ref.shape, F32)
    acc_ref[...] = jnp.zeros(acc_ref.shape, F32)

    def block(j, diagonal):
        rows = pl.ds(pl.multiple_of(j * t, t), t)
        kr = kr_ref[rows, :]
        for a in range(ATT_HEADS):
            k = jnp.concatenate([kn_ref[rows, a * QK_NOPE:(a + 1) * QK_NOPE], kr], axis=1)
            s = jnp.dot(k, qt_ref[a * HEAD_PAD:(a + 1) * HEAD_PAD, :],
                        preferred_element_type=F32)
            if diagonal:
                key_chunk = lax.broadcasted_iota(jnp.int32, (t, 1), 0) // CHUNK
                qry_chunk = lax.broadcasted_iota(jnp.int32, (1, t), 1) // CHUNK
                s = jnp.where(key_chunk <= qry_chunk, s, -jnp.inf)
            m_prev = m_ref[a]
            m_new = jnp.maximum(m_prev, jnp.max(s, axis=0, keepdims=True))
            alpha = jnp.exp2(m_prev - m_new)
            p = jnp.exp2(s - m_new)
            l_ref[a] = alpha * l_ref[a] + jnp.sum(p, axis=0, keepdims=True)
            pv = jnp.dot(vt_ref[j, a * V_DIM:(a + 1) * V_DIM, :], p.astype(BF16),
                         preferred_element_type=F32)
            acc_ref[a] = alpha * acc_ref[a] + pv
            m_ref[a] = m_new

    def body(j, carry):
        block(j, diagonal=False)
        return carry

    lax.fori_loop(0, i, body, 0)
    block(i, diagonal=True)

    for a in range(ATT_HEADS):
        o = acc_ref[a] / l_ref[a]
        o_ref[:, a * V_DIM:(a + 1) * V_DIM] = o.T.astype(o_ref.dtype)


def _mla_attn(qt, kn, kr, vt, batch):
    nt, _, t = qt.shape
    nq = nt // batch
    s = nq * t
    ha = ATT_HEADS
    return pl.pallas_call(
        _mla_attn_kernel,
        grid=(batch, MLA_HEADS // ha, nq),
        in_specs=[
            pl.BlockSpec((None, ha * HEAD_PAD, t), lambda b, h, i: (b * nq + i, h, 0)),
            pl.BlockSpec((None, s, ha * QK_NOPE), lambda b, h, i: (b, 0, h)),
            pl.BlockSpec((None, s, LANES), lambda b, h, i: (b, 0, 0)),
            pl.BlockSpec((None, nq, ha * V_DIM, t), lambda b, h, i: (b, 0, h, 0)),
        ],
        out_specs=pl.BlockSpec((None, t, ha * V_DIM), lambda b, h, i: (b, i, h)),
        out_shape=jax.ShapeDtypeStruct((batch, s, MLA_HEADS * V_DIM), BF16),
        scratch_shapes=[
            pltpu.VMEM((ha, 1, t), F32),
            pltpu.VMEM((ha, 1, t), F32),
            pltpu.VMEM((ha, V_DIM, t), F32),
        ],
        compiler_params=_params("parallel", "parallel", "arbitrary"),
        name="mla_attn",
    )(qt, kn.reshape(batch, s, -1), kr.reshape(batch, s, LANES),
      vt.reshape(batch, nq, MLA_HEADS * V_DIM, t))


def _mla_out_kernel(x_ref, o_ref, w_ref, y_ref):
    y_ref[...] = x_ref[...] + jnp.dot(o_ref[...], w_ref[...], preferred_element_type=F32)


def _mla_out(x, o, w):
    m, d = x.shape
    tm = MLA_O_TM
    return pl.pallas_call(
        _mla_out_kernel,
        grid=(m // tm,),
        in_specs=[
            pl.BlockSpec((tm, d), lambda i: (i, 0)),
            pl.BlockSpec((tm, o.shape[1]), lambda i: (i, 0)),
            pl.BlockSpec(w.shape, lambda i: (0, 0)),
        ],
        out_specs=pl.BlockSpec((tm, d), lambda i: (i, 0)),
        out_shape=jax.ShapeDtypeStruct((m, d), F32),
        compiler_params=_params("parallel"),
        name="mla_out",
    )(x, o, w)


def _mla(x, positions, gain, w_in, gq, w_q_up, gkv, w_kv_up, w_out):
    b, _ = positions.shape
    m, d = x.shape
    w_in_p = jnp.pad(w_in, ((0, 0), (0, LANES - QK_ROPE))).astype(BF16)
    wqt = w_q_up.T.astype(BF16)
    w_kv = w_kv_up.reshape(KV_LORA, MLA_HEADS, QK_NOPE + V_DIM)
    wkn = w_kv[:, :, :QK_NOPE].reshape(KV_LORA, -1).astype(BF16)
    wvt = w_kv[:, :, QK_NOPE:].reshape(KV_LORA, -1).T.astype(BF16)
    qt, kn, kr, vt = _mla_proj(x, positions, gain, w_in_p, gq, wqt, gkv, wkn, wvt)
    o = _mla_attn(qt, kn, kr, vt, b)
    return _mla_out(x, o.reshape(m, -1), w_out.astype(BF16))


def kernel(x, positions, ln_ffn1, ffn1_w_in, ffn1_w_out, ln_mix, ln_ffn2, ffn2_w_in, ffn2_w_out,
           sgu_w_in, sgu_v_gain, sgu_v_bias, sgu_w_spatial, sgu_b_spatial, sgu_w_out,
           mla_w_in, mla_q_norm, mla_w_q_up, mla_kv_norm, mla_w_kv_up, mla_w_out, ln_final):
    b, s, d = x.shape
    depth = ln_ffn1.shape[0]
    h = x.reshape(b * s, d)
    for i in range(depth):
        h = _ffn(h, ln_ffn1[i], ffn1_w_in[i].astype(BF16), ffn1_w_out[i].astype(BF16))
        j = i // 2
        if i % 2 == 0:
            h = _sgu(h, ln_mix[i], sgu_w_in[j].astype(BF16), sgu_v_gain[j], sgu_v_bias[j],
                     sgu_w_spatial[j], sgu_b_spatial[j], sgu_w_out[j].astype(BF16))
        else:
            h = _mla(h, positions, ln_mix[i], mla_w_in[j], mla_q_norm[j], mla_w_q_up[j],
                     mla_kv_norm[j], mla_w_kv_up[j], mla_w_out[j])
        last = i == depth - 1
        h = _ffn(h, ln_ffn2[i], ffn2_w_in[i].astype(BF16), ffn2_w_out[i].astype(BF16),
                 final_gain=ln_final if last else None)
    return h.reshape(b, s, d)
```

```python
import functools
import math

import jax
import jax.numpy as jnp
from jax import lax
from jax.experimental import pallas as pl
from jax.experimental.pallas import tpu as pltpu

F32 = jnp.float32
BF16 = jnp.bfloat16

EPS = 1e-6
CHUNK = 64
SGU_BLOCK = 128
SGU_GROUPS = 8
MLA_HEADS = 16
Q_LORA = 512
KV_LORA = 512
QK_NOPE = 128
QK_ROPE = 64
V_DIM = 128
QK_DIM = QK_NOPE + QK_ROPE
ROPE_THETA = 10000.0

LANES = 128
HEAD_PAD = 2 * LANES
VMEM_LIMIT = 61 * 1024 * 1024

FFN_TM = 1024
FFN_TF = 512
FFN_ROWS = 64
FFN_COLS = 512
SGU_TM = 512
SGU_GS = 2
MLA_P_TM = 256
MLA_O_TM = 512
ATT_T = 512
ATT_HEADS = 4


def _rms_normalize(x, gain):
    return x * lax.rsqrt(jnp.mean(x * x, axis=-1, keepdims=True) + EPS) * gain


def _params(*sem):
    return pltpu.CompilerParams(dimension_semantics=sem, vmem_limit_bytes=VMEM_LIMIT)


def _dot_nt(a, b):
    return lax.dot_general(a, b, (((1,), (1,)), ((), ())), preferred_element_type=F32)


def _ffn_kernel(x_ref, g_ref, wg_ref, wu_ref, wo_ref, *rest, final_norm):
    if final_norm:
        gf_ref, o_ref, xn_ref = rest
    else:
        o_ref, xn_ref = rest
    j = pl.program_id(1)

    tm, d = x_ref.shape
    row_chunks = [slice(r, r + FFN_ROWS) for r in range(0, tm, FFN_ROWS)]

    @pl.when(j == 0)
    def _():
        for rows in row_chunks:
            xn_ref[rows, :] = _rms_normalize(x_ref[rows, :], g_ref[...]).astype(BF16)
        o_ref[...] = jnp.zeros(o_ref.shape, F32)

    xn = xn_ref[...]
    gate = jnp.dot(xn, wg_ref[...], preferred_element_type=F32)
    up = jnp.dot(xn, wu_ref[...], preferred_element_type=F32)
    act = (gate * jax.nn.sigmoid(gate) * up).astype(BF16)
    for c in range(0, d, FFN_COLS):
        o_ref[:, c:c + FFN_COLS] += jnp.dot(act, wo_ref[:, c:c + FFN_COLS],
                                            preferred_element_type=F32)

    @pl.when(j == pl.num_programs(1) - 1)
    def _():
        for rows in row_chunks:
            y = x_ref[rows, :] + 0.5 * o_ref[rows, :]
            o_ref[rows, :] = y
            if final_norm:
                rstd = lax.rsqrt(jnp.mean(y * y, axis=-1, keepdims=True) + EPS)
                o_ref[rows, :] = o_ref[rows, :] * rstd * gf_ref[...]


def _ffn(x, gain, w_in, w_out, final_gain=None):
    m, d = x.shape
    f = w_out.shape[0]
    nf = f // FFN_TF
    final_norm = final_gain is not None
    in_specs = [
        pl.BlockSpec((FFN_TM, d), lambda i, j: (i, 0)),
        pl.BlockSpec((1, d), lambda i, j: (0, 0)),
        pl.BlockSpec((d, FFN_TF), lambda i, j: (0, j)),
        pl.BlockSpec((d, FFN_TF), lambda i, j: (0, j + nf)),
        pl.BlockSpec((FFN_TF, d), lambda i, j: (j, 0)),
    ]
    args = [x, gain.reshape(1, d), w_in, w_in, w_out]
    if final_norm:
        in_specs.append(pl.BlockSpec((1, d), lambda i, j: (0, 0)))
        args.append(final_gain.reshape(1, d))
    return pl.pallas_call(
        functools.partial(_ffn_kernel, final_norm=final_norm),
        grid=(m // FFN_TM, nf),
        in_specs=in_specs,
        out_specs=pl.BlockSpec((FFN_TM, d), lambda i, j: (i, 0)),
        out_shape=jax.ShapeDtypeStruct((m, d), F32),
        scratch_shapes=[pltpu.VMEM((FFN_TM, d), BF16)],
        compiler_params=_params("parallel", "arbitrary"),
        name="ffn_final" if final_norm else "ffn",
    )(*args)


def _sgu_kernel(x_ref, g_ref, wu_ref, wv_ref, vg_ref, vb_ref, ws_ref, bs_ref, wo_ref,
                o_ref, xn_ref, u_ref, v_ref, mu_ref, rstd_ref):
    j = pl.program_id(1)
    ns = SGU_GROUPS // SGU_GS
    cw = wu_ref.shape[1]
    gw = cw // SGU_GS
    nl = cw // LANES
    tm = x_ref.shape[0]
    width = SGU_GROUPS * gw

    @pl.when(j == 0)
    def _():
        xn_ref[...] = _rms_normalize(x_ref[...], g_ref[...]).astype(BF16)
        o_ref[...] = jnp.zeros(o_ref.shape, F32)

    @pl.when(j < ns)
    def _():
        xn = xn_ref[...]
        u = jax.nn.gelu(jnp.dot(xn, wu_ref[...], preferred_element_type=F32))
        v = jax.nn.gelu(jnp.dot(xn, wv_ref[...], preferred_element_type=F32))
        u_ref[j] = u.astype(BF16)
        v_ref[j] = v

    @pl.when(j == ns)
    def _():
        total = jnp.zeros((tm, LANES), F32)
        for s in range(ns):
            for c in range(nl):
                total += v_ref[s, :, c * LANES:(c + 1) * LANES]
        mu = jnp.broadcast_to(jnp.sum(total, axis=-1, keepdims=True) / width, (tm, LANES))
        sq = jnp.zeros((tm, LANES), F32)
        for s in range(ns):
            for c in range(nl):
                dv = v_ref[s, :, c * LANES:(c + 1) * LANES] - mu
                sq += dv * dv
        var = jnp.sum(sq, axis=-1, keepdims=True) / width
        mu_ref[...] = mu
        rstd_ref[...] = jnp.broadcast_to(lax.rsqrt(var + EPS), (tm, LANES))

    @pl.when(j >= ns)
    def _():
        s = j - ns
        mu = pltpu.repeat(mu_ref[...], nl, 1)
        rstd = pltpu.repeat(rstd_ref[...], nl, 1)
        vn = ((v_ref[s] - mu) * rstd * vg_ref[0] + vb_ref[0]).astype(BF16)
        u = u_ref[s]
        row_chunk = lax.broadcasted_iota(jnp.int32, (SGU_BLOCK, SGU_BLOCK), 0) // CHUNK
        col_chunk = lax.broadcasted_iota(jnp.int32, (SGU_BLOCK, SGU_BLOCK), 1) // CHUNK
        visible = row_chunk >= col_chunk
        cols = []
        for gi in range(SGU_GS):
            ws = jnp.where(visible, ws_ref[gi], 0.0).astype(BF16)
            bias = pltpu.repeat(bs_ref[gi], gw // LANES, 1)
            blocks = []
            for b in range(tm // SGU_BLOCK):
                rows = slice(b * SGU_BLOCK, (b + 1) * SGU_BLOCK)
                lanes = slice(gi * gw, (gi + 1) * gw)
                mixed = jnp.dot(ws, vn[rows, lanes], preferred_element_type=F32) + bias
                blocks.append((u[rows, lanes].astype(F32) * mixed).astype(BF16))
            cols.append(jnp.concatenate(blocks, axis=0))
        gated = jnp.concatenate(cols, axis=1)
        o_ref[...] += jnp.dot(gated, wo_ref[...], preferred_element_type=F32)

    @pl.when(j == pl.num_programs(1) - 1)
    def _():
        o_ref[...] += x_ref[...]


def _sgu(x, gain, w_in, v_gain, v_bias, w_spatial, b_spatial, w_out):
    m, d = x.shape
    width = w_out.shape[0]
    ns = SGU_GROUPS // SGU_GS
    cw = width // ns
    tm = SGU_TM

    def phase1(j):
        return jnp.minimum(j, ns - 1)

    def phase2(j):
        return jnp.maximum(j - ns, 0)

    bias = jnp.broadcast_to(b_spatial[:, :, None], (SGU_GROUPS, SGU_BLOCK, LANES))
    return pl.pallas_call(
        _sgu_kernel,
        grid=(m // tm, 2 * ns),
        in_specs=[
            pl.BlockSpec((tm, d), lambda i, j: (i, 0)),
            pl.BlockSpec((1, d), lambda i, j: (0, 0)),
            pl.BlockSpec((d, cw), lambda i, j: (0, phase1(j))),
            pl.BlockSpec((d, cw), lambda i, j: (0, ns + phase1(j))),
            pl.BlockSpec((1, 1, cw), lambda i, j: (phase2(j), 0, 0)),
            pl.BlockSpec((1, 1, cw), lambda i, j: (phase2(j), 0, 0)),
            pl.BlockSpec((SGU_GS, SGU_BLOCK, SGU_BLOCK), lambda i, j: (phase2(j), 0, 0)),
            pl.BlockSpec((SGU_GS, SGU_BLOCK, LANES), lambda i, j: (phase2(j), 0, 0)),
            pl.BlockSpec((cw, d), lambda i, j: (phase2(j), 0)),
        ],
        out_specs=pl.BlockSpec((tm, d), lambda i, j: (i, 0)),
        out_shape=jax.ShapeDtypeStruct((m, d), F32),
        scratch_shapes=[
            pltpu.VMEM((tm, d), BF16),
            pltpu.VMEM((ns, tm, cw), BF16),
            pltpu.VMEM((ns, tm, cw), F32),
            pltpu.VMEM((tm, LANES), F32),
            pltpu.VMEM((tm, LANES), F32),
        ],
        compiler_params=_params("parallel", "arbitrary"),
        name="sgu",
    )(x, gain.reshape(1, d), w_in, w_in,
      v_gain.reshape(ns, 1, cw), v_bias.reshape(ns, 1, cw),
      w_spatial, bias, w_out)


def _mla_proj_kernel(x_ref, posc_ref, posr_ref, freqr_ref, freqc_ref, g_ref, wi_ref, gq_ref,
                     wqt_ref, gkv_ref, wkn_ref, wvt_ref, qt_ref, kn_ref, kr_ref, vt_ref):
    half = QK_ROPE // 2
    hn = _rms_normalize(x_ref[...], g_ref[...]).astype(BF16)
    proj = jnp.dot(hn, wi_ref[...], preferred_element_type=F32)
    qn = _rms_normalize(proj[:, :Q_LORA], gq_ref[...]).astype(BF16)
    kvn = _rms_normalize(proj[:, Q_LORA:Q_LORA + KV_LORA], gkv_ref[...]).astype(BF16)
    kr = proj[:, Q_LORA + KV_LORA:]

    ang = posc_ref[...].astype(F32) * freqr_ref[...]
    lane = lax.broadcasted_iota(jnp.int32, ang.shape, 1)
    sin = jnp.sin(ang)
    sin_lo = jnp.where(lane < half, -sin, 0.0)
    sin_hi = jnp.where((lane >= half) & (lane < QK_ROPE), sin, 0.0)
    kr_ref[...] = (kr * jnp.cos(ang) + pltpu.roll(kr, LANES - half, 1) * sin_lo
                   + pltpu.roll(kr, half, 1) * sin_hi).astype(BF16)

    kn_ref[...] = jnp.dot(kvn, wkn_ref[...], preferred_element_type=F32).astype(BF16)
    vt_ref[...] = _dot_nt(wvt_ref[...], kvn).astype(BF16)

    ang_t = freqc_ref[...] * posr_ref[...].astype(F32)
    cos_t = jnp.cos(ang_t)
    sin_t = jnp.sin(ang_t)
    scale = QK_DIM ** -0.5 * math.log2(math.e)
    q_t = _dot_nt(wqt_ref[...], qn) * scale
    for h in range(MLA_HEADS):
        src = h * QK_DIM
        dst = h * HEAD_PAD
        x1 = q_t[src + QK_NOPE:src + QK_NOPE + half]
        x2 = q_t[src + QK_NOPE + half:src + QK_DIM]
        qt_ref[dst:dst + QK_NOPE, :] = q_t[src:src + QK_NOPE].astype(BF16)
        qt_ref[dst + QK_NOPE:dst + QK_NOPE + half, :] = (x1 * cos_t - x2 * sin_t).astype(BF16)
        qt_ref[dst + QK_NOPE + half:dst + QK_DIM, :] = (x1 * sin_t + x2 * cos_t).astype(BF16)
        qt_ref[dst + QK_DIM:dst + HEAD_PAD, :] = jnp.zeros((HEAD_PAD - QK_DIM, q_t.shape[1]), BF16)


def _mla_proj(x, positions, gain, w_in, gq, wqt, gkv, wkn, wvt):
    m, d = x.shape
    tm = MLA_P_TM
    r = ATT_T // tm
    nt = m // ATT_T
    half = QK_ROPE // 2
    inv_freq = 1.0 / (ROPE_THETA ** (jnp.arange(half, dtype=F32) / half))
    freq_row = jnp.concatenate([inv_freq, inv_freq, jnp.zeros((LANES - QK_ROPE,), F32)])
    const = lambda i: (0, 0)

    def resident(shape):
        return pl.BlockSpec(shape, const, pipeline_mode=pl.Buffered(1))

    return pl.pallas_call(
        _mla_proj_kernel,
        grid=(m // tm,),
        in_specs=[
            pl.BlockSpec((tm, d), lambda i: (i, 0)),
            pl.BlockSpec((tm, 1), lambda i: (i, 0)),
            pl.BlockSpec((None, 1, tm), lambda i: (i, 0, 0)),
            resident((1, LANES)),
            resident((half, 1)),
            resident((1, d)),
            resident(w_in.shape),
            resident((1, Q_LORA)),
            resident(wqt.shape),
            resident((1, KV_LORA)),
            resident(wkn.shape),
            resident(wvt.shape),
        ],
        out_specs=[
            pl.BlockSpec((None, MLA_HEADS * HEAD_PAD, tm), lambda i: (i // r, 0, i % r)),
            pl.BlockSpec((tm, MLA_HEADS * QK_NOPE), lambda i: (i, 0)),
            pl.BlockSpec((tm, LANES), lambda i: (i, 0)),
            pl.BlockSpec((None, MLA_HEADS * V_DIM, tm), lambda i: (i // r, 0, i % r)),
        ],
        out_shape=[
            jax.ShapeDtypeStruct((nt, MLA_HEADS * HEAD_PAD, ATT_T), BF16),
            jax.ShapeDtypeStruct((m, MLA_HEADS * QK_NOPE), BF16),
            jax.ShapeDtypeStruct((m, LANES), BF16),
            jax.ShapeDtypeStruct((nt, MLA_HEADS * V_DIM, ATT_T), BF16),
        ],
        compiler_params=_params("parallel"),
        name="mla_proj",
    )(x, positions.reshape(m, 1), positions.reshape(m // tm, 1, tm), freq_row.reshape(1, LANES),
      inv_freq.reshape(half, 1), gain.reshape(1, d), w_in, gq.reshape(1, -1), wqt,
      gkv.reshape(1, -1), wkn, wvt)


def _mla_attn_kernel(qt_ref, kn_ref, kr_ref, vt_ref, o_ref, m_ref, l_ref, acc_ref):
    i = pl.program_id(2)
    t = qt_ref.shape[1]

    m_ref[...] = jnp.full(m_ref.shape, -jnp.inf, F32)
    l_ref[...] = jnp.zeros(l_ref.shape, F32)
    acc_ref[...] = jnp.zeros(acc_ref.shape, F32)

    def block(j, diagonal):
        rows = pl.ds(pl.multiple_of(j * t, t), t)
        kr = kr_ref[rows, :]
        for a in range(ATT_HEADS):
            k = jnp.concatenate([kn_ref[rows, a * QK_NOPE:(a + 1) * QK_NOPE], kr], axis=1)
            s = jnp.dot(k, qt_ref[a * HEAD_PAD:(a + 1) * HEAD_PAD, :],
                        preferred_element_type=F32)
            if diagonal:
                key_chunk = lax.broadcasted_iota(jnp.int32, (t, 1), 0) // CHUNK
                qry_chunk = lax.broadcasted_iota(jnp.int32, (1, t), 1) // CHUNK
                s = jnp.where(key_chunk <= qry_chunk, s, -jnp.inf)
            m_prev = m_ref[a]
            m_new = jnp.maximum(m_prev, jnp.max(s, axis=0, keepdims=True))
            alpha = jnp.exp2(m_prev - m_new)
            p = jnp.exp2(s - m_new)
            l_ref[a] = alpha * l_ref[a] + jnp.sum(p, axis=0, keepdims=True)
            pv = jnp.dot(vt_ref[j, a * V_DIM:(a + 1) * V_DIM, :], p.astype(BF16),
                         preferred_element_type=F32)
            acc_ref[a] = alpha * acc_ref[a] + pv
            m_ref[a] = m_new

    def body(j, carry):
        block(j, diagonal=False)
        return carry

    lax.fori_loop(0, i, body, 0)
    block(i, diagonal=True)

    for a in range(ATT_HEADS):
        o = acc_ref[a] / l_ref[a]
        o_ref[:, a * V_DIM:(a + 1) * V_DIM] = o.T.astype(o_ref.dtype)


def _mla_attn(qt, kn, kr, vt, batch):
    nt, _, t = qt.shape
    nq = nt // batch
    s = nq * t
    ha = ATT_HEADS
    return pl.pallas_call(
        _mla_attn_kernel,
        grid=(batch, MLA_HEADS // ha, nq),
        in_specs=[
            pl.BlockSpec((None, ha * HEAD_PAD, t), lambda b, h, i: (b * nq + i, h, 0)),
            pl.BlockSpec((None, s, ha * QK_NOPE), lambda b, h, i: (b, 0, h)),
            pl.BlockSpec((None, s, LANES), lambda b, h, i: (b, 0, 0)),
            pl.BlockSpec((None, nq, ha * V_DIM, t), lambda b, h, i: (b, 0, h, 0)),
        ],
        out_specs=pl.BlockSpec((None, t, ha * V_DIM), lambda b, h, i: (b, i, h)),
        out_shape=jax.ShapeDtypeStruct((batch, s, MLA_HEADS * V_DIM), BF16),
        scratch_shapes=[
            pltpu.VMEM((ha, 1, t), F32),
            pltpu.VMEM((ha, 1, t), F32),
            pltpu.VMEM((ha, V_DIM, t), F32),
        ],
        compiler_params=_params("parallel", "parallel", "arbitrary"),
        name="mla_attn",
    )(qt, kn.reshape(batch, s, -1), kr.reshape(batch, s, LANES),
      vt.reshape(batch, nq, MLA_HEADS * V_DIM, t))


def _mla_out_kernel(x_ref, o_ref, w_ref, y_ref):
    y_ref[...] = x_ref[...] + jnp.dot(o_ref[...], w_ref[...], preferred_element_type=F32)


def _mla_out(x, o, w):
    m, d = x.shape
    tm = MLA_O_TM
    return pl.pallas_call(
        _mla_out_kernel,
        grid=(m // tm,),
        in_specs=[
            pl.BlockSpec((tm, d), lambda i: (i, 0)),
            pl.BlockSpec((tm, o.shape[1]), lambda i: (i, 0)),
            pl.BlockSpec(w.shape, lambda i: (0, 0)),
        ],
        out_specs=pl.BlockSpec((tm, d), lambda i: (i, 0)),
        out_shape=jax.ShapeDtypeStruct((m, d), F32),
        compiler_params=_params("parallel"),
        name="mla_out",
    )(x, o, w)


def _mla(x, positions, gain, w_in, gq, w_q_up, gkv, w_kv_up, w_out):
    b, _ = positions.shape
    m, d = x.shape
    w_in_p = jnp.pad(w_in, ((0, 0), (0, LANES - QK_ROPE))).astype(BF16)
    wqt = w_q_up.T.astype(BF16)
    w_kv = w_kv_up.reshape(KV_LORA, MLA_HEADS, QK_NOPE + V_DIM)
    wkn = w_kv[:, :, :QK_NOPE].reshape(KV_LORA, -1).astype(BF16)
    wvt = w_kv[:, :, QK_NOPE:].reshape(KV_LORA, -1).T.astype(BF16)
    qt, kn, kr, vt = _mla_proj(x, positions, gain, w_in_p, gq, wqt, gkv, wkn, wvt)
    o = _mla_attn(qt, kn, kr, vt, b)
    return _mla_out(x, o.reshape(m, -1), w_out.astype(BF16))


def kernel(x, positions, ln_ffn1, ffn1_w_in, ffn1_w_out, ln_mix, ln_ffn2, ffn2_w_in, ffn2_w_out,
           sgu_w_in, sgu_v_gain, sgu_v_bias, sgu_w_spatial, sgu_b_spatial, sgu_w_out,
           mla_w_in, mla_q_norm, mla_w_q_up, mla_kv_norm, mla_w_kv_up, mla_w_out, ln_final):
    b, s, d = x.shape
    depth = ln_ffn1.shape[0]
    h = x.reshape(b * s, d)
    for i in range(depth):
        h = _ffn(h, ln_ffn1[i], ffn1_w_in[i].astype(BF16), ffn1_w_out[i].astype(BF16))
        j = i // 2
        if i % 2 == 0:
            h = _sgu(h, ln_mix[i], sgu_w_in[j].astype(BF16), sgu_v_gain[j], sgu_v_bias[j],
                     sgu_w_spatial[j], sgu_b_spatial[j], sgu_w_out[j].astype(BF16))
        else:
            h = _mla(h, positions, ln_mix[i], mla_w_in[j], mla_q_norm[j], mla_w_q_up[j],
                     mla_kv_norm[j], mla_w_kv_up[j], mla_w_out[j])
        last = i == depth - 1
        h = _ffn(h, ln_ffn2[i], ffn2_w_in[i].astype(BF16), ffn2_w_out[i].astype(BF16),
                 final_gain=ln_final if last else None)
    return h.reshape(b, s, d)
```

```python
import functools
import math

import jax
import jax.numpy as jnp
from jax import lax
from jax.experimental import pallas as pl
from jax.experimental.pallas import tpu as pltpu

F32 = jnp.float32
BF16 = jnp.bfloat16

EPS = 1e-6
CHUNK = 64
SGU_BLOCK = 128
SGU_GROUPS = 8
MLA_HEADS = 16
Q_LORA = 512
KV_LORA = 512
QK_NOPE = 128
QK_ROPE = 64
V_DIM = 128
QK_DIM = QK_NOPE + QK_ROPE
ROPE_THETA = 10000.0

LANES = 128
HEAD_PAD = 2 * LANES
VMEM_LIMIT = 61 * 1024 * 1024

FFN_TM = 1024
FFN_TF = 512
FFN_ROWS = 64
FFN_COLS = 512
SGU_TM = 512
SGU_GS = 2
MLA_P_TM = 256
MLA_O_TM = 512
ATT_T = 512
ATT_HEADS = 4


def _rms_normalize(x, gain):
    return x * lax.rsqrt(jnp.mean(x * x, axis=-1, keepdims=True) + EPS) * gain


def _params(*sem):
    return pltpu.CompilerParams(dimension_semantics=sem, vmem_limit_bytes=VMEM_LIMIT)


def _dot_nt(a, b):
    return lax.dot_general(a, b, (((1,), (1,)), ((), ())), preferred_element_type=F32)


def _ffn_kernel(x_ref, g_ref, wg_ref, wu_ref, wo_ref, *rest, final_norm):
    if final_norm:
        gf_ref, o_ref, xn_ref = rest
    else:
        o_ref, xn_ref = rest
    j = pl.program_id(1)

    tm, d = x_ref.shape
    row_chunks = [slice(r, r + FFN_ROWS) for r in range(0, tm, FFN_ROWS)]

    @pl.when(j == 0)
    def _():
        for rows in row_chunks:
            xn_ref[rows, :] = _rms_normalize(x_ref[rows, :], g_ref[...]).astype(BF16)
        o_ref[...] = jnp.zeros(o_ref.shape, F32)

    xn = xn_ref[...]
    gate = jnp.dot(xn, wg_ref[...], preferred_element_type=F32)
    up = jnp.dot(xn, wu_ref[...], preferred_element_type=F32)
    act = (gate * jax.nn.sigmoid(gate) * up).astype(BF16)
    for c in range(0, d, FFN_COLS):
        o_ref[:, c:c + FFN_COLS] += jnp.dot(act, wo_ref[:, c:c + FFN_COLS],
                                            preferred_element_type=F32)

    @pl.when(j == pl.num_programs(1) - 1)
    def _():
        for rows in row_chunks:
            y = x_ref[rows, :] + 0.5 * o_ref[rows, :]
            o_ref[rows, :] = y
            if final_norm:
                rstd = lax.rsqrt(jnp.mean(y * y, axis=-1, keepdims=True) + EPS)
                o_ref[rows, :] = o_ref[rows, :] * rstd * gf_ref[...]


def _ffn(x, gain, w_in, w_out, final_gain=None):
    m, d = x.shape
    f = w_out.shape[0]
    nf = f // FFN_TF
    final_norm = final_gain is not None
    in_specs = [
        pl.BlockSpec((FFN_TM, d), lambda i, j: (i, 0)),
        pl.BlockSpec((1, d), lambda i, j: (0, 0)),
        pl.BlockSpec((d, FFN_TF), lambda i, j: (0, j)),
        pl.BlockSpec((d, FFN_TF), lambda i, j: (0, j + nf)),
        pl.BlockSpec((FFN_TF, d), lambda i, j: (j, 0)),
    ]
    args = [x, gain.reshape(1, d), w_in, w_in, w_out]
    if final_norm:
        in_specs.append(pl.BlockSpec((1, d), lambda i, j: (0, 0)))
        args.append(final_gain.reshape(1, d))
    return pl.pallas_call(
        functools.partial(_ffn_kernel, final_norm=final_norm),
        grid=(m // FFN_TM, nf),
        in_specs=in_specs,
        out_specs=pl.BlockSpec((FFN_TM, d), lambda i, j: (i, 0)),
        out_shape=jax.ShapeDtypeStruct((m, d), F32),
        scratch_shapes=[pltpu.VMEM((FFN_TM, d), BF16)],
        compiler_params=_params("parallel", "arbitrary"),
        name="ffn_final" if final_norm else "ffn",
    )(*args)


def _sgu_kernel(x_ref, g_ref, wu_ref, wv_ref, vg_ref, vb_ref, ws_ref, bs_ref, wo_ref,
                o_ref, xn_ref, u_ref, v_ref, mu_ref, rstd_ref):
    j = pl.program_id(1)
    ns = SGU_GROUPS // SGU_GS
    cw = wu_ref.shape[1]
    gw = cw // SGU_GS
    nl = cw // LANES
    tm = x_ref.shape[0]
    width = SGU_GROUPS * gw

    @pl.when(j == 0)
    def _():
        xn_ref[...] = _rms_normalize(x_ref[...], g_ref[...]).astype(BF16)
        o_ref[...] = jnp.zeros(o_ref.shape, F32)

    @pl.when(j < ns)
    def _():
        xn = xn_ref[...]
        u = jax.nn.gelu(jnp.dot(xn, wu_ref[...], preferred_element_type=F32))
        v = jax.nn.gelu(jnp.dot(xn, wv_ref[...], preferred_element_type=F32))
        u_ref[j] = u.astype(BF16)
        v_ref[j] = v

    @pl.when(j == ns)
    def _():
        total = jnp.zeros((tm, LANES), F32)
        for s in range(ns):
            for c in range(nl):
                total += v_ref[s, :, c * LANES:(c + 1) * LANES]
        mu = jnp.broadcast_to(jnp.sum(total, axis=-1, keepdims=True) / width, (tm, LANES))
        sq = jnp.zeros((tm, LANES), F32)
        for s in range(ns):
            for c in range(nl):
                dv = v_ref[s, :, c * LANES:(c + 1) * LANES] - mu
                sq += dv * dv
        var = jnp.sum(sq, axis=-1, keepdims=True) / width
        mu_ref[...] = mu
        rstd_ref[...] = jnp.broadcast_to(lax.rsqrt(var + EPS), (tm, LANES))

    @pl.when(j >= ns)
    def _():
        s = j - ns
        mu = pltpu.repeat(mu_ref[...], nl, 1)
        rstd = pltpu.repeat(rstd_ref[...], nl, 1)
        vn = ((v_ref[s] - mu) * rstd * vg_ref[0] + vb_ref[0]).astype(BF16)
        u = u_ref[s]
        row_chunk = lax.broadcasted_iota(jnp.int32, (SGU_BLOCK, SGU_BLOCK), 0) // CHUNK
        col_chunk = lax.broadcasted_iota(jnp.int32, (SGU_BLOCK, SGU_BLOCK), 1) // CHUNK
        visible = row_chunk >= col_chunk
        cols = []
        for gi in range(SGU_GS):
            ws = jnp.where(visible, ws_ref[gi], 0.0).astype(BF16)
            bias = pltpu.repeat(bs_ref[gi], gw // LANES, 1)
            blocks = []
            for b in range(tm // SGU_BLOCK):
                rows = slice(b * SGU_BLOCK, (b + 1) * SGU_BLOCK)
                lanes = slice(gi * gw, (gi + 1) * gw)
                mixed = jnp.dot(ws, vn[rows, lanes], preferred_element_type=F32) + bias
                blocks.append((u[rows, lanes].astype(F32) * mixed).astype(BF16))
            cols.append(jnp.concatenate(blocks, axis=0))
        gated = jnp.concatenate(cols, axis=1)
        o_ref[...] += jnp.dot(gated, wo_ref[...], preferred_element_type=F32)

    @pl.when(j == pl.num_programs(1) - 1)
    def _():
        o_ref[...] += x_ref[...]


def _sgu(x, gain, w_in, v_gain, v_bias, w_spatial, b_spatial, w_out):
    m, d = x.shape
    width = w_out.shape[0]
    ns = SGU_GROUPS // SGU_GS
    cw = width // ns
    tm = SGU_TM

    def phase1(j):
        return jnp.minimum(j, ns - 1)

    def phase2(j):
        return jnp.maximum(j - ns, 0)

    bias = jnp.broadcast_to(b_spatial[:, :, None], (SGU_GROUPS, SGU_BLOCK, LANES))
    return pl.pallas_call(
        _sgu_kernel,
        grid=(m // tm, 2 * ns),
        in_specs=[
            pl.BlockSpec((tm, d), lambda i, j: (i, 0)),
            pl.BlockSpec((1, d), lambda i, j: (0, 0)),
            pl.BlockSpec((d, cw), lambda i, j: (0, phase1(j))),
            pl.BlockSpec((d, cw), lambda i, j: (0, ns + phase1(j))),
            pl.BlockSpec((1, 1, cw), lambda i, j: (phase2(j), 0, 0)),
            pl.BlockSpec((1, 1, cw), lambda i, j: (phase2(j), 0, 0)),
            pl.BlockSpec((SGU_GS, SGU_BLOCK, SGU_BLOCK), lambda i, j: (phase2(j), 0, 0)),
            pl.BlockSpec((SGU_GS, SGU_BLOCK, LANES), lambda i, j: (phase2(j), 0, 0)),
            pl.BlockSpec((cw, d), lambda i, j: (phase2(j), 0)),
        ],
        out_specs=pl.BlockSpec((tm, d), lambda i, j: (i, 0)),
        out_shape=jax.ShapeDtypeStruct((m, d), F32),
        scratch_shapes=[
            pltpu.VMEM((tm, d), BF16),
            pltpu.VMEM((ns, tm, cw), BF16),
            pltpu.VMEM((ns, tm, cw), F32),
            pltpu.VMEM((tm, LANES), F32),
            pltpu.VMEM((tm, LANES), F32),
        ],
        compiler_params=_params("parallel", "arbitrary"),
        name="sgu",
    )(x, gain.reshape(1, d), w_in, w_in,
      v_gain.reshape(ns, 1, cw), v_bias.reshape(ns, 1, cw),
      w_spatial, bias, w_out)


def _mla_proj_kernel(x_ref, posc_ref, posr_ref, freqr_ref, freqc_ref, g_ref, wi_ref, gq_ref,
                     wqt_ref, gkv_ref, wkn_ref, wvt_ref, qt_ref, kn_ref, kr_ref, vt_ref):
    half = QK_ROPE // 2
    hn = _rms_normalize(x_ref[...], g_ref[...]).astype(BF16)
    proj = jnp.dot(hn, wi_ref[...], preferred_element_type=F32)
    qn = _rms_normalize(proj[:, :Q_LORA], gq_ref[...]).astype(BF16)
    kvn = _rms_normalize(proj[:, Q_LORA:Q_LORA + KV_LORA], gkv_ref[...]).astype(BF16)
    kr = proj[:, Q_LORA + KV_LORA:]

    ang = posc_ref[...].astype(F32) * freqr_ref[...]
    lane = lax.broadcasted_iota(jnp.int32, ang.shape, 1)
    sin = jnp.sin(ang)
    sin_lo = jnp.where(lane < half, -sin, 0.0)
    sin_hi = jnp.where((lane >= half) & (lane < QK_ROPE), sin, 0.0)
    kr_ref[...] = (kr * jnp.cos(ang) + pltpu.roll(kr, LANES - half, 1) * sin_lo
                   + pltpu.roll(kr, half, 1) * sin_hi).astype(BF16)

    kn_ref[...] = jnp.dot(kvn, wkn_ref[...], preferred_element_type=F32).astype(BF16)
    vt_ref[...] = _dot_nt(wvt_ref[...], kvn).astype(BF16)

    ang_t = freqc_ref[...] * posr_ref[...].astype(F32)
    cos_t = jnp.cos(ang_t)
    sin_t = jnp.sin(ang_t)
    scale = QK_DIM ** -0.5 * math.log2(math.e)
    q_t = _dot_nt(wqt_ref[...], qn) * scale
    for h in range(MLA_HEADS):
        src = h * QK_DIM
        dst = h * HEAD_PAD
        x1 = q_t[src + QK_NOPE:src + QK_NOPE + half]
        x2 = q_t[src + QK_NOPE + half:src + QK_DIM]
        qt_ref[dst:dst + QK_NOPE, :] = q_t[src:src + QK_NOPE].astype(BF16)
        qt_ref[dst + QK_NOPE:dst + QK_NOPE + half, :] = (x1 * cos_t - x2 * sin_t).astype(BF16)
        qt_ref[dst + QK_NOPE + half:dst + QK_DIM, :] = (x1 * sin_t + x2 * cos_t).astype(BF16)
        qt_ref[dst + QK_DIM:dst + HEAD_PAD, :] = jnp.zeros((HEAD_PAD - QK_DIM, q_t.shape[1]), BF16)


def _mla_proj(x, positions, gain, w_in, gq, wqt, gkv, wkn, wvt):
    m, d = x.shape
    tm = MLA_P_TM
    r = ATT_T // tm
    nt = m // ATT_T
    half = QK_ROPE // 2
    inv_freq = 1.0 / (ROPE_THETA ** (jnp.arange(half, dtype=F32) / half))
    freq_row = jnp.concatenate([inv_freq, inv_freq, jnp.zeros((LANES - QK_ROPE,), F32)])
    const = lambda i: (0, 0)

    def resident(shape):
        return pl.BlockSpec(shape, const, pipeline_mode=pl.Buffered(1))

    return pl.pallas_call(
        _mla_proj_kernel,
        grid=(m // tm,),
        in_specs=[
            pl.BlockSpec((tm, d), lambda i: (i, 0)),
            pl.BlockSpec((tm, 1), lambda i: (i, 0)),
            pl.BlockSpec((None, 1, tm), lambda i: (i, 0, 0)),
            resident((1, LANES)),
            resident((half, 1)),
            resident((1, d)),
            resident(w_in.shape),
            resident((1, Q_LORA)),
            resident(wqt.shape),
            resident((1, KV_LORA)),
            resident(wkn.shape),
            resident(wvt.shape),
        ],
        out_specs=[
            pl.BlockSpec((None, MLA_HEADS * HEAD_PAD, tm), lambda i: (i // r, 0, i % r)),
            pl.BlockSpec((tm, MLA_HEADS * QK_NOPE), lambda i: (i, 0)),
            pl.BlockSpec((tm, LANES), lambda i: (i, 0)),
            pl.BlockSpec((None, MLA_HEADS * V_DIM, tm), lambda i: (i // r, 0, i % r)),
        ],
        out_shape=[
            jax.ShapeDtypeStruct((nt, MLA_HEADS * HEAD_PAD, ATT_T), BF16),
            jax.ShapeDtypeStruct((m, MLA_HEADS * QK_NOPE), BF16),
            jax.ShapeDtypeStruct((m, LANES), BF16),
            jax.ShapeDtypeStruct((nt, MLA_HEADS * V_DIM, ATT_T), BF16),
        ],
        compiler_params=_params("parallel"),
        name="mla_proj",
    )(x, positions.reshape(m, 1), positions.reshape(m // tm, 1, tm), freq_row.reshape(1, LANES),
      inv_freq.reshape(half, 1), gain.reshape(1, d), w_in, gq.reshape(1, -1), wqt,
      gkv.reshape(1, -1), wkn, wvt)


def _mla_attn_kernel(qt_ref, kn_ref, kr_ref, vt_ref, o_ref,
                     m_ref, l_ref, acc_ref, s_ref, p_ref, alpha_ref):
    i = pl.program_id(2)
    t = qt_ref.shape[1]
    last = ATT_HEADS - 1

    def visible_only(s):
        key_chunk = lax.broadcasted_iota(jnp.int32, (t, 1), 0) // CHUNK
        qry_chunk = lax.broadcasted_iota(jnp.int32, (1, t), 1) // CHUNK
        return jnp.where(key_chunk <= qry_chunk, s, -jnp.inf)

    def scores(j, a):
        rows = pl.ds(pl.multiple_of(j * t, t), t)
        k = jnp.concatenate([kn_ref[rows, a * QK_NOPE:(a + 1) * QK_NOPE], kr_ref[rows, :]], axis=1)
        return jnp.dot(k, qt_ref[a * HEAD_PAD:(a + 1) * HEAD_PAD, :],
                       preferred_element_type=F32)

    def weighted_values(j, a, p, alpha):
        pv = jnp.dot(vt_ref[j, a * V_DIM:(a + 1) * V_DIM, :], p,
                     preferred_element_type=F32)
        acc_ref[a] = alpha * acc_ref[a] + pv

    def softmax(a, s):
        m_prev = m_ref[a]
        m_new = jnp.maximum(m_prev, jnp.max(s, axis=0, keepdims=True))
        alpha = jnp.exp2(m_prev - m_new)
        p = jnp.exp2(s - m_new)
        l_ref[a] = alpha * l_ref[a] + jnp.sum(p, axis=0, keepdims=True)
        m_ref[a] = m_new
        return p.astype(BF16), alpha

    def key_tile(j, diagonal):
        s_cur = s_ref[...]
        p_prev, alpha_prev = p_ref[...], alpha_ref[...]
        for a in range(ATT_HEADS):
            if a < last:
                s_nxt = scores(j, a + 1)
                if diagonal:
                    s_nxt = visible_only(s_nxt)
            elif not diagonal:
                s_nxt = scores(j + 1, 0)
            weighted_values(jnp.maximum(j - 1, 0) if a == 0 else j, (a - 1) % ATT_HEADS,
                            p_prev, alpha_prev)
            p_prev, alpha_prev = softmax(a, s_cur)
            s_cur = s_nxt
        return s_nxt, p_prev, alpha_prev

    m_ref[...] = jnp.full(m_ref.shape, -jnp.inf, F32)
    l_ref[...] = jnp.zeros(l_ref.shape, F32)
    acc_ref[...] = jnp.zeros(acc_ref.shape, F32)
    p_ref[...] = jnp.zeros(p_ref.shape, BF16)
    alpha_ref[...] = jnp.ones(alpha_ref.shape, F32)
    s_ref[...] = scores(0, 0)

    def body(j, carry):
        s_nxt, p, alpha = key_tile(j, diagonal=False)
        s_ref[...] = s_nxt
        p_ref[...] = p
        alpha_ref[...] = alpha
        return carry

    lax.fori_loop(0, i, body, 0)
    s_ref[...] = visible_only(s_ref[...])
    _, p, alpha = key_tile(i, diagonal=True)
    weighted_values(i, last, p, alpha)

    for a in range(ATT_HEADS):
        o = acc_ref[a] / l_ref[a]
        o_ref[:, a * V_DIM:(a + 1) * V_DIM] = o.T.astype(o_ref.dtype)


def _mla_attn(qt, kn, kr, vt, batch):
    nt, _, t = qt.shape
    nq = nt // batch
    s = nq * t
    ha = ATT_HEADS
    return pl.pallas_call(
        _mla_attn_kernel,
        grid=(batch, MLA_HEADS // ha, nq),
        in_specs=[
            pl.BlockSpec((None, ha * HEAD_PAD, t), lambda b, h, i: (b * nq + i, h, 0)),
            pl.BlockSpec((None, s, ha * QK_NOPE), lambda b, h, i: (b, 0, h)),
            pl.BlockSpec((None, s, LANES), lambda b, h, i: (b, 0, 0)),
            pl.BlockSpec((None, nq, ha * V_DIM, t), lambda b, h, i: (b, 0, h, 0)),
        ],
        out_specs=pl.BlockSpec((None, t, ha * V_DIM), lambda b, h, i: (b, i, h)),
        out_shape=jax.ShapeDtypeStruct((batch, s, MLA_HEADS * V_DIM), BF16),
        scratch_shapes=[
            pltpu.VMEM((ha, 1, t), F32),
            pltpu.VMEM((ha, 1, t), F32),
            pltpu.VMEM((ha, V_DIM, t), F32),
            pltpu.VMEM((t, t), F32),
            pltpu.VMEM((t, t), BF16),
            pltpu.VMEM((1, t), F32),
        ],
        compiler_params=_params("parallel", "parallel", "arbitrary"),
        name="mla_attn",
    )(qt, kn.reshape(batch, s, -1), kr.reshape(batch, s, LANES),
      vt.reshape(batch, nq, MLA_HEADS * V_DIM, t))


def _mla_out_kernel(x_ref, o_ref, w_ref, y_ref):
    y_ref[...] = x_ref[...] + jnp.dot(o_ref[...], w_ref[...], preferred_element_type=F32)


def _mla_out(x, o, w):
    m, d = x.shape
    tm = MLA_O_TM
    return pl.pallas_call(
        _mla_out_kernel,
        grid=(m // tm,),
        in_specs=[
            pl.BlockSpec((tm, d), lambda i: (i, 0)),
            pl.BlockSpec((tm, o.shape[1]), lambda i: (i, 0)),
            pl.BlockSpec(w.shape, lambda i: (0, 0)),
        ],
        out_specs=pl.BlockSpec((tm, d), lambda i: (i, 0)),
        out_shape=jax.ShapeDtypeStruct((m, d), F32),
        compiler_params=_params("parallel"),
        name="mla_out",
    )(x, o, w)


def _mla(x, positions, gain, w_in, gq, w_q_up, gkv, w_kv_up, w_out):
    b, _ = positions.shape
    m, d = x.shape
    w_in_p = jnp.pad(w_in, ((0, 0), (0, LANES - QK_ROPE))).astype(BF16)
    wqt = w_q_up.T.astype(BF16)
    w_kv = w_kv_up.reshape(KV_LORA, MLA_HEADS, QK_NOPE + V_DIM)
    wkn = w_kv[:, :, :QK_NOPE].reshape(KV_LORA, -1).astype(BF16)
    wvt = w_kv[:, :, QK_NOPE:].reshape(KV_LORA, -1).T.astype(BF16)
    qt, kn, kr, vt = _mla_proj(x, positions, gain, w_in_p, gq, wqt, gkv, wkn, wvt)
    o = _mla_attn(qt, kn, kr, vt, b)
    return _mla_out(x, o.reshape(m, -1), w_out.astype(BF16))


def kernel(x, positions, ln_ffn1, ffn1_w_in, ffn1_w_out, ln_mix, ln_ffn2, ffn2_w_in, ffn2_w_out,
           sgu_w_in, sgu_v_gain, sgu_v_bias, sgu_w_spatial, sgu_b_spatial, sgu_w_out,
           mla_w_in, mla_q_norm, mla_w_q_up, mla_kv_norm, mla_w_kv_up, mla_w_out, ln_final):
    b, s, d = x.shape
    depth = ln_ffn1.shape[0]
    h = x.reshape(b * s, d)
    for i in range(depth):
        h = _ffn(h, ln_ffn1[i], ffn1_w_in[i].astype(BF16), ffn1_w_out[i].astype(BF16))
        j = i // 2
        if i % 2 == 0:
            h = _sgu(h, ln_mix[i], sgu_w_in[j].astype(BF16), sgu_v_gain[j], sgu_v_bias[j],
                     sgu_w_spatial[j], sgu_b_spatial[j], sgu_w_out[j].astype(BF16))
        else:
            h = _mla(h, positions, ln_mix[i], mla_w_in[j], mla_q_norm[j], mla_w_q_up[j],
                     mla_kv_norm[j], mla_w_kv_up[j], mla_w_out[j])
        last = i == depth - 1
        h = _ffn(h, ln_ffn2[i], ffn2_w_in[i].astype(BF16), ffn2_w_out[i].astype(BF16),
                 final_gain=ln_final if last else None)
    return h.reshape(b, s, d)
```

```python
import functools
import math

import jax
import jax.numpy as jnp
from jax import lax
from jax.experimental import pallas as pl
from jax.experimental.pallas import tpu as pltpu

F32 = jnp.float32
BF16 = jnp.bfloat16

EPS = 1e-6
CHUNK = 64
SGU_BLOCK = 128
SGU_GROUPS = 8
MLA_HEADS = 16
Q_LORA = 512
KV_LORA = 512
QK_NOPE = 128
QK_ROPE = 64
V_DIM = 128
QK_DIM = QK_NOPE + QK_ROPE
ROPE_THETA = 10000.0

LANES = 128
BF16_SUBLANES = 16
HEAD_PAD = 2 * LANES
VMEM_LIMIT = 61 * 1024 * 1024

FFN_TM = 1024
FFN_TF = 512
FFN_ROWS = 64
FFN_COLS = 512
SGU_TM = 512
SGU_GS = 2
MLA_P_TM = 256
MLA_O_TM = 512
ATT_T = 512
ATT_HEADS = 4
ATT_SUM_ROWS = 16


def _rms_normalize(x, gain):
    return x * lax.rsqrt(jnp.mean(x * x, axis=-1, keepdims=True) + EPS) * gain


def _params(*sem):
    return pltpu.CompilerParams(dimension_semantics=sem, vmem_limit_bytes=VMEM_LIMIT)


def _dot_nt(a, b):
    return lax.dot_general(a, b, (((1,), (1,)), ((), ())), preferred_element_type=F32)


def _cast_plan(stacked, layer, gi, gj):
    _, rows, cols = stacked.shape

    def fits(br, bc):
        return br % BF16_SUBLANES == 0 and bc % LANES == 0

    if rows % gi == 0 and cols % gj == 0 and fits(rows // gi, cols // gj):
        block, index = (rows // gi, cols // gj), (lambda i, j: (i, j))
    else:
        assert rows % gj == 0 and cols % gi == 0 and fits(rows // gj, cols // gi), stacked.shape
        block, index = (rows // gj, cols // gi), (lambda i, j: (j, i))
    src = pl.BlockSpec((None,) + block, lambda i, j: (layer,) + index(i, j))
    dst = pl.BlockSpec(block, index)
    return src, dst, jax.ShapeDtypeStruct((rows, cols), BF16)


def _cast_blocks(src_refs, dst_refs):
    for src, dst in zip(src_refs, dst_refs):
        dst[...] = src[...].astype(BF16)


def _ffn_kernel(x_ref, g_ref, wg_ref, wu_ref, wo_ref, *rest, final_norm, n_cast):
    rest = list(rest)
    gf_ref = rest.pop(0) if final_norm else None
    cast_src = [rest.pop(0) for _ in range(n_cast)]
    o_ref = rest.pop(0)
    cast_dst = [rest.pop(0) for _ in range(n_cast)]
    (xn_ref,) = rest
    j = pl.program_id(1)
    _cast_blocks(cast_src, cast_dst)

    tm, d = x_ref.shape
    row_chunks = [slice(r, r + FFN_ROWS) for r in range(0, tm, FFN_ROWS)]

    @pl.when(j == 0)
    def _():
        for rows in row_chunks:
            xn_ref[rows, :] = _rms_normalize(x_ref[rows, :], g_ref[...]).astype(BF16)
        o_ref[...] = jnp.zeros(o_ref.shape, F32)

    xn = xn_ref[...]
    gate = jnp.dot(xn, wg_ref[...], preferred_element_type=F32)
    up = jnp.dot(xn, wu_ref[...], preferred_element_type=F32)
    act = (gate * jax.nn.sigmoid(gate) * up).astype(BF16)
    for c in range(0, d, FFN_COLS):
        o_ref[:, c:c + FFN_COLS] += jnp.dot(act, wo_ref[:, c:c + FFN_COLS],
                                            preferred_element_type=F32)

    @pl.when(j == pl.num_programs(1) - 1)
    def _():
        for rows in row_chunks:
            y = x_ref[rows, :] + 0.5 * o_ref[rows, :]
            o_ref[rows, :] = y
            if final_norm:
                rstd = lax.rsqrt(jnp.mean(y * y, axis=-1, keepdims=True) + EPS)
                o_ref[rows, :] = o_ref[rows, :] * rstd * gf_ref[...]


def _ffn(x, gain, w_in, w_out, final_gain=None, cast_next=None):
    m, d = x.shape
    f = w_out.shape[0]
    nf = f // FFN_TF
    grid = (m // FFN_TM, nf)
    final_norm = final_gain is not None
    in_specs = [
        pl.BlockSpec((FFN_TM, d), lambda i, j: (i, 0)),
        pl.BlockSpec((1, d), lambda i, j: (0, 0)),
        pl.BlockSpec((d, FFN_TF), lambda i, j: (0, j)),
        pl.BlockSpec((d, FFN_TF), lambda i, j: (0, j + nf)),
        pl.BlockSpec((FFN_TF, d), lambda i, j: (j, 0)),
    ]
    args = [x, gain.reshape(1, d), w_in, w_in, w_out]
    if final_norm:
        in_specs.append(pl.BlockSpec((1, d), lambda i, j: (0, 0)))
        args.append(final_gain.reshape(1, d))
    out_specs = [pl.BlockSpec((FFN_TM, d), lambda i, j: (i, 0))]
    out_shape = [jax.ShapeDtypeStruct((m, d), F32)]
    *stacks, layer = cast_next if cast_next is not None else (None,)
    for stacked in stacks:
        src, dst, shape = _cast_plan(stacked, layer, *grid)
        in_specs.append(src)
        args.append(stacked)
        out_specs.append(dst)
        out_shape.append(shape)
    out = pl.pallas_call(
        functools.partial(_ffn_kernel, final_norm=final_norm, n_cast=len(stacks)),
        grid=grid,
        in_specs=in_specs,
        out_specs=out_specs,
        out_shape=out_shape,
        scratch_shapes=[pltpu.VMEM((FFN_TM, d), BF16)],
        compiler_params=_params("parallel", "arbitrary"),
        name="ffn_final" if final_norm else "ffn",
    )(*args)
    return out[0], tuple(out[1:])


def _sgu_kernel(x_ref, g_ref, wu_ref, wv_ref, vg_ref, vb_ref, ws_ref, bs_ref, wo_ref, *rest,
                n_cast):
    cast_src, (o_ref, *rest) = rest[:n_cast], rest[n_cast:]
    cast_dst, (xn_ref, u_ref, v_ref, mu_ref, rstd_ref) = rest[:n_cast], rest[n_cast:]
    j = pl.program_id(1)
    _cast_blocks(cast_src, cast_dst)
    ns = SGU_GROUPS // SGU_GS
    cw = wu_ref.shape[1]
    gw = cw // SGU_GS
    nl = cw // LANES
    tm = x_ref.shape[0]
    width = SGU_GROUPS * gw

    @pl.when(j == 0)
    def _():
        xn_ref[...] = _rms_normalize(x_ref[...], g_ref[...]).astype(BF16)
        o_ref[...] = jnp.zeros(o_ref.shape, F32)

    @pl.when(j < ns)
    def _():
        xn = xn_ref[...]
        u = jax.nn.gelu(jnp.dot(xn, wu_ref[...], preferred_element_type=F32))
        v = jax.nn.gelu(jnp.dot(xn, wv_ref[...], preferred_element_type=F32))
        u_ref[j] = u.astype(BF16)
        v_ref[j] = v.astype(BF16)

    @pl.when(j == ns)
    def _():
        total = jnp.zeros((tm, LANES), F32)
        for s in range(ns):
            for c in range(nl):
                total += v_ref[s, :, c * LANES:(c + 1) * LANES].astype(F32)
        mu = jnp.broadcast_to(jnp.sum(total, axis=-1, keepdims=True) / width, (tm, LANES))
        sq = jnp.zeros((tm, LANES), F32)
        for s in range(ns):
            for c in range(nl):
                dv = v_ref[s, :, c * LANES:(c + 1) * LANES].astype(F32) - mu
                sq += dv * dv
        var = jnp.sum(sq, axis=-1, keepdims=True) / width
        mu_ref[...] = mu
        rstd_ref[...] = jnp.broadcast_to(lax.rsqrt(var + EPS), (tm, LANES))

    @pl.when(j >= ns)
    def _():
        s = j - ns
        mu = pltpu.repeat(mu_ref[...], nl, 1)
        rstd = pltpu.repeat(rstd_ref[...], nl, 1)
        vn = ((v_ref[s].astype(F32) - mu) * rstd * vg_ref[0] + vb_ref[0]).astype(BF16)
        u = u_ref[s]
        row_chunk = lax.broadcasted_iota(jnp.int32, (SGU_BLOCK, SGU_BLOCK), 0) // CHUNK
        col_chunk = lax.broadcasted_iota(jnp.int32, (SGU_BLOCK, SGU_BLOCK), 1) // CHUNK
        visible = row_chunk >= col_chunk
        cols = []
        for gi in range(SGU_GS):
            ws = jnp.where(visible, ws_ref[gi], 0.0).astype(BF16)
            bias = pltpu.repeat(bs_ref[gi], gw // LANES, 1)
            blocks = []
            for b in range(tm // SGU_BLOCK):
                rows = slice(b * SGU_BLOCK, (b + 1) * SGU_BLOCK)
                lanes = slice(gi * gw, (gi + 1) * gw)
                mixed = jnp.dot(ws, vn[rows, lanes], preferred_element_type=F32) + bias
                blocks.append((u[rows, lanes].astype(F32) * mixed).astype(BF16))
            cols.append(jnp.concatenate(blocks, axis=0))
        gated = jnp.concatenate(cols, axis=1)
        o_ref[...] += jnp.dot(gated, wo_ref[...], preferred_element_type=F32)

    @pl.when(j == pl.num_programs(1) - 1)
    def _():
        o_ref[...] += x_ref[...]


def _sgu(x, gain, w_in, v_gain, v_bias, w_spatial, b_spatial, w_out, cast_next=None):
    m, d = x.shape
    width = w_out.shape[0]
    ns = SGU_GROUPS // SGU_GS
    cw = width // ns
    tm = SGU_TM
    grid = (m // tm, 2 * ns)

    def phase1(j):
        return jnp.minimum(j, ns - 1)

    def phase2(j):
        return jnp.maximum(j - ns, 0)

    bias = jnp.broadcast_to(b_spatial[:, :, None], (SGU_GROUPS, SGU_BLOCK, LANES))
    in_specs = [
        pl.BlockSpec((tm, d), lambda i, j: (i, 0)),
        pl.BlockSpec((1, d), lambda i, j: (0, 0)),
        pl.BlockSpec((d, cw), lambda i, j: (0, phase1(j))),
        pl.BlockSpec((d, cw), lambda i, j: (0, ns + phase1(j))),
        pl.BlockSpec((1, 1, cw), lambda i, j: (phase2(j), 0, 0)),
        pl.BlockSpec((1, 1, cw), lambda i, j: (phase2(j), 0, 0)),
        pl.BlockSpec((SGU_GS, SGU_BLOCK, SGU_BLOCK), lambda i, j: (phase2(j), 0, 0)),
        pl.BlockSpec((SGU_GS, SGU_BLOCK, LANES), lambda i, j: (phase2(j), 0, 0)),
        pl.BlockSpec((cw, d), lambda i, j: (phase2(j), 0)),
    ]
    args = [x, gain.reshape(1, d), w_in, w_in, v_gain.reshape(ns, 1, cw),
            v_bias.reshape(ns, 1, cw), w_spatial, bias, w_out]
    out_specs = [pl.BlockSpec((tm, d), lambda i, j: (i, 0))]
    out_shape = [jax.ShapeDtypeStruct((m, d), F32)]
    *stacks, layer = cast_next if cast_next is not None else (None,)
    for stacked in stacks:
        src, dst, shape = _cast_plan(stacked, layer, *grid)
        in_specs.append(src)
        args.append(stacked)
        out_specs.append(dst)
        out_shape.append(shape)
    out = pl.pallas_call(
        functools.partial(_sgu_kernel, n_cast=len(stacks)),
        grid=grid,
        in_specs=in_specs,
        out_specs=out_specs,
        out_shape=out_shape,
        scratch_shapes=[
            pltpu.VMEM((tm, d), BF16),
            pltpu.VMEM((ns, tm, cw), BF16),
            pltpu.VMEM((ns, tm, cw), BF16),
            pltpu.VMEM((tm, LANES), F32),
            pltpu.VMEM((tm, LANES), F32),
        ],
        compiler_params=_params("parallel", "arbitrary"),
        name="sgu",
    )(*args)
    return out[0], tuple(out[1:])


def _mla_proj_kernel(x_ref, posc_ref, posr_ref, freqr_ref, freqc_ref, g_ref, wi_ref, gq_ref,
                     wqt_ref, gkv_ref, wkn_ref, wvt_ref, qt_ref, kn_ref, kr_ref, vt_ref):
    half = QK_ROPE // 2
    hn = _rms_normalize(x_ref[...], g_ref[...]).astype(BF16)
    proj = jnp.dot(hn, wi_ref[...], preferred_element_type=F32)
    qn = _rms_normalize(proj[:, :Q_LORA], gq_ref[...]).astype(BF16)
    kvn = _rms_normalize(proj[:, Q_LORA:Q_LORA + KV_LORA], gkv_ref[...]).astype(BF16)
    kr = proj[:, Q_LORA + KV_LORA:]

    ang = posc_ref[...].astype(F32) * freqr_ref[...]
    lane = lax.broadcasted_iota(jnp.int32, ang.shape, 1)
    sin = jnp.sin(ang)
    sin_lo = jnp.where(lane < half, -sin, 0.0)
    sin_hi = jnp.where((lane >= half) & (lane < QK_ROPE), sin, 0.0)
    kr_ref[...] = (kr * jnp.cos(ang) + pltpu.roll(kr, LANES - half, 1) * sin_lo
                   + pltpu.roll(kr, half, 1) * sin_hi).astype(BF16)

    kn_ref[...] = jnp.dot(kvn, wkn_ref[...], preferred_element_type=F32).astype(BF16)
    vt_ref[...] = _dot_nt(wvt_ref[...], kvn).astype(BF16)

    ang_t = freqc_ref[...] * posr_ref[...].astype(F32)
    cos_t = jnp.cos(ang_t)
    sin_t = jnp.sin(ang_t)
    scale = QK_DIM ** -0.5 * math.log2(math.e)
    q_t = _dot_nt(wqt_ref[...], qn) * scale
    for h in range(MLA_HEADS):
        src = h * QK_DIM
        dst = h * HEAD_PAD
        x1 = q_t[src + QK_NOPE:src + QK_NOPE + half]
        x2 = q_t[src + QK_NOPE + half:src + QK_DIM]
        qt_ref[dst:dst + QK_NOPE, :] = q_t[src:src + QK_NOPE].astype(BF16)
        qt_ref[dst + QK_NOPE:dst + QK_NOPE + half, :] = (x1 * cos_t - x2 * sin_t).astype(BF16)
        qt_ref[dst + QK_NOPE + half:dst + QK_DIM, :] = (x1 * sin_t + x2 * cos_t).astype(BF16)
        qt_ref[dst + QK_DIM:dst + HEAD_PAD, :] = jnp.zeros((HEAD_PAD - QK_DIM, q_t.shape[1]), BF16)


def _mla_proj(x, positions, gain, w_in, gq, wqt, gkv, wkn, wvt):
    m, d = x.shape
    tm = MLA_P_TM
    r = ATT_T // tm
    nt = m // ATT_T
    half = QK_ROPE // 2
    inv_freq = 1.0 / (ROPE_THETA ** (jnp.arange(half, dtype=F32) / half))
    freq_row = jnp.concatenate([inv_freq, inv_freq, jnp.zeros((LANES - QK_ROPE,), F32)])
    const = lambda i: (0, 0)

    def resident(shape):
        return pl.BlockSpec(shape, const, pipeline_mode=pl.Buffered(1))

    return pl.pallas_call(
        _mla_proj_kernel,
        grid=(m // tm,),
        in_specs=[
            pl.BlockSpec((tm, d), lambda i: (i, 0)),
            pl.BlockSpec((tm, 1), lambda i: (i, 0)),
            pl.BlockSpec((None, 1, tm), lambda i: (i, 0, 0)),
            resident((1, LANES)),
            resident((half, 1)),
            resident((1, d)),
            resident(w_in.shape),
            resident((1, Q_LORA)),
            resident(wqt.shape),
            resident((1, KV_LORA)),
            resident(wkn.shape),
            resident(wvt.shape),
        ],
        out_specs=[
            pl.BlockSpec((None, MLA_HEADS * HEAD_PAD, tm), lambda i: (i // r, 0, i % r)),
            pl.BlockSpec((tm, MLA_HEADS * QK_NOPE), lambda i: (i, 0)),
            pl.BlockSpec((tm, LANES), lambda i: (i, 0)),
            pl.BlockSpec((None, MLA_HEADS * V_DIM, tm), lambda i: (i // r, 0, i % r)),
        ],
        out_shape=[
            jax.ShapeDtypeStruct((nt, MLA_HEADS * HEAD_PAD, ATT_T), BF16),
            jax.ShapeDtypeStruct((m, MLA_HEADS * QK_NOPE), BF16),
            jax.ShapeDtypeStruct((m, LANES), BF16),
            jax.ShapeDtypeStruct((nt, MLA_HEADS * V_DIM, ATT_T), BF16),
        ],
        compiler_params=_params("parallel"),
        name="mla_proj",
    )(x, positions.reshape(m, 1), positions.reshape(m // tm, 1, tm), freq_row.reshape(1, LANES),
      inv_freq.reshape(half, 1), gain.reshape(1, d), w_in, gq.reshape(1, -1), wqt,
      gkv.reshape(1, -1), wkn, wvt)


def _mla_attn_kernel(qt_ref, kn_ref, kr_ref, vt_ref, o_ref,
                     m_ref, acc_ref, s_ref, p_ref, alpha_ref):
    i = pl.program_id(2)
    t = qt_ref.shape[1]
    last = ATT_HEADS - 1

    def visible_only(s):
        key_chunk = lax.broadcasted_iota(jnp.int32, (t, 1), 0) // CHUNK
        qry_chunk = lax.broadcasted_iota(jnp.int32, (1, t), 1) // CHUNK
        return jnp.where(key_chunk <= qry_chunk, s, -jnp.inf)

    def scores(j, a):
        rows = pl.ds(pl.multiple_of(j * t, t), t)
        k = jnp.concatenate([kn_ref[rows, a * QK_NOPE:(a + 1) * QK_NOPE], kr_ref[rows, :]], axis=1)
        return jnp.dot(k, qt_ref[a * HEAD_PAD:(a + 1) * HEAD_PAD, :],
                       preferred_element_type=F32)

    ones_rows = jnp.ones((ATT_SUM_ROWS, t), BF16)

    def weighted_values(j, a, p, alpha):
        v_ext = jnp.concatenate([vt_ref[j, a * V_DIM:(a + 1) * V_DIM, :], ones_rows], axis=0)
        pv = jnp.dot(v_ext, p, preferred_element_type=F32)
        acc_ref[a] = alpha * acc_ref[a] + pv

    def softmax(a, s):
        m_prev = m_ref[a]
        m_new = jnp.maximum(m_prev, jnp.max(s, axis=0, keepdims=True))
        alpha = jnp.exp2(m_prev - m_new)
        p = jnp.exp2(s - m_new)
        m_ref[a] = m_new
        return p.astype(BF16), alpha

    def key_tile(j, diagonal):
        s_cur = s_ref[...]
        p_prev, alpha_prev = p_ref[...], alpha_ref[...]
        for a in range(ATT_HEADS):
            if a < last:
                s_nxt = scores(j, a + 1)
                if diagonal:
                    s_nxt = visible_only(s_nxt)
            elif not diagonal:
                s_nxt = scores(j + 1, 0)
            weighted_values(jnp.maximum(j - 1, 0) if a == 0 else j, (a - 1) % ATT_HEADS,
                            p_prev, alpha_prev)
            p_prev, alpha_prev = softmax(a, s_cur)
            s_cur = s_nxt
        return s_nxt, p_prev, alpha_prev

    m_ref[...] = jnp.full(m_ref.shape, -jnp.inf, F32)
    acc_ref[...] = jnp.zeros(acc_ref.shape, F32)
    p_ref[...] = jnp.zeros(p_ref.shape, BF16)
    alpha_ref[...] = jnp.ones(alpha_ref.shape, F32)
    s_ref[...] = scores(0, 0)

    def body(j, carry):
        s_nxt, p, alpha = key_tile(j, diagonal=False)
        s_ref[...] = s_nxt
        p_ref[...] = p
        alpha_ref[...] = alpha
        return carry

    lax.fori_loop(0, i, body, 0)
    s_ref[...] = visible_only(s_ref[...])
    _, p, alpha = key_tile(i, diagonal=True)
    weighted_values(i, last, p, alpha)

    for a in range(ATT_HEADS):
        o = acc_ref[a, :V_DIM, :] / acc_ref[a, V_DIM:V_DIM + 1, :]
        o_ref[:, a * V_DIM:(a + 1) * V_DIM] = o.T.astype(o_ref.dtype)


def _mla_attn(qt, kn, kr, vt, batch):
    nt, _, t = qt.shape
    nq = nt // batch
    s = nq * t
    ha = ATT_HEADS
    return pl.pallas_call(
        _mla_attn_kernel,
        grid=(batch, MLA_HEADS // ha, nq),
        in_specs=[
            pl.BlockSpec((None, ha * HEAD_PAD, t), lambda b, h, i: (b * nq + i, h, 0)),
            pl.BlockSpec((None, s, ha * QK_NOPE), lambda b, h, i: (b, 0, h)),
            pl.BlockSpec((None, s, LANES), lambda b, h, i: (b, 0, 0)),
            pl.BlockSpec((None, nq, ha * V_DIM, t), lambda b, h, i: (b, 0, h, 0)),
        ],
        out_specs=pl.BlockSpec((None, t, ha * V_DIM), lambda b, h, i: (b, i, h)),
        out_shape=jax.ShapeDtypeStruct((batch, s, MLA_HEADS * V_DIM), BF16),
        scratch_shapes=[
            pltpu.VMEM((ha, 1, t), F32),
            pltpu.VMEM((ha, V_DIM + ATT_SUM_ROWS, t), F32),
            pltpu.VMEM((t, t), F32),
            pltpu.VMEM((t, t), BF16),
            pltpu.VMEM((1, t), F32),
        ],
        compiler_params=_params("parallel", "parallel", "arbitrary"),
        name="mla_attn",
    )(qt, kn.reshape(batch, s, -1), kr.reshape(batch, s, LANES),
      vt.reshape(batch, nq, MLA_HEADS * V_DIM, t))


def _mla_out_kernel(x_ref, o_ref, w_ref, y_ref):
    y_ref[...] = x_ref[...] + jnp.dot(o_ref[...], w_ref[...], preferred_element_type=F32)


def _mla_out(x, o, w):
    m, d = x.shape
    tm = MLA_O_TM
    return pl.pallas_call(
        _mla_out_kernel,
        grid=(m // tm,),
        in_specs=[
            pl.BlockSpec((tm, d), lambda i: (i, 0)),
            pl.BlockSpec((tm, o.shape[1]), lambda i: (i, 0)),
            pl.BlockSpec(w.shape, lambda i: (0, 0)),
        ],
        out_specs=pl.BlockSpec((tm, d), lambda i: (i, 0)),
        out_shape=jax.ShapeDtypeStruct((m, d), F32),
        compiler_params=_params("parallel"),
        name="mla_out",
    )(x, o, w)


def _mla(x, positions, gain, w_in, gq, w_q_up, gkv, w_kv_up, w_out):
    b, _ = positions.shape
    m, d = x.shape
    w_in_p = jnp.pad(w_in, ((0, 0), (0, LANES - QK_ROPE))).astype(BF16)
    wqt = w_q_up.T.astype(BF16)
    w_kv = w_kv_up.reshape(KV_LORA, MLA_HEADS, QK_NOPE + V_DIM)
    wkn = w_kv[:, :, :QK_NOPE].reshape(KV_LORA, -1).astype(BF16)
    wvt = w_kv[:, :, QK_NOPE:].reshape(KV_LORA, -1).T.astype(BF16)
    qt, kn, kr, vt = _mla_proj(x, positions, gain, w_in_p, gq, wqt, gkv, wkn, wvt)
    o = _mla_attn(qt, kn, kr, vt, b)
    return _mla_out(x, o.reshape(m, -1), w_out.astype(BF16))


def kernel(x, positions, ln_ffn1, ffn1_w_in, ffn1_w_out, ln_mix, ln_ffn2, ffn2_w_in, ffn2_w_out,
           sgu_w_in, sgu_v_gain, sgu_v_bias, sgu_w_spatial, sgu_b_spatial, sgu_w_out,
           mla_w_in, mla_q_norm, mla_w_q_up, mla_kv_norm, mla_w_kv_up, mla_w_out, ln_final):
    b, s, d = x.shape
    depth = ln_ffn1.shape[0]
    h = x.reshape(b * s, d)

    ffn_stacks = [(w_in, w_out, i) for i in range(depth)
                  for w_in, w_out in ((ffn1_w_in, ffn1_w_out), (ffn2_w_in, ffn2_w_out))]
    ffn_weights = {0: (ffn1_w_in[0].astype(BF16), ffn1_w_out[0].astype(BF16))}
    uncast = list(range(1, len(ffn_stacks)))

    def side_cast(call, *args, **kwargs):
        target = uncast.pop(0) if uncast else None
        out, cast = call(*args, cast_next=ffn_stacks[target] if target is not None else None,
                         **kwargs)
        if target is not None:
            ffn_weights[target] = cast
        return out

    for i in range(depth):
        h = side_cast(_ffn, h, ln_ffn1[i], *ffn_weights[2 * i])
        j = i // 2
        if i % 2 == 0:
            h = side_cast(_sgu, h, ln_mix[i], sgu_w_in[j].astype(BF16), sgu_v_gain[j],
                          sgu_v_bias[j], sgu_w_spatial[j], sgu_b_spatial[j],
                          sgu_w_out[j].astype(BF16))
        else:
            h = _mla(h, positions, ln_mix[i], mla_w_in[j], mla_q_norm[j], mla_w_q_up[j],
                     mla_kv_norm[j], mla_w_kv_up[j], mla_w_out[j])
        last = i == depth - 1
        h = side_cast(_ffn, h, ln_ffn2[i], *ffn_weights[2 * i + 1],
                      final_gain=ln_final if last else None)
    return h.reshape(b, s, d)
```

```python
import functools
import math

import jax
import jax.numpy as jnp
from jax import lax
from jax.experimental import pallas as pl
from jax.experimental.pallas import tpu as pltpu

F32 = jnp.float32
BF16 = jnp.bfloat16

EPS = 1e-6
CHUNK = 64
SGU_BLOCK = 128
SGU_GROUPS = 8
MLA_HEADS = 16
Q_LORA = 512
KV_LORA = 512
QK_NOPE = 128
QK_ROPE = 64
V_DIM = 128
QK_DIM = QK_NOPE + QK_ROPE
ROPE_THETA = 10000.0

LANES = 128
BF16_SUBLANES = 16
HEAD_PAD = 2 * LANES
VMEM_LIMIT = 61 * 1024 * 1024

FFN_TM = 1024
FFN_TF = 512
FFN_ROWS = 64
FFN_COLS = 512
SGU_IN_TM = 256
SGU_OUT_TM = 256
SGU_COLS = 1024
SGU_OUT_COLS = 512
MLA_P_TM = 256
MLA_O_TM = 512
ATT_T = 512
ATT_HEADS = 4
ATT_SUM_ROWS = 16


def _rms_normalize(x, gain):
    return x * lax.rsqrt(jnp.mean(x * x, axis=-1, keepdims=True) + EPS) * gain


def _params(*sem):
    return pltpu.CompilerParams(dimension_semantics=sem, vmem_limit_bytes=VMEM_LIMIT)


def _dot_nt(a, b):
    return lax.dot_general(a, b, (((1,), (1,)), ((), ())), preferred_element_type=F32)


def _cast_plan(stacked, layer, grid):
    _, rows, cols = stacked.shape
    steps = math.prod(grid)
    for nb in range(1, steps + 1):
        na = steps // nb
        if (na * nb == steps and rows % na == 0 and cols % nb == 0
                and (rows // na) % BF16_SUBLANES == 0 and (cols // nb) % LANES == 0):
            break
    else:
        raise ValueError(f"cannot spread a cast of {stacked.shape} over grid {grid}")
    block = (rows // na, cols // nb)

    def index(*ids):
        flat = ids[0]
        for extent, idx in zip(grid[1:], ids[1:]):
            flat = flat * extent + idx
        return flat // nb, flat % nb

    src = pl.BlockSpec((None,) + block, lambda *ids: (layer,) + index(*ids))
    dst = pl.BlockSpec(block, index)
    return src, dst, jax.ShapeDtypeStruct((rows, cols), BF16)


def _cast_blocks(src_refs, dst_refs):
    for src, dst in zip(src_refs, dst_refs):
        dst[...] = src[...].astype(BF16)


def _ffn_kernel(x_ref, g_ref, wg_ref, wu_ref, wo_ref, *rest, final_norm, n_cast):
    rest = list(rest)
    gf_ref = rest.pop(0) if final_norm else None
    cast_src = [rest.pop(0) for _ in range(n_cast)]
    o_ref = rest.pop(0)
    cast_dst = [rest.pop(0) for _ in range(n_cast)]
    (xn_ref,) = rest
    j = pl.program_id(1)
    _cast_blocks(cast_src, cast_dst)

    tm, d = x_ref.shape
    row_chunks = [slice(r, r + FFN_ROWS) for r in range(0, tm, FFN_ROWS)]

    @pl.when(j == 0)
    def _():
        for rows in row_chunks:
            xn_ref[rows, :] = _rms_normalize(x_ref[rows, :], g_ref[...]).astype(BF16)
        o_ref[...] = jnp.zeros(o_ref.shape, F32)

    xn = xn_ref[...]
    gate = jnp.dot(xn, wg_ref[...], preferred_element_type=F32)
    up = jnp.dot(xn, wu_ref[...], preferred_element_type=F32)
    act = (gate * jax.nn.sigmoid(gate) * up).astype(BF16)
    for c in range(0, d, FFN_COLS):
        o_ref[:, c:c + FFN_COLS] += jnp.dot(act, wo_ref[:, c:c + FFN_COLS],
                                            preferred_element_type=F32)

    @pl.when(j == pl.num_programs(1) - 1)
    def _():
        for rows in row_chunks:
            y = x_ref[rows, :] + 0.5 * o_ref[rows, :]
            o_ref[rows, :] = y
            if final_norm:
                rstd = lax.rsqrt(jnp.mean(y * y, axis=-1, keepdims=True) + EPS)
                o_ref[rows, :] = o_ref[rows, :] * rstd * gf_ref[...]


def _ffn(x, gain, w_in, w_out, final_gain=None, cast_next=None):
    m, d = x.shape
    f = w_out.shape[0]
    nf = f // FFN_TF
    grid = (m // FFN_TM, nf)
    final_norm = final_gain is not None
    in_specs = [
        pl.BlockSpec((FFN_TM, d), lambda i, j: (i, 0)),
        pl.BlockSpec((1, d), lambda i, j: (0, 0)),
        pl.BlockSpec((d, FFN_TF), lambda i, j: (0, j)),
        pl.BlockSpec((d, FFN_TF), lambda i, j: (0, j + nf)),
        pl.BlockSpec((FFN_TF, d), lambda i, j: (j, 0)),
    ]
    args = [x, gain.reshape(1, d), w_in, w_in, w_out]
    if final_norm:
        in_specs.append(pl.BlockSpec((1, d), lambda i, j: (0, 0)))
        args.append(final_gain.reshape(1, d))
    out_specs = [pl.BlockSpec((FFN_TM, d), lambda i, j: (i, 0))]
    out_shape = [jax.ShapeDtypeStruct((m, d), F32)]
    *stacks, layer = cast_next if cast_next is not None else (None,)
    for stacked in stacks:
        src, dst, shape = _cast_plan(stacked, layer, grid)
        in_specs.append(src)
        args.append(stacked)
        out_specs.append(dst)
        out_shape.append(shape)
    out = pl.pallas_call(
        functools.partial(_ffn_kernel, final_norm=final_norm, n_cast=len(stacks)),
        grid=grid,
        in_specs=in_specs,
        out_specs=out_specs,
        out_shape=out_shape,
        scratch_shapes=[pltpu.VMEM((FFN_TM, d), BF16)],
        compiler_params=_params("parallel", "arbitrary"),
        name="ffn_final" if final_norm else "ffn",
    )(*args)
    return out[0], tuple(out[1:])


def _resident(shape):
    return pl.BlockSpec(shape, lambda i: (0,) * len(shape), pipeline_mode=pl.Buffered(1))


def _sgu_in_kernel(x_ref, g_ref, w_ref, vg_ref, vb_ref, *rest, n_cast):
    cast_src, (u_ref, vn_ref, *rest) = rest[:n_cast], rest[n_cast:]
    cast_dst, (xn_ref, v_ref) = rest[:n_cast], rest[n_cast:]
    _cast_blocks(cast_src, cast_dst)
    tm, width = u_ref.shape
    chunks = [slice(c, c + SGU_COLS) for c in range(0, width, SGU_COLS)]
    tiles = [slice(c, c + LANES) for c in range(0, width, LANES)]

    def project(cols):
        return jnp.dot(xn_ref[...], w_ref[:, cols], preferred_element_type=F32)

    xn_ref[...] = _rms_normalize(x_ref[...], g_ref[...]).astype(BF16)

    def mean_pass():
        total = jnp.zeros((tm, LANES), F32)
        for t in tiles:
            total += v_ref[:, t]
        return jnp.broadcast_to(jnp.sum(total, axis=-1, keepdims=True) / width, (tm, LANES))

    def rstd_pass(mu):
        sq = jnp.zeros((tm, LANES), F32)
        for t in tiles:
            dv = v_ref[:, t] - mu
            sq += dv * dv
        var = jnp.sum(sq, axis=-1, keepdims=True) / width
        return jnp.broadcast_to(lax.rsqrt(var + EPS), (tm, LANES))

    def normalize(mu, rstd):
        for t in tiles:
            vn_ref[:, t] = ((v_ref[:, t] - mu) * rstd * vg_ref[:, t] + vb_ref[:, t]).astype(BF16)

    for cols in chunks:
        v_ref[:, cols] = jax.nn.gelu(project(slice(width + cols.start, width + cols.stop)))
    stats = {}
    side_work = [lambda: stats.update(mu=mean_pass()),
                 lambda: stats.update(rstd=rstd_pass(stats["mu"])),
                 lambda: normalize(stats["mu"], stats["rstd"])]
    assert len(chunks) >= len(side_work)
    for k, cols in enumerate(chunks):
        u = project(cols)
        if k < len(side_work):
            side_work[k]()
        u_ref[:, cols] = jax.nn.gelu(u).astype(BF16)


def _sgu_out_kernel(x_ref, u_ref, vn_ref, ws_ref, bs_ref, wo_ref, *rest, n_cast):
    cast_src, (o_ref, *rest) = rest[:n_cast], rest[n_cast:]
    cast_dst, (gated_ref,) = rest[:n_cast], rest[n_cast:]
    _cast_blocks(cast_src, cast_dst)
    tm, width = u_ref.shape
    d = o_ref.shape[1]
    gw = width // SGU_GROUPS
    row_chunk = lax.broadcasted_iota(jnp.int32, (SGU_BLOCK, SGU_BLOCK), 0) // CHUNK
    col_chunk = lax.broadcasted_iota(jnp.int32, (SGU_BLOCK, SGU_BLOCK), 1) // CHUNK
    visible = row_chunk >= col_chunk
    for g in range(SGU_GROUPS):
        lanes = slice(g * gw, (g + 1) * gw)
        ws = jnp.where(visible, ws_ref[g], 0.0).astype(BF16)
        bias = pltpu.repeat(bs_ref[g], gw // LANES, 1)
        for b in range(tm // SGU_BLOCK):
            rows = slice(b * SGU_BLOCK, (b + 1) * SGU_BLOCK)
            mixed = jnp.dot(ws, vn_ref[rows, lanes], preferred_element_type=F32) + bias
            gated_ref[rows, lanes] = (u_ref[rows, lanes].astype(F32) * mixed).astype(BF16)
    gated = gated_ref[...]
    for c in range(0, d, SGU_OUT_COLS):
        cols = slice(c, c + SGU_OUT_COLS)
        o_ref[:, cols] = x_ref[:, cols] + jnp.dot(gated, wo_ref[:, cols],
                                                  preferred_element_type=F32)


def _sgu(x, gain, w_in, v_gain, v_bias, w_spatial, b_spatial, w_out, cast_next=None):
    m, d = x.shape
    width = w_out.shape[0]
    rows = lambda tm: (lambda i: (i, 0))

    def with_cast(in_specs, args, out_specs, out_shape, stacked, n_tiles):
        if stacked is None:
            return 0
        src, dst, shape = _cast_plan(stacked, cast_next[-1], (n_tiles,))
        in_specs.append(src)
        args.append(stacked)
        out_specs.append(dst)
        out_shape.append(shape)
        return 1

    tm = SGU_IN_TM
    in_specs = [pl.BlockSpec((tm, d), rows(tm)), _resident((1, d)), _resident(w_in.shape),
                _resident((1, width)), _resident((1, width))]
    args = [x, gain.reshape(1, d), w_in, v_gain.reshape(1, width), v_bias.reshape(1, width)]
    out_specs = [pl.BlockSpec((tm, width), rows(tm)), pl.BlockSpec((tm, width), rows(tm))]
    out_shape = [jax.ShapeDtypeStruct((m, width), BF16), jax.ShapeDtypeStruct((m, width), BF16)]
    n_cast = with_cast(in_specs, args, out_specs, out_shape,
                       cast_next[0] if cast_next else None, m // tm)
    u, vn, *cast_in = pl.pallas_call(
        functools.partial(_sgu_in_kernel, n_cast=n_cast),
        grid=(m // tm,),
        in_specs=in_specs,
        out_specs=out_specs,
        out_shape=out_shape,
        scratch_shapes=[pltpu.VMEM((tm, d), BF16), pltpu.VMEM((tm, width), F32)],
        compiler_params=_params("parallel"),
        name="sgu_in",
    )(*args)

    tm = SGU_OUT_TM
    bias = jnp.broadcast_to(b_spatial[:, :, None], (SGU_GROUPS, SGU_BLOCK, LANES))
    in_specs = [pl.BlockSpec((tm, d), rows(tm)), pl.BlockSpec((tm, width), rows(tm)),
                pl.BlockSpec((tm, width), rows(tm)), _resident(w_spatial.shape),
                _resident(bias.shape), _resident(w_out.shape)]
    args = [x, u, vn, w_spatial, bias, w_out]
    out_specs = [pl.BlockSpec((tm, d), rows(tm))]
    out_shape = [jax.ShapeDtypeStruct((m, d), F32)]
    n_cast = with_cast(in_specs, args, out_specs, out_shape,
                       cast_next[1] if cast_next else None, m // tm)
    y, *cast_out = pl.pallas_call(
        functools.partial(_sgu_out_kernel, n_cast=n_cast),
        grid=(m // tm,),
        in_specs=in_specs,
        out_specs=out_specs,
        out_shape=out_shape,
        scratch_shapes=[pltpu.VMEM((tm, width), BF16)],
        compiler_params=_params("parallel"),
        name="sgu_out",
    )(*args)
    return y, tuple(cast_in + cast_out)


def _mla_proj_kernel(x_ref, posc_ref, posr_ref, freqr_ref, freqc_ref, g_ref, wi_ref, gq_ref,
                     wqt_ref, gkv_ref, wkn_ref, wvt_ref, qt_ref, kn_ref, kr_ref, vt_ref):
    half = QK_ROPE // 2
    hn = _rms_normalize(x_ref[...], g_ref[...]).astype(BF16)
    proj = jnp.dot(hn, wi_ref[...], preferred_element_type=F32)
    qn = _rms_normalize(proj[:, :Q_LORA], gq_ref[...]).astype(BF16)
    kvn = _rms_normalize(proj[:, Q_LORA:Q_LORA + KV_LORA], gkv_ref[...]).astype(BF16)
    kr = proj[:, Q_LORA + KV_LORA:]

    ang = posc_ref[...].astype(F32) * freqr_ref[...]
    lane = lax.broadcasted_iota(jnp.int32, ang.shape, 1)
    sin = jnp.sin(ang)
    sin_lo = jnp.where(lane < half, -sin, 0.0)
    sin_hi = jnp.where((lane >= half) & (lane < QK_ROPE), sin, 0.0)
    kr_ref[...] = (kr * jnp.cos(ang) + pltpu.roll(kr, LANES - half, 1) * sin_lo
                   + pltpu.roll(kr, half, 1) * sin_hi).astype(BF16)

    kn_ref[...] = jnp.dot(kvn, wkn_ref[...], preferred_element_type=F32).astype(BF16)
    vt_ref[...] = _dot_nt(wvt_ref[...], kvn).astype(BF16)

    ang_t = freqc_ref[...] * posr_ref[...].astype(F32)
    cos_t = jnp.cos(ang_t)
    sin_t = jnp.sin(ang_t)
    scale = QK_DIM ** -0.5 * math.log2(math.e)
    q_t = _dot_nt(wqt_ref[...], qn) * scale
    for h in range(MLA_HEADS):
        src = h * QK_DIM
        dst = h * HEAD_PAD
        x1 = q_t[src + QK_NOPE:src + QK_NOPE + half]
        x2 = q_t[src + QK_NOPE + half:src + QK_DIM]
        qt_ref[dst:dst + QK_NOPE, :] = q_t[src:src + QK_NOPE].astype(BF16)
        qt_ref[dst + QK_NOPE:dst + QK_NOPE + half, :] = (x1 * cos_t - x2 * sin_t).astype(BF16)
        qt_ref[dst + QK_NOPE + half:dst + QK_DIM, :] = (x1 * sin_t + x2 * cos_t).astype(BF16)
        qt_ref[dst + QK_DIM:dst + HEAD_PAD, :] = jnp.zeros((HEAD_PAD - QK_DIM, q_t.shape[1]), BF16)


def _mla_proj(x, positions, gain, w_in, gq, wqt, gkv, wkn, wvt):
    m, d = x.shape
    tm = MLA_P_TM
    r = ATT_T // tm
    nt = m // ATT_T
    half = QK_ROPE // 2
    inv_freq = 1.0 / (ROPE_THETA ** (jnp.arange(half, dtype=F32) / half))
    freq_row = jnp.concatenate([inv_freq, inv_freq, jnp.zeros((LANES - QK_ROPE,), F32)])
    const = lambda i: (0, 0)

    def resident(shape):
        return pl.BlockSpec(shape, const, pipeline_mode=pl.Buffered(1))

    return pl.pallas_call(
        _mla_proj_kernel,
        grid=(m // tm,),
        in_specs=[
            pl.BlockSpec((tm, d), lambda i: (i, 0)),
            pl.BlockSpec((tm, 1), lambda i: (i, 0)),
            pl.BlockSpec((None, 1, tm), lambda i: (i, 0, 0)),
            resident((1, LANES)),
            resident((half, 1)),
            resident((1, d)),
            resident(w_in.shape),
            resident((1, Q_LORA)),
            resident(wqt.shape),
            resident((1, KV_LORA)),
            resident(wkn.shape),
            resident(wvt.shape),
        ],
        out_specs=[
            pl.BlockSpec((None, MLA_HEADS * HEAD_PAD, tm), lambda i: (i // r, 0, i % r)),
            pl.BlockSpec((tm, MLA_HEADS * QK_NOPE), lambda i: (i, 0)),
            pl.BlockSpec((tm, LANES), lambda i: (i, 0)),
            pl.BlockSpec((None, MLA_HEADS * V_DIM, tm), lambda i: (i // r, 0, i % r)),
        ],
        out_shape=[
            jax.ShapeDtypeStruct((nt, MLA_HEADS * HEAD_PAD, ATT_T), BF16),
            jax.ShapeDtypeStruct((m, MLA_HEADS * QK_NOPE), BF16),
            jax.ShapeDtypeStruct((m, LANES), BF16),
            jax.ShapeDtypeStruct((nt, MLA_HEADS * V_DIM, ATT_T), BF16),
        ],
        compiler_params=_params("parallel"),
        name="mla_proj",
    )(x, positions.reshape(m, 1), positions.reshape(m // tm, 1, tm), freq_row.reshape(1, LANES),
      inv_freq.reshape(half, 1), gain.reshape(1, d), w_in, gq.reshape(1, -1), wqt,
      gkv.reshape(1, -1), wkn, wvt)


def _mla_attn_kernel(qt_ref, kn_ref, kr_ref, vt_ref, o_ref,
                     m_ref, acc_ref, s_ref, p_ref, alpha_ref):
    i = pl.program_id(2)
    t = qt_ref.shape[1]
    last = ATT_HEADS - 1

    def visible_only(s):
        key_chunk = lax.broadcasted_iota(jnp.int32, (t, 1), 0) // CHUNK
        qry_chunk = lax.broadcasted_iota(jnp.int32, (1, t), 1) // CHUNK
        return jnp.where(key_chunk <= qry_chunk, s, -jnp.inf)

    def scores(j, a):
        rows = pl.ds(pl.multiple_of(j * t, t), t)
        k = jnp.concatenate([kn_ref[rows, a * QK_NOPE:(a + 1) * QK_NOPE], kr_ref[rows, :]], axis=1)
        return jnp.dot(k, qt_ref[a * HEAD_PAD:(a + 1) * HEAD_PAD, :],
                       preferred_element_type=F32)

    ones_rows = jnp.ones((ATT_SUM_ROWS, t), BF16)

    def weighted_values(j, a, p, alpha):
        v_ext = jnp.concatenate([vt_ref[j, a * V_DIM:(a + 1) * V_DIM, :], ones_rows], axis=0)
        pv = jnp.dot(v_ext, p, preferred_element_type=F32)
        acc_ref[a] = alpha * acc_ref[a] + pv

    def softmax(a, s):
        m_prev = m_ref[a]
        m_new = jnp.maximum(m_prev, jnp.max(s, axis=0, keepdims=True))
        alpha = jnp.exp2(m_prev - m_new)
        p = jnp.exp2(s - m_new)
        m_ref[a] = m_new
        return p.astype(BF16), alpha

    def key_tile(j, diagonal):
        s_cur = s_ref[...]
        p_prev, alpha_prev = p_ref[...], alpha_ref[...]
        for a in range(ATT_HEADS):
            if a < last:
                s_nxt = scores(j, a + 1)
                if diagonal:
                    s_nxt = visible_only(s_nxt)
            elif not diagonal:
                s_nxt = scores(j + 1, 0)
            weighted_values(jnp.maximum(j - 1, 0) if a == 0 else j, (a - 1) % ATT_HEADS,
                            p_prev, alpha_prev)
            p_prev, alpha_prev = softmax(a, s_cur)
            s_cur = s_nxt
        return s_nxt, p_prev, alpha_prev

    m_ref[...] = jnp.full(m_ref.shape, -jnp.inf, F32)
    acc_ref[...] = jnp.zeros(acc_ref.shape, F32)
    p_ref[...] = jnp.zeros(p_ref.shape, BF16)
    alpha_ref[...] = jnp.ones(alpha_ref.shape, F32)
    s_ref[...] = scores(0, 0)

    def body(j, carry):
        s_nxt, p, alpha = key_tile(j, diagonal=False)
        s_ref[...] = s_nxt
        p_ref[...] = p
        alpha_ref[...] = alpha
        return carry

    lax.fori_loop(0, i, body, 0)
    s_ref[...] = visible_only(s_ref[...])
    _, p, alpha = key_tile(i, diagonal=True)
    weighted_values(i, last, p, alpha)

    for a in range(ATT_HEADS):
        o = acc_ref[a, :V_DIM, :] / acc_ref[a, V_DIM:V_DIM + 1, :]
        o_ref[:, a * V_DIM:(a + 1) * V_DIM] = o.T.astype(o_ref.dtype)


def _mla_attn(qt, kn, kr, vt, batch):
    nt, _, t = qt.shape
    nq = nt // batch
    s = nq * t
    ha = ATT_HEADS
    return pl.pallas_call(
        _mla_attn_kernel,
        grid=(batch, MLA_HEADS // ha, nq),
        in_specs=[
            pl.BlockSpec((None, ha * HEAD_PAD, t), lambda b, h, i: (b * nq + i, h, 0)),
            pl.BlockSpec((None, s, ha * QK_NOPE), lambda b, h, i: (b, 0, h)),
            pl.BlockSpec((None, s, LANES), lambda b, h, i: (b, 0, 0)),
            pl.BlockSpec((None, nq, ha * V_DIM, t), lambda b, h, i: (b, 0, h, 0)),
        ],
        out_specs=pl.BlockSpec((None, t, ha * V_DIM), lambda b, h, i: (b, i, h)),
        out_shape=jax.ShapeDtypeStruct((batch, s, MLA_HEADS * V_DIM), BF16),
        scratch_shapes=[
            pltpu.VMEM((ha, 1, t), F32),
            pltpu.VMEM((ha, V_DIM + ATT_SUM_ROWS, t), F32),
            pltpu.VMEM((t, t), F32),
            pltpu.VMEM((t, t), BF16),
            pltpu.VMEM((1, t), F32),
        ],
        compiler_params=_params("parallel", "parallel", "arbitrary"),
        name="mla_attn",
    )(qt, kn.reshape(batch, s, -1), kr.reshape(batch, s, LANES),
      vt.reshape(batch, nq, MLA_HEADS * V_DIM, t))


def _mla_out_kernel(x_ref, o_ref, w_ref, y_ref):
    y_ref[...] = x_ref[...] + jnp.dot(o_ref[...], w_ref[...], preferred_element_type=F32)


def _mla_out(x, o, w):
    m, d = x.shape
    tm = MLA_O_TM
    return pl.pallas_call(
        _mla_out_kernel,
        grid=(m // tm,),
        in_specs=[
            pl.BlockSpec((tm, d), lambda i: (i, 0)),
            pl.BlockSpec((tm, o.shape[1]), lambda i: (i, 0)),
            pl.BlockSpec(w.shape, lambda i: (0, 0)),
        ],
        out_specs=pl.BlockSpec((tm, d), lambda i: (i, 0)),
        out_shape=jax.ShapeDtypeStruct((m, d), F32),
        compiler_params=_params("parallel"),
        name="mla_out",
    )(x, o, w)


def _mla(x, positions, gain, w_in, gq, w_q_up, gkv, w_kv_up, w_out):
    b, _ = positions.shape
    m, d = x.shape
    w_in_p = jnp.pad(w_in, ((0, 0), (0, LANES - QK_ROPE))).astype(BF16)
    wqt = w_q_up.T.astype(BF16)
    w_kv = w_kv_up.reshape(KV_LORA, MLA_HEADS, QK_NOPE + V_DIM)
    wkn = w_kv[:, :, :QK_NOPE].reshape(KV_LORA, -1).astype(BF16)
    wvt = w_kv[:, :, QK_NOPE:].reshape(KV_LORA, -1).T.astype(BF16)
    qt, kn, kr, vt = _mla_proj(x, positions, gain, w_in_p, gq, wqt, gkv, wkn, wvt)
    o = _mla_attn(qt, kn, kr, vt, b)
    return _mla_out(x, o.reshape(m, -1), w_out.astype(BF16))


def kernel(x, positions, ln_ffn1, ffn1_w_in, ffn1_w_out, ln_mix, ln_ffn2, ffn2_w_in, ffn2_w_out,
           sgu_w_in, sgu_v_gain, sgu_v_bias, sgu_w_spatial, sgu_b_spatial, sgu_w_out,
           mla_w_in, mla_q_norm, mla_w_q_up, mla_kv_norm, mla_w_kv_up, mla_w_out, ln_final):
    b, s, d = x.shape
    depth = ln_ffn1.shape[0]
    h = x.reshape(b * s, d)

    ffn_stacks = [(w_in, w_out, i) for i in range(depth)
                  for w_in, w_out in ((ffn1_w_in, ffn1_w_out), (ffn2_w_in, ffn2_w_out))]
    ffn_weights = {0: (ffn1_w_in[0].astype(BF16), ffn1_w_out[0].astype(BF16))}
    uncast = list(range(1, len(ffn_stacks)))

    def side_cast(call, *args, **kwargs):
        target = uncast.pop(0) if uncast else None
        out, cast = call(*args, cast_next=ffn_stacks[target] if target is not None else None,
                         **kwargs)
        if target is not None:
            ffn_weights[target] = cast
        return out

    for i in range(depth):
        h = side_cast(_ffn, h, ln_ffn1[i], *ffn_weights[2 * i])
        j = i // 2
        if i % 2 == 0:
            h = side_cast(_sgu, h, ln_mix[i], sgu_w_in[j].astype(BF16), sgu_v_gain[j],
                          sgu_v_bias[j], sgu_w_spatial[j], sgu_b_spatial[j],
                          sgu_w_out[j].astype(BF16))
        else:
            h = _mla(h, positions, ln_mix[i], mla_w_in[j], mla_q_norm[j], mla_w_q_up[j],
                     mla_kv_norm[j], mla_w_kv_up[j], mla_w_out[j])
        last = i == depth - 1
        h = side_cast(_ffn, h, ln_ffn2[i], *ffn_weights[2 * i + 1],
                      final_gain=ln_final if last else None)
    return h.reshape(b, s, d)
```

```python
import functools
import math

import jax
import jax.numpy as jnp
from jax import lax
from jax.experimental import pallas as pl
from jax.experimental.pallas import tpu as pltpu

F32 = jnp.float32
BF16 = jnp.bfloat16

EPS = 1e-6
CHUNK = 64
SGU_BLOCK = 128
SGU_GROUPS = 8
MLA_HEADS = 16
Q_LORA = 512
KV_LORA = 512
QK_NOPE = 128
QK_ROPE = 64
V_DIM = 128
QK_DIM = QK_NOPE + QK_ROPE
ROPE_THETA = 10000.0

LANES = 128
BF16_SUBLANES = 16
HEAD_PAD = 2 * LANES
VMEM_LIMIT = 61 * 1024 * 1024

FFN_TM = 1024
FFN_TF = 512
FFN_ROWS = 64
FFN_COLS = 512
SGU_IN_TM = 256
SGU_OUT_TM = 256
SGU_COLS = 1024
SGU_OUT_COLS = 512
MLA_P_TM = 256
MLA_O_TM = 512
ATT_T = 512
ATT_HEADS = 4
ATT_SUM_ROWS = 16


def _rms_normalize(x, gain):
    return x * lax.rsqrt(jnp.mean(x * x, axis=-1, keepdims=True) + EPS) * gain


def _params(*sem):
    return pltpu.CompilerParams(dimension_semantics=sem, vmem_limit_bytes=VMEM_LIMIT)


def _dot_nt(a, b):
    return lax.dot_general(a, b, (((1,), (1,)), ((), ())), preferred_element_type=F32)


def _cast_plan(stacked, layer, grid):
    _, rows, cols = stacked.shape
    steps = math.prod(grid)
    for nb in range(1, steps + 1):
        na = steps // nb
        if (na * nb == steps and rows % na == 0 and cols % nb == 0
                and (rows // na) % BF16_SUBLANES == 0 and (cols // nb) % LANES == 0):
            break
    else:
        raise ValueError(f"cannot spread a cast of {stacked.shape} over grid {grid}")
    block = (rows // na, cols // nb)

    def index(*ids):
        flat = ids[0]
        for extent, idx in zip(grid[1:], ids[1:]):
            flat = flat * extent + idx
        return flat // nb, flat % nb

    src = pl.BlockSpec((None,) + block, lambda *ids: (layer,) + index(*ids))
    dst = pl.BlockSpec(block, index)
    return src, dst, jax.ShapeDtypeStruct((rows, cols), BF16)


def _cast_blocks(src_refs, dst_refs):
    for src, dst in zip(src_refs, dst_refs):
        dst[...] = src[...].astype(BF16)


def _ffn_kernel(x_ref, g_ref, wg_ref, wu_ref, wo_ref, *rest, final_norm, n_cast):
    rest = list(rest)
    gf_ref = rest.pop(0) if final_norm else None
    cast_src = [rest.pop(0) for _ in range(n_cast)]
    o_ref = rest.pop(0)
    cast_dst = [rest.pop(0) for _ in range(n_cast)]
    (xn_ref,) = rest
    j = pl.program_id(1)
    _cast_blocks(cast_src, cast_dst)

    tm, d = x_ref.shape
    row_chunks = [slice(r, r + FFN_ROWS) for r in range(0, tm, FFN_ROWS)]

    @pl.when(j == 0)
    def _():
        for rows in row_chunks:
            xn_ref[rows, :] = _rms_normalize(x_ref[rows, :], g_ref[...]).astype(BF16)
        o_ref[...] = jnp.zeros(o_ref.shape, F32)

    xn = xn_ref[...]
    gate = jnp.dot(xn, wg_ref[...], preferred_element_type=F32)
    up = jnp.dot(xn, wu_ref[...], preferred_element_type=F32)
    act = (gate * jax.nn.sigmoid(gate) * up).astype(BF16)
    for c in range(0, d, FFN_COLS):
        o_ref[:, c:c + FFN_COLS] += jnp.dot(act, wo_ref[:, c:c + FFN_COLS],
                                            preferred_element_type=F32)

    @pl.when(j == pl.num_programs(1) - 1)
    def _():
        for rows in row_chunks:
            y = x_ref[rows, :] + 0.5 * o_ref[rows, :]
            o_ref[rows, :] = y
            if final_norm:
                rstd = lax.rsqrt(jnp.mean(y * y, axis=-1, keepdims=True) + EPS)
                o_ref[rows, :] = o_ref[rows, :] * rstd * gf_ref[...]


def _ffn(x, gain, w_in, w_out, final_gain=None, cast_next=None):
    m, d = x.shape
    f = w_out.shape[0]
    nf = f // FFN_TF
    grid = (m // FFN_TM, nf)
    final_norm = final_gain is not None
    in_specs = [
        pl.BlockSpec((FFN_TM, d), lambda i, j: (i, 0)),
        pl.BlockSpec((1, d), lambda i, j: (0, 0)),
        pl.BlockSpec((d, FFN_TF), lambda i, j: (0, j)),
        pl.BlockSpec((d, FFN_TF), lambda i, j: (0, j + nf)),
        pl.BlockSpec((FFN_TF, d), lambda i, j: (j, 0)),
    ]
    args = [x, gain.reshape(1, d), w_in, w_in, w_out]
    if final_norm:
        in_specs.append(pl.BlockSpec((1, d), lambda i, j: (0, 0)))
        args.append(final_gain.reshape(1, d))
    out_specs = [pl.BlockSpec((FFN_TM, d), lambda i, j: (i, 0))]
    out_shape = [jax.ShapeDtypeStruct((m, d), F32)]
    *stacks, layer = cast_next if cast_next is not None else (None,)
    for stacked in stacks:
        src, dst, shape = _cast_plan(stacked, layer, grid)
        in_specs.append(src)
        args.append(stacked)
        out_specs.append(dst)
        out_shape.append(shape)
    out = pl.pallas_call(
        functools.partial(_ffn_kernel, final_norm=final_norm, n_cast=len(stacks)),
        grid=grid,
        in_specs=in_specs,
        out_specs=out_specs,
        out_shape=out_shape,
        scratch_shapes=[pltpu.VMEM((FFN_TM, d), BF16)],
        compiler_params=_params("parallel", "arbitrary"),
        name="ffn_final" if final_norm else "ffn",
    )(*args)
    return out[0], tuple(out[1:])


def _resident(shape):
    return pl.BlockSpec(shape, lambda i: (0,) * len(shape), pipeline_mode=pl.Buffered(1))


def _sgu_in_kernel(x_ref, g_ref, w_ref, vg_ref, vb_ref, *rest, n_cast):
    cast_src, (u_ref, vn_ref, *rest) = rest[:n_cast], rest[n_cast:]
    cast_dst, (xn_ref, v_ref) = rest[:n_cast], rest[n_cast:]
    _cast_blocks(cast_src, cast_dst)
    tm, width = u_ref.shape
    chunks = [slice(c, c + SGU_COLS) for c in range(0, width, SGU_COLS)]
    tiles = [slice(c, c + LANES) for c in range(0, width, LANES)]

    def project(cols):
        return jnp.dot(xn_ref[...], w_ref[:, cols], preferred_element_type=F32)

    xn_ref[...] = _rms_normalize(x_ref[...], g_ref[...]).astype(BF16)

    def mean_pass():
        total = jnp.zeros((tm, LANES), F32)
        for t in tiles:
            total += v_ref[:, t]
        return jnp.broadcast_to(jnp.sum(total, axis=-1, keepdims=True) / width, (tm, LANES))

    def rstd_pass(mu):
        sq = jnp.zeros((tm, LANES), F32)
        for t in tiles:
            dv = v_ref[:, t] - mu
            sq += dv * dv
        var = jnp.sum(sq, axis=-1, keepdims=True) / width
        return jnp.broadcast_to(lax.rsqrt(var + EPS), (tm, LANES))

    def normalize(mu, rstd):
        for t in tiles:
            vn_ref[:, t] = ((v_ref[:, t] - mu) * rstd * vg_ref[:, t] + vb_ref[:, t]).astype(BF16)

    for cols in chunks:
        v_ref[:, cols] = jax.nn.gelu(project(slice(width + cols.start, width + cols.stop)))
    stats = {}
    side_work = [lambda: stats.update(mu=mean_pass()),
                 lambda: stats.update(rstd=rstd_pass(stats["mu"])),
                 lambda: normalize(stats["mu"], stats["rstd"])]
    assert len(chunks) >= len(side_work)
    for k, cols in enumerate(chunks):
        u = project(cols)
        if k < len(side_work):
            side_work[k]()
        u_ref[:, cols] = jax.nn.gelu(u).astype(BF16)


def _sgu_out_kernel(x_ref, u_ref, vn_ref, ws_ref, bs_ref, wo_ref, *rest, n_cast):
    cast_src, (o_ref, *rest) = rest[:n_cast], rest[n_cast:]
    cast_dst, (gated_ref,) = rest[:n_cast], rest[n_cast:]
    _cast_blocks(cast_src, cast_dst)
    tm, width = u_ref.shape
    d = o_ref.shape[1]
    gw = width // SGU_GROUPS
    row_chunk = lax.broadcasted_iota(jnp.int32, (SGU_BLOCK, SGU_BLOCK), 0) // CHUNK
    col_chunk = lax.broadcasted_iota(jnp.int32, (SGU_BLOCK, SGU_BLOCK), 1) // CHUNK
    visible = row_chunk >= col_chunk
    for g in range(SGU_GROUPS):
        lanes = slice(g * gw, (g + 1) * gw)
        ws = jnp.where(visible, ws_ref[g], 0.0).astype(BF16)
        bias = jnp.tile(bs_ref[g], (1, gw // LANES))
        for b in range(tm // SGU_BLOCK):
            rows = slice(b * SGU_BLOCK, (b + 1) * SGU_BLOCK)
            mixed = jnp.dot(ws, vn_ref[rows, lanes], preferred_element_type=F32) + bias
            gated_ref[rows, lanes] = (u_ref[rows, lanes].astype(F32) * mixed).astype(BF16)
    gated = gated_ref[...]
    for c in range(0, d, SGU_OUT_COLS):
        cols = slice(c, c + SGU_OUT_COLS)
        o_ref[:, cols] = x_ref[:, cols] + jnp.dot(gated, wo_ref[:, cols],
                                                  preferred_element_type=F32)


def _sgu(x, gain, w_in, v_gain, v_bias, w_spatial, b_spatial, w_out, cast_next=None):
    m, d = x.shape
    width = w_out.shape[0]
    rows = lambda tm: (lambda i: (i, 0))

    def with_cast(in_specs, args, out_specs, out_shape, stacked, n_tiles):
        if stacked is None:
            return 0
        src, dst, shape = _cast_plan(stacked, cast_next[-1], (n_tiles,))
        in_specs.append(src)
        args.append(stacked)
        out_specs.append(dst)
        out_shape.append(shape)
        return 1

    tm = SGU_IN_TM
    in_specs = [pl.BlockSpec((tm, d), rows(tm)), _resident((1, d)), _resident(w_in.shape),
                _resident((1, width)), _resident((1, width))]
    args = [x, gain.reshape(1, d), w_in, v_gain.reshape(1, width), v_bias.reshape(1, width)]
    out_specs = [pl.BlockSpec((tm, width), rows(tm)), pl.BlockSpec((tm, width), rows(tm))]
    out_shape = [jax.ShapeDtypeStruct((m, width), BF16), jax.ShapeDtypeStruct((m, width), BF16)]
    n_cast = with_cast(in_specs, args, out_specs, out_shape,
                       cast_next[0] if cast_next else None, m // tm)
    u, vn, *cast_in = pl.pallas_call(
        functools.partial(_sgu_in_kernel, n_cast=n_cast),
        grid=(m // tm,),
        in_specs=in_specs,
        out_specs=out_specs,
        out_shape=out_shape,
        scratch_shapes=[pltpu.VMEM((tm, d), BF16), pltpu.VMEM((tm, width), F32)],
        compiler_params=_params("parallel"),
        name="sgu_in",
    )(*args)

    tm = SGU_OUT_TM
    bias = jnp.broadcast_to(b_spatial[:, :, None], (SGU_GROUPS, SGU_BLOCK, LANES))
    in_specs = [pl.BlockSpec((tm, d), rows(tm)), pl.BlockSpec((tm, width), rows(tm)),
                pl.BlockSpec((tm, width), rows(tm)), _resident(w_spatial.shape),
                _resident(bias.shape), _resident(w_out.shape)]
    args = [x, u, vn, w_spatial, bias, w_out]
    out_specs = [pl.BlockSpec((tm, d), rows(tm))]
    out_shape = [jax.ShapeDtypeStruct((m, d), F32)]
    n_cast = with_cast(in_specs, args, out_specs, out_shape,
                       cast_next[1] if cast_next else None, m // tm)
    y, *cast_out = pl.pallas_call(
        functools.partial(_sgu_out_kernel, n_cast=n_cast),
        grid=(m // tm,),
        in_specs=in_specs,
        out_specs=out_specs,
        out_shape=out_shape,
        scratch_shapes=[pltpu.VMEM((tm, width), BF16)],
        compiler_params=_params("parallel"),
        name="sgu_out",
    )(*args)
    return y, tuple(cast_in + cast_out)


def _mla_proj_kernel(x_ref, posc_ref, posr_ref, freqr_ref, freqc_ref, g_ref, wi_ref, gq_ref,
                     wqt_ref, gkv_ref, wkn_ref, wvt_ref, qt_ref, kn_ref, kr_ref, vt_ref):
    half = QK_ROPE // 2
    hn = _rms_normalize(x_ref[...], g_ref[...]).astype(BF16)
    proj = jnp.dot(hn, wi_ref[...], preferred_element_type=F32)
    qn = _rms_normalize(proj[:, :Q_LORA], gq_ref[...]).astype(BF16)
    kvn = _rms_normalize(proj[:, Q_LORA:Q_LORA + KV_LORA], gkv_ref[...]).astype(BF16)
    kr = proj[:, Q_LORA + KV_LORA:]

    ang = posc_ref[...].astype(F32) * freqr_ref[...]
    lane = lax.broadcasted_iota(jnp.int32, ang.shape, 1)
    sin = jnp.sin(ang)
    sin_lo = jnp.where(lane < half, -sin, 0.0)
    sin_hi = jnp.where((lane >= half) & (lane < QK_ROPE), sin, 0.0)
    kr_ref[...] = (kr * jnp.cos(ang) + pltpu.roll(kr, LANES - half, 1) * sin_lo
                   + pltpu.roll(kr, half, 1) * sin_hi).astype(BF16)

    kn_ref[...] = jnp.dot(kvn, wkn_ref[...], preferred_element_type=F32).astype(BF16)
    vt_ref[...] = _dot_nt(wvt_ref[...], kvn).astype(BF16)

    ang_t = freqc_ref[...] * posr_ref[...].astype(F32)
    cos_t = jnp.cos(ang_t)
    sin_t = jnp.sin(ang_t)
    scale = QK_DIM ** -0.5 * math.log2(math.e)
    q_t = _dot_nt(wqt_ref[...], qn) * scale
    for h in range(MLA_HEADS):
        src = h * QK_DIM
        dst = h * HEAD_PAD
        x1 = q_t[src + QK_NOPE:src + QK_NOPE + half]
        x2 = q_t[src + QK_NOPE + half:src + QK_DIM]
        qt_ref[dst:dst + QK_NOPE, :] = q_t[src:src + QK_NOPE].astype(BF16)
        qt_ref[dst + QK_NOPE:dst + QK_NOPE + half, :] = (x1 * cos_t - x2 * sin_t).astype(BF16)
        qt_ref[dst + QK_NOPE + half:dst + QK_DIM, :] = (x1 * sin_t + x2 * cos_t).astype(BF16)
        qt_ref[dst + QK_DIM:dst + HEAD_PAD, :] = jnp.zeros((HEAD_PAD - QK_DIM, q_t.shape[1]), BF16)


def _mla_proj(x, positions, gain, w_in, gq, wqt, gkv, wkn, wvt):
    m, d = x.shape
    tm = MLA_P_TM
    r = ATT_T // tm
    nt = m // ATT_T
    half = QK_ROPE // 2
    inv_freq = 1.0 / (ROPE_THETA ** (jnp.arange(half, dtype=F32) / half))
    freq_row = jnp.concatenate([inv_freq, inv_freq, jnp.zeros((LANES - QK_ROPE,), F32)])
    const = lambda i: (0, 0)

    def resident(shape):
        return pl.BlockSpec(shape, const, pipeline_mode=pl.Buffered(1))

    return pl.pallas_call(
        _mla_proj_kernel,
        grid=(m // tm,),
        in_specs=[
            pl.BlockSpec((tm, d), lambda i: (i, 0)),
            pl.BlockSpec((tm, 1), lambda i: (i, 0)),
            pl.BlockSpec((None, 1, tm), lambda i: (i, 0, 0)),
            resident((1, LANES)),
            resident((half, 1)),
            resident((1, d)),
            resident(w_in.shape),
            resident((1, Q_LORA)),
            resident(wqt.shape),
            resident((1, KV_LORA)),
            resident(wkn.shape),
            resident(wvt.shape),
        ],
        out_specs=[
            pl.BlockSpec((None, MLA_HEADS * HEAD_PAD, tm), lambda i: (i // r, 0, i % r)),
            pl.BlockSpec((tm, MLA_HEADS * QK_NOPE), lambda i: (i, 0)),
            pl.BlockSpec((tm, LANES), lambda i: (i, 0)),
            pl.BlockSpec((None, MLA_HEADS * V_DIM, tm), lambda i: (i // r, 0, i % r)),
        ],
        out_shape=[
            jax.ShapeDtypeStruct((nt, MLA_HEADS * HEAD_PAD, ATT_T), BF16),
            jax.ShapeDtypeStruct((m, MLA_HEADS * QK_NOPE), BF16),
            jax.ShapeDtypeStruct((m, LANES), BF16),
            jax.ShapeDtypeStruct((nt, MLA_HEADS * V_DIM, ATT_T), BF16),
        ],
        compiler_params=_params("parallel"),
        name="mla_proj",
    )(x, positions.reshape(m, 1), positions.reshape(m // tm, 1, tm), freq_row.reshape(1, LANES),
      inv_freq.reshape(half, 1), gain.reshape(1, d), w_in, gq.reshape(1, -1), wqt,
      gkv.reshape(1, -1), wkn, wvt)


def _mla_attn_kernel(qt_ref, kn_ref, kr_ref, vt_ref, o_ref,
                     m_ref, acc_ref, s_ref, p_ref, alpha_ref):
    i = pl.program_id(2)
    t = qt_ref.shape[1]
    last = ATT_HEADS - 1

    def visible_only(s):
        key_chunk = lax.broadcasted_iota(jnp.int32, (t, 1), 0) // CHUNK
        qry_chunk = lax.broadcasted_iota(jnp.int32, (1, t), 1) // CHUNK
        return jnp.where(key_chunk <= qry_chunk, s, -jnp.inf)

    def scores(j, a):
        rows = pl.ds(pl.multiple_of(j * t, t), t)
        k = jnp.concatenate([kn_ref[rows, a * QK_NOPE:(a + 1) * QK_NOPE], kr_ref[rows, :]], axis=1)
        return jnp.dot(k, qt_ref[a * HEAD_PAD:(a + 1) * HEAD_PAD, :],
                       preferred_element_type=F32)

    ones_rows = jnp.ones((ATT_SUM_ROWS, t), BF16)

    def weighted_values(j, a, p, alpha):
        v_ext = jnp.concatenate([vt_ref[j, a * V_DIM:(a + 1) * V_DIM, :], ones_rows], axis=0)
        pv = jnp.dot(v_ext, p, preferred_element_type=F32)
        acc_ref[a] = alpha * acc_ref[a] + pv

    def softmax(a, s):
        m_prev = m_ref[a]
        m_new = jnp.maximum(m_prev, jnp.max(s, axis=0, keepdims=True))
        alpha = jnp.exp2(m_prev - m_new)
        p = jnp.exp2((s - m_new).astype(BF16))
        m_ref[a] = m_new
        return p, alpha

    def key_tile(j, diagonal):
        s_cur = s_ref[...]
        p_prev, alpha_prev = p_ref[...], alpha_ref[...]
        for a in range(ATT_HEADS):
            if a < last:
                s_nxt = scores(j, a + 1)
                if diagonal:
                    s_nxt = visible_only(s_nxt)
            elif not diagonal:
                s_nxt = scores(j + 1, 0)
            weighted_values(jnp.maximum(j - 1, 0) if a == 0 else j, (a - 1) % ATT_HEADS,
                            p_prev, alpha_prev)
            p_prev, alpha_prev = softmax(a, s_cur)
            s_cur = s_nxt
        return s_nxt, p_prev, alpha_prev

    m_ref[...] = jnp.full(m_ref.shape, -jnp.inf, F32)
    acc_ref[...] = jnp.zeros(acc_ref.shape, F32)
    p_ref[...] = jnp.zeros(p_ref.shape, BF16)
    alpha_ref[...] = jnp.ones(alpha_ref.shape, F32)
    s_ref[...] = scores(0, 0)

    def body(j, carry):
        s_nxt, p, alpha = key_tile(j, diagonal=False)
        s_ref[...] = s_nxt
        p_ref[...] = p
        alpha_ref[...] = alpha
        return carry

    lax.fori_loop(0, i, body, 0)
    s_ref[...] = visible_only(s_ref[...])
    _, p, alpha = key_tile(i, diagonal=True)
    weighted_values(i, last, p, alpha)

    for a in range(ATT_HEADS):
        o = acc_ref[a, :V_DIM, :] / acc_ref[a, V_DIM:V_DIM + 1, :]
        o_ref[:, a * V_DIM:(a + 1) * V_DIM] = o.T.astype(o_ref.dtype)


def _mla_attn(qt, kn, kr, vt, batch):
    nt, _, t = qt.shape
    nq = nt // batch
    s = nq * t
    ha = ATT_HEADS
    return pl.pallas_call(
        _mla_attn_kernel,
        grid=(batch, MLA_HEADS // ha, nq),
        in_specs=[
            pl.BlockSpec((None, ha * HEAD_PAD, t), lambda b, h, i: (b * nq + i, h, 0)),
            pl.BlockSpec((None, s, ha * QK_NOPE), lambda b, h, i: (b, 0, h)),
            pl.BlockSpec((None, s, LANES), lambda b, h, i: (b, 0, 0)),
            pl.BlockSpec((None, nq, ha * V_DIM, t), lambda b, h, i: (b, 0, h, 0)),
        ],
        out_specs=pl.BlockSpec((None, t, ha * V_DIM), lambda b, h, i: (b, i, h)),
        out_shape=jax.ShapeDtypeStruct((batch, s, MLA_HEADS * V_DIM), BF16),
        scratch_shapes=[
            pltpu.VMEM((ha, 1, t), F32),
            pltpu.VMEM((ha, V_DIM + ATT_SUM_ROWS, t), F32),
            pltpu.VMEM((t, t), F32),
            pltpu.VMEM((t, t), BF16),
            pltpu.VMEM((1, t), F32),
        ],
        compiler_params=_params("parallel", "parallel", "arbitrary"),
        name="mla_attn",
    )(qt, kn.reshape(batch, s, -1), kr.reshape(batch, s, LANES),
      vt.reshape(batch, nq, MLA_HEADS * V_DIM, t))


def _mla_out_kernel(x_ref, o_ref, w_ref, y_ref):
    y_ref[...] = x_ref[...] + jnp.dot(o_ref[...], w_ref[...], preferred_element_type=F32)


def _mla_out(x, o, w):
    m, d = x.shape
    tm = MLA_O_TM
    return pl.pallas_call(
        _mla_out_kernel,
        grid=(m // tm,),
        in_specs=[
            pl.BlockSpec((tm, d), lambda i: (i, 0)),
            pl.BlockSpec((tm, o.shape[1]), lambda i: (i, 0)),
            pl.BlockSpec(w.shape, lambda i: (0, 0)),
        ],
        out_specs=pl.BlockSpec((tm, d), lambda i: (i, 0)),
        out_shape=jax.ShapeDtypeStruct((m, d), F32),
        compiler_params=_params("parallel"),
        name="mla_out",
    )(x, o, w)


def _mla(x, positions, gain, w_in, gq, w_q_up, gkv, w_kv_up, w_out):
    b, _ = positions.shape
    m, d = x.shape
    w_in_p = jnp.pad(w_in, ((0, 0), (0, LANES - QK_ROPE))).astype(BF16)
    wqt = w_q_up.T.astype(BF16)
    w_kv = w_kv_up.reshape(KV_LORA, MLA_HEADS, QK_NOPE + V_DIM)
    wkn = w_kv[:, :, :QK_NOPE].reshape(KV_LORA, -1).astype(BF16)
    wvt = w_kv[:, :, QK_NOPE:].reshape(KV_LORA, -1).T.astype(BF16)
    qt, kn, kr, vt = _mla_proj(x, positions, gain, w_in_p, gq, wqt, gkv, wkn, wvt)
    o = _mla_attn(qt, kn, kr, vt, b)
    return _mla_out(x, o.reshape(m, -1), w_out.astype(BF16))


def kernel(x, positions, ln_ffn1, ffn1_w_in, ffn1_w_out, ln_mix, ln_ffn2, ffn2_w_in, ffn2_w_out,
           sgu_w_in, sgu_v_gain, sgu_v_bias, sgu_w_spatial, sgu_b_spatial, sgu_w_out,
           mla_w_in, mla_q_norm, mla_w_q_up, mla_kv_norm, mla_w_kv_up, mla_w_out, ln_final):
    b, s, d = x.shape
    depth = ln_ffn1.shape[0]
    h = x.reshape(b * s, d)

    ffn_stacks = [(w_in, w_out, i) for i in range(depth)
                  for w_in, w_out in ((ffn1_w_in, ffn1_w_out), (ffn2_w_in, ffn2_w_out))]
    ffn_weights = {0: (ffn1_w_in[0].astype(BF16), ffn1_w_out[0].astype(BF16))}
    uncast = list(range(1, len(ffn_stacks)))

    def side_cast(call, *args, **kwargs):
        target = uncast.pop(0) if uncast else None
        out, cast = call(*args, cast_next=ffn_stacks[target] if target is not None else None,
                         **kwargs)
        if target is not None:
            ffn_weights[target] = cast
        return out

    for i in range(depth):
        h = side_cast(_ffn, h, ln_ffn1[i], *ffn_weights[2 * i])
        j = i // 2
        if i % 2 == 0:
            h = side_cast(_sgu, h, ln_mix[i], sgu_w_in[j].astype(BF16), sgu_v_gain[j],
                          sgu_v_bias[j], sgu_w_spatial[j], sgu_b_spatial[j],
                          sgu_w_out[j].astype(BF16))
        else:
            h = _mla(h, positions, ln_mix[i], mla_w_in[j], mla_q_norm[j], mla_w_q_up[j],
                     mla_kv_norm[j], mla_w_kv_up[j], mla_w_out[j])
        last = i == depth - 1
        h = side_cast(_ffn, h, ln_ffn2[i], *ffn_weights[2 * i + 1],
                      final_gain=ln_final if last else None)
    return h.reshape(b, s, d)
```

```python
import functools
import math

import jax
import jax.numpy as jnp
from jax import lax
from jax.experimental import pallas as pl
from jax.experimental.pallas import tpu as pltpu

F32 = jnp.float32
BF16 = jnp.bfloat16

EPS = 1e-6
CHUNK = 64
SGU_BLOCK = 128
SGU_GROUPS = 8
MLA_HEADS = 16
Q_LORA = 512
KV_LORA = 512
QK_NOPE = 128
QK_ROPE = 64
V_DIM = 128
QK_DIM = QK_NOPE + QK_ROPE
ROPE_THETA = 10000.0

LANES = 128
BF16_SUBLANES = 16
HEAD_PAD = 2 * LANES
VMEM_LIMIT = 61 * 1024 * 1024

FFN_TM = 1024
FFN_TF = 512
FFN_ROWS = 64
FFN_COLS = 512
SGU_IN_TM = 256
SGU_OUT_TM = 256
SGU_COLS = 1024
SGU_OUT_COLS = 512
MLA_P_TM = 256
MLA_O_TM = 512
ATT_T = 512
ATT_HEADS = 4
ATT_SUM_ROWS = 16


def _rms_normalize(x, gain):
    return x * lax.rsqrt(jnp.mean(x * x, axis=-1, keepdims=True) + EPS) * gain


def _params(*sem):
    return pltpu.CompilerParams(dimension_semantics=sem, vmem_limit_bytes=VMEM_LIMIT)


def _dot_nt(a, b):
    return lax.dot_general(a, b, (((1,), (1,)), ((), ())), preferred_element_type=F32)


def _cast_plan(stacked, layer, grid):
    _, rows, cols = stacked.shape
    steps = math.prod(grid)

    def split(n):
        for nb in range(1, n + 1):
            na = n // nb
            if (na * nb == n and rows % na == 0 and cols % nb == 0
                    and (rows // na) % BF16_SUBLANES == 0 and (cols // nb) % LANES == 0):
                return na, nb
        return None

    used = next(n for n in range(steps, 0, -1) if split(n))
    na, nb = split(used)
    block = (rows // na, cols // nb)

    def index(*ids):
        flat = ids[0]
        for extent, idx in zip(grid[1:], ids[1:]):
            flat = flat * extent + idx
        if used < steps:
            flat = jnp.minimum(flat, used - 1)
        return flat // nb, flat % nb

    src = pl.BlockSpec((None,) + block, lambda *ids: (layer,) + index(*ids))
    dst = pl.BlockSpec(block, index)
    return src, dst, jax.ShapeDtypeStruct((rows, cols), BF16)


def _cast_blocks(src_refs, dst_refs):
    for src, dst in zip(src_refs, dst_refs):
        dst[...] = src[...].astype(BF16)


def _ffn_kernel(x_ref, g_ref, wg_ref, wu_ref, wo_ref, *rest, final_norm, n_cast):
    rest = list(rest)
    gf_ref = rest.pop(0) if final_norm else None
    cast_src = [rest.pop(0) for _ in range(n_cast)]
    o_ref = rest.pop(0)
    cast_dst = [rest.pop(0) for _ in range(n_cast)]
    (xn_ref,) = rest
    j = pl.program_id(1)
    _cast_blocks(cast_src, cast_dst)

    tm, d = x_ref.shape
    row_chunks = [slice(r, r + FFN_ROWS) for r in range(0, tm, FFN_ROWS)]

    @pl.when(j == 0)
    def _():
        for rows in row_chunks:
            xn_ref[rows, :] = _rms_normalize(x_ref[rows, :], g_ref[...]).astype(BF16)
        o_ref[...] = jnp.zeros(o_ref.shape, F32)

    xn = xn_ref[...]
    gate = jnp.dot(xn, wg_ref[...], preferred_element_type=F32)
    up = jnp.dot(xn, wu_ref[...], preferred_element_type=F32)
    act = (gate * jax.nn.sigmoid(gate) * up).astype(BF16)
    for c in range(0, d, FFN_COLS):
        o_ref[:, c:c + FFN_COLS] += jnp.dot(act, wo_ref[:, c:c + FFN_COLS],
                                            preferred_element_type=F32)

    @pl.when(j == pl.num_programs(1) - 1)
    def _():
        for rows in row_chunks:
            y = x_ref[rows, :] + 0.5 * o_ref[rows, :]
            o_ref[rows, :] = y
            if final_norm:
                rstd = lax.rsqrt(jnp.mean(y * y, axis=-1, keepdims=True) + EPS)
                o_ref[rows, :] = o_ref[rows, :] * rstd * gf_ref[...]


def _ffn(x, gain, w_in, w_out, final_gain=None, cast_next=()):
    m, d = x.shape
    f = w_out.shape[0]
    nf = f // FFN_TF
    grid = (m // FFN_TM, nf)
    final_norm = final_gain is not None
    in_specs = [
        pl.BlockSpec((FFN_TM, d), lambda i, j: (i, 0)),
        pl.BlockSpec((1, d), lambda i, j: (0, 0)),
        pl.BlockSpec((d, FFN_TF), lambda i, j: (0, j)),
        pl.BlockSpec((d, FFN_TF), lambda i, j: (0, j + nf)),
        pl.BlockSpec((FFN_TF, d), lambda i, j: (j, 0)),
    ]
    args = [x, gain.reshape(1, d), w_in, w_in, w_out]
    if final_norm:
        in_specs.append(pl.BlockSpec((1, d), lambda i, j: (0, 0)))
        args.append(final_gain.reshape(1, d))
    out_specs = [pl.BlockSpec((FFN_TM, d), lambda i, j: (i, 0))]
    out_shape = [jax.ShapeDtypeStruct((m, d), F32)]
    stacks = [stacked for stacked, _ in cast_next]
    for stacked, layer in cast_next:
        src, dst, shape = _cast_plan(stacked, layer, grid)
        in_specs.append(src)
        args.append(stacked)
        out_specs.append(dst)
        out_shape.append(shape)
    out = pl.pallas_call(
        functools.partial(_ffn_kernel, final_norm=final_norm, n_cast=len(stacks)),
        grid=grid,
        in_specs=in_specs,
        out_specs=out_specs,
        out_shape=out_shape,
        scratch_shapes=[pltpu.VMEM((FFN_TM, d), BF16)],
        compiler_params=_params("parallel", "arbitrary"),
        name="ffn_final" if final_norm else "ffn",
    )(*args)
    return out[0], tuple(out[1:])


def _resident(shape):
    return pl.BlockSpec(shape, lambda i: (0,) * len(shape), pipeline_mode=pl.Buffered(1))


def _sgu_in_kernel(x_ref, g_ref, w_ref, vg_ref, vb_ref, *rest, n_cast):
    cast_src, (u_ref, vn_ref, *rest) = rest[:n_cast], rest[n_cast:]
    cast_dst, (xn_ref, v_ref) = rest[:n_cast], rest[n_cast:]
    _cast_blocks(cast_src, cast_dst)
    tm, width = u_ref.shape
    chunks = [slice(c, c + SGU_COLS) for c in range(0, width, SGU_COLS)]
    tiles = [slice(c, c + LANES) for c in range(0, width, LANES)]

    def project(cols):
        return jnp.dot(xn_ref[...], w_ref[:, cols], preferred_element_type=F32)

    xn_ref[...] = _rms_normalize(x_ref[...], g_ref[...]).astype(BF16)

    def mean_pass():
        total = jnp.zeros((tm, LANES), F32)
        for t in tiles:
            total += v_ref[:, t]
        return jnp.broadcast_to(jnp.sum(total, axis=-1, keepdims=True) / width, (tm, LANES))

    def rstd_pass(mu):
        sq = jnp.zeros((tm, LANES), F32)
        for t in tiles:
            dv = v_ref[:, t] - mu
            sq += dv * dv
        var = jnp.sum(sq, axis=-1, keepdims=True) / width
        return jnp.broadcast_to(lax.rsqrt(var + EPS), (tm, LANES))

    def normalize(mu, rstd):
        for t in tiles:
            vn_ref[:, t] = ((v_ref[:, t] - mu) * rstd * vg_ref[:, t] + vb_ref[:, t]).astype(BF16)

    for cols in chunks:
        v_ref[:, cols] = jax.nn.gelu(project(slice(width + cols.start, width + cols.stop)))
    stats = {}
    side_work = [lambda: stats.update(mu=mean_pass()),
                 lambda: stats.update(rstd=rstd_pass(stats["mu"])),
                 lambda: normalize(stats["mu"], stats["rstd"])]
    assert len(chunks) >= len(side_work)
    for k, cols in enumerate(chunks):
        u = project(cols)
        if k < len(side_work):
            side_work[k]()
        u_ref[:, cols] = jax.nn.gelu(u).astype(BF16)


def _sgu_out_kernel(x_ref, u_ref, vn_ref, ws_ref, bs_ref, wo_ref, *rest, n_cast):
    cast_src, (o_ref, *rest) = rest[:n_cast], rest[n_cast:]
    cast_dst, (gated_ref,) = rest[:n_cast], rest[n_cast:]
    _cast_blocks(cast_src, cast_dst)
    tm, width = u_ref.shape
    d = o_ref.shape[1]
    gw = width // SGU_GROUPS
    row_chunk = lax.broadcasted_iota(jnp.int32, (SGU_BLOCK, SGU_BLOCK), 0) // CHUNK
    col_chunk = lax.broadcasted_iota(jnp.int32, (SGU_BLOCK, SGU_BLOCK), 1) // CHUNK
    visible = row_chunk >= col_chunk
    for g in range(SGU_GROUPS):
        lanes = slice(g * gw, (g + 1) * gw)
        ws = jnp.where(visible, ws_ref[g], 0.0).astype(BF16)
        bias = jnp.tile(bs_ref[g], (1, gw // LANES))
        for b in range(tm // SGU_BLOCK):
            rows = slice(b * SGU_BLOCK, (b + 1) * SGU_BLOCK)
            mixed = jnp.dot(ws, vn_ref[rows, lanes], preferred_element_type=F32) + bias
            gated_ref[rows, lanes] = (u_ref[rows, lanes].astype(F32) * mixed).astype(BF16)
    gated = gated_ref[...]
    for c in range(0, d, SGU_OUT_COLS):
        cols = slice(c, c + SGU_OUT_COLS)
        o_ref[:, cols] = x_ref[:, cols] + jnp.dot(gated, wo_ref[:, cols],
                                                  preferred_element_type=F32)


def _sgu(x, gain, w_in, v_gain, v_bias, w_spatial, b_spatial, w_out, cast_next=()):
    m, d = x.shape
    width = w_out.shape[0]
    rows = lambda tm: (lambda i: (i, 0))

    def with_cast(in_specs, args, out_specs, out_shape, pairs, n_tiles):
        for stacked, layer in pairs:
            src, dst, shape = _cast_plan(stacked, layer, (n_tiles,))
            in_specs.append(src)
            args.append(stacked)
            out_specs.append(dst)
            out_shape.append(shape)
        return len(pairs)

    tm = SGU_IN_TM
    in_specs = [pl.BlockSpec((tm, d), rows(tm)), _resident((1, d)), _resident(w_in.shape),
                _resident((1, width)), _resident((1, width))]
    args = [x, gain.reshape(1, d), w_in, v_gain.reshape(1, width), v_bias.reshape(1, width)]
    out_specs = [pl.BlockSpec((tm, width), rows(tm)), pl.BlockSpec((tm, width), rows(tm))]
    out_shape = [jax.ShapeDtypeStruct((m, width), BF16), jax.ShapeDtypeStruct((m, width), BF16)]
    n_cast = with_cast(in_specs, args, out_specs, out_shape, cast_next[:1], m // tm)
    u, vn, *cast_in = pl.pallas_call(
        functools.partial(_sgu_in_kernel, n_cast=n_cast),
        grid=(m // tm,),
        in_specs=in_specs,
        out_specs=out_specs,
        out_shape=out_shape,
        scratch_shapes=[pltpu.VMEM((tm, d), BF16), pltpu.VMEM((tm, width), F32)],
        compiler_params=_params("parallel"),
        name="sgu_in",
    )(*args)

    tm = SGU_OUT_TM
    bias = jnp.broadcast_to(b_spatial[:, :, None], (SGU_GROUPS, SGU_BLOCK, LANES))
    in_specs = [pl.BlockSpec((tm, d), rows(tm)), pl.BlockSpec((tm, width), rows(tm)),
                pl.BlockSpec((tm, width), rows(tm)), _resident(w_spatial.shape),
                _resident(bias.shape), _resident(w_out.shape)]
    args = [x, u, vn, w_spatial, bias, w_out]
    out_specs = [pl.BlockSpec((tm, d), rows(tm))]
    out_shape = [jax.ShapeDtypeStruct((m, d), F32)]
    n_cast = with_cast(in_specs, args, out_specs, out_shape, cast_next[1:], m // tm)
    y, *cast_out = pl.pallas_call(
        functools.partial(_sgu_out_kernel, n_cast=n_cast),
        grid=(m // tm,),
        in_specs=in_specs,
        out_specs=out_specs,
        out_shape=out_shape,
        scratch_shapes=[pltpu.VMEM((tm, width), BF16)],
        compiler_params=_params("parallel"),
        name="sgu_out",
    )(*args)
    return y, tuple(cast_in + cast_out)


def _mla_proj_kernel(x_ref, posc_ref, posr_ref, freqr_ref, freqc_ref, g_ref, wi_ref, gq_ref,
                     wqt_ref, gkv_ref, wkn_ref, wvt_ref, qt_ref, kn_ref, kr_ref, vt_ref):
    half = QK_ROPE // 2
    hn = _rms_normalize(x_ref[...], g_ref[...]).astype(BF16)
    proj = jnp.dot(hn, wi_ref[...], preferred_element_type=F32)
    qn = _rms_normalize(proj[:, :Q_LORA], gq_ref[...]).astype(BF16)
    kvn = _rms_normalize(proj[:, Q_LORA:Q_LORA + KV_LORA], gkv_ref[...]).astype(BF16)
    kr = proj[:, Q_LORA + KV_LORA:]

    ang = posc_ref[...].astype(F32) * freqr_ref[...]
    lane = lax.broadcasted_iota(jnp.int32, ang.shape, 1)
    sin = jnp.sin(ang)
    sin_lo = jnp.where(lane < half, -sin, 0.0)
    sin_hi = jnp.where((lane >= half) & (lane < QK_ROPE), sin, 0.0)
    kr_ref[...] = (kr * jnp.cos(ang) + pltpu.roll(kr, LANES - half, 1) * sin_lo
                   + pltpu.roll(kr, half, 1) * sin_hi).astype(BF16)

    kn_ref[...] = jnp.dot(kvn, wkn_ref[...], preferred_element_type=F32).astype(BF16)
    vt_ref[...] = _dot_nt(wvt_ref[...], kvn).astype(BF16)

    ang_t = freqc_ref[...] * posr_ref[...].astype(F32)
    cos_t = jnp.cos(ang_t)
    sin_t = jnp.sin(ang_t)
    scale = QK_DIM ** -0.5 * math.log2(math.e)
    q_t = _dot_nt(wqt_ref[...], qn) * scale
    for h in range(MLA_HEADS):
        src = h * QK_DIM
        dst = h * HEAD_PAD
        x1 = q_t[src + QK_NOPE:src + QK_NOPE + half]
        x2 = q_t[src + QK_NOPE + half:src + QK_DIM]
        qt_ref[dst:dst + QK_NOPE, :] = q_t[src:src + QK_NOPE].astype(BF16)
        qt_ref[dst + QK_NOPE:dst + QK_NOPE + half, :] = (x1 * cos_t - x2 * sin_t).astype(BF16)
        qt_ref[dst + QK_NOPE + half:dst + QK_DIM, :] = (x1 * sin_t + x2 * cos_t).astype(BF16)
        qt_ref[dst + QK_DIM:dst + HEAD_PAD, :] = jnp.zeros((HEAD_PAD - QK_DIM, q_t.shape[1]), BF16)


def _mla_proj(x, positions, gain, w_in, gq, wqt, gkv, wkn, wvt):
    m, d = x.shape
    tm = MLA_P_TM
    r = ATT_T // tm
    nt = m // ATT_T
    half = QK_ROPE // 2
    inv_freq = 1.0 / (ROPE_THETA ** (jnp.arange(half, dtype=F32) / half))
    freq_row = jnp.concatenate([inv_freq, inv_freq, jnp.zeros((LANES - QK_ROPE,), F32)])
    const = lambda i: (0, 0)

    def resident(shape):
        return pl.BlockSpec(shape, const, pipeline_mode=pl.Buffered(1))

    return pl.pallas_call(
        _mla_proj_kernel,
        grid=(m // tm,),
        in_specs=[
            pl.BlockSpec((tm, d), lambda i: (i, 0)),
            pl.BlockSpec((tm, 1), lambda i: (i, 0)),
            pl.BlockSpec((None, 1, tm), lambda i: (i, 0, 0)),
            resident((1, LANES)),
            resident((half, 1)),
            resident((1, d)),
            resident(w_in.shape),
            resident((1, Q_LORA)),
            resident(wqt.shape),
            resident((1, KV_LORA)),
            resident(wkn.shape),
            resident(wvt.shape),
        ],
        out_specs=[
            pl.BlockSpec((None, MLA_HEADS * HEAD_PAD, tm), lambda i: (i // r, 0, i % r)),
            pl.BlockSpec((tm, MLA_HEADS * QK_NOPE), lambda i: (i, 0)),
            pl.BlockSpec((tm, LANES), lambda i: (i, 0)),
            pl.BlockSpec((None, MLA_HEADS * V_DIM, tm), lambda i: (i // r, 0, i % r)),
        ],
        out_shape=[
            jax.ShapeDtypeStruct((nt, MLA_HEADS * HEAD_PAD, ATT_T), BF16),
            jax.ShapeDtypeStruct((m, MLA_HEADS * QK_NOPE), BF16),
            jax.ShapeDtypeStruct((m, LANES), BF16),
            jax.ShapeDtypeStruct((nt, MLA_HEADS * V_DIM, ATT_T), BF16),
        ],
        compiler_params=_params("parallel"),
        name="mla_proj",
    )(x, positions.reshape(m, 1), positions.reshape(m // tm, 1, tm), freq_row.reshape(1, LANES),
      inv_freq.reshape(half, 1), gain.reshape(1, d), w_in, gq.reshape(1, -1), wqt,
      gkv.reshape(1, -1), wkn, wvt)


def _mla_attn_kernel(qt_ref, kn_ref, kr_ref, vt_ref, o_ref,
                     m_ref, acc_ref, s_ref, p_ref, alpha_ref):
    i = pl.program_id(2)
    t = qt_ref.shape[1]
    last = ATT_HEADS - 1

    def visible_only(s):
        key_chunk = lax.broadcasted_iota(jnp.int32, (t, 1), 0) // CHUNK
        qry_chunk = lax.broadcasted_iota(jnp.int32, (1, t), 1) // CHUNK
        return jnp.where(key_chunk <= qry_chunk, s, -jnp.inf)

    def scores(j, a):
        rows = pl.ds(pl.multiple_of(j * t, t), t)
        k = jnp.concatenate([kn_ref[rows, a * QK_NOPE:(a + 1) * QK_NOPE], kr_ref[rows, :]], axis=1)
        return jnp.dot(k, qt_ref[a * HEAD_PAD:(a + 1) * HEAD_PAD, :],
                       preferred_element_type=F32)

    ones_rows = jnp.ones((ATT_SUM_ROWS, t), BF16)

    def weighted_values(j, a, p, alpha):
        v_ext = jnp.concatenate([vt_ref[j, a * V_DIM:(a + 1) * V_DIM, :], ones_rows], axis=0)
        pv = jnp.dot(v_ext, p, preferred_element_type=F32)
        acc_ref[a] = alpha * acc_ref[a] + pv

    def softmax(a, s):
        m_prev = m_ref[a]
        m_new = jnp.maximum(m_prev, jnp.max(s, axis=0, keepdims=True))
        alpha = jnp.exp2(m_prev - m_new)
        p = jnp.exp2(s - m_new)
        m_ref[a] = m_new
        return p.astype(BF16), alpha

    def key_tile(j, diagonal, carry=None):
        s_cur, p_prev, alpha_prev = carry or (s_ref[...], p_ref[...], alpha_ref[...])
        for a in range(ATT_HEADS):
            if a < last:
                s_nxt = scores(j, a + 1)
                if diagonal:
                    s_nxt = visible_only(s_nxt)
            elif not diagonal:
                s_nxt = scores(j + 1, 0)
            weighted_values(jnp.maximum(j - 1, 0) if a == 0 else j, (a - 1) % ATT_HEADS,
                            p_prev, alpha_prev)
            p_prev, alpha_prev = softmax(a, s_cur)
            s_cur = s_nxt
        return s_nxt, p_prev, alpha_prev

    m_ref[...] = jnp.full(m_ref.shape, -jnp.inf, F32)
    acc_ref[...] = jnp.zeros(acc_ref.shape, F32)
    p_ref[...] = jnp.zeros(p_ref.shape, BF16)
    alpha_ref[...] = jnp.ones(alpha_ref.shape, F32)
    s_ref[...] = scores(0, 0)

    def save(carry):
        s_ref[...], p_ref[...], alpha_ref[...] = carry

    def pair(jj, carry):
        save(key_tile(2 * jj + 1, False, key_tile(2 * jj, False)))
        return carry

    lax.fori_loop(0, i // 2, pair, 0)

    @pl.when(i % 2 == 1)
    def _():
        save(key_tile(i - 1, False))

    s_ref[...] = visible_only(s_ref[...])
    _, p, alpha = key_tile(i, diagonal=True)
    weighted_values(i, last, p, alpha)

    for a in range(ATT_HEADS):
        o = acc_ref[a, :V_DIM, :] / acc_ref[a, V_DIM:V_DIM + 1, :]
        o_ref[:, a * V_DIM:(a + 1) * V_DIM] = o.T.astype(o_ref.dtype)


def _mla_attn(qt, kn, kr, vt, batch):
    nt, _, t = qt.shape
    nq = nt // batch
    s = nq * t
    ha = ATT_HEADS
    return pl.pallas_call(
        _mla_attn_kernel,
        grid=(batch, MLA_HEADS // ha, nq),
        in_specs=[
            pl.BlockSpec((None, ha * HEAD_PAD, t), lambda b, h, i: (b * nq + i, h, 0)),
            pl.BlockSpec((None, s, ha * QK_NOPE), lambda b, h, i: (b, 0, h)),
            pl.BlockSpec((None, s, LANES), lambda b, h, i: (b, 0, 0)),
            pl.BlockSpec((None, nq, ha * V_DIM, t), lambda b, h, i: (b, 0, h, 0)),
        ],
        out_specs=pl.BlockSpec((None, t, ha * V_DIM), lambda b, h, i: (b, i, h)),
        out_shape=jax.ShapeDtypeStruct((batch, s, MLA_HEADS * V_DIM), BF16),
        scratch_shapes=[
            pltpu.VMEM((ha, 1, t), F32),
            pltpu.VMEM((ha, V_DIM + ATT_SUM_ROWS, t), F32),
            pltpu.VMEM((t, t), F32),
            pltpu.VMEM((t, t), BF16),
            pltpu.VMEM((1, t), F32),
        ],
        compiler_params=_params("parallel", "parallel", "arbitrary"),
        name="mla_attn",
    )(qt, kn.reshape(batch, s, -1), kr.reshape(batch, s, LANES),
      vt.reshape(batch, nq, MLA_HEADS * V_DIM, t))


def _mla_out_kernel(x_ref, o_ref, w_ref, y_ref):
    y_ref[...] = x_ref[...] + jnp.dot(o_ref[...], w_ref[...], preferred_element_type=F32)


def _mla_out(x, o, w):
    m, d = x.shape
    tm = MLA_O_TM
    return pl.pallas_call(
        _mla_out_kernel,
        grid=(m // tm,),
        in_specs=[
            pl.BlockSpec((tm, d), lambda i: (i, 0)),
            pl.BlockSpec((tm, o.shape[1]), lambda i: (i, 0)),
            pl.BlockSpec(w.shape, lambda i: (0, 0)),
        ],
        out_specs=pl.BlockSpec((tm, d), lambda i: (i, 0)),
        out_shape=jax.ShapeDtypeStruct((m, d), F32),
        compiler_params=_params("parallel"),
        name="mla_out",
    )(x, o, w)


def _mla(x, positions, gain, w_in, gq, w_q_up, gkv, w_kv_up, w_out):
    b, _ = positions.shape
    m, d = x.shape
    w_in_p = jnp.pad(w_in, ((0, 0), (0, LANES - QK_ROPE))).astype(BF16)
    wqt = w_q_up.T.astype(BF16)
    w_kv = w_kv_up.reshape(KV_LORA, MLA_HEADS, QK_NOPE + V_DIM)
    wkn = w_kv[:, :, :QK_NOPE].reshape(KV_LORA, -1).astype(BF16)
    wvt = w_kv[:, :, QK_NOPE:].reshape(KV_LORA, -1).T.astype(BF16)
    qt, kn, kr, vt = _mla_proj(x, positions, gain, w_in_p, gq, wqt, gkv, wkn, wvt)
    o = _mla_attn(qt, kn, kr, vt, b)
    return _mla_out(x, o.reshape(m, -1), w_out.astype(BF16))


def kernel(x, positions, ln_ffn1, ffn1_w_in, ffn1_w_out, ln_mix, ln_ffn2, ffn2_w_in, ffn2_w_out,
           sgu_w_in, sgu_v_gain, sgu_v_bias, sgu_w_spatial, sgu_b_spatial, sgu_w_out,
           mla_w_in, mla_q_norm, mla_w_q_up, mla_kv_norm, mla_w_kv_up, mla_w_out, ln_final):
    b, s, d = x.shape
    depth = ln_ffn1.shape[0]
    h = x.reshape(b * s, d)

    ffn_stacks = [((w_in, i), (w_out, i)) for i in range(depth)
                  for w_in, w_out in ((ffn1_w_in, ffn1_w_out), (ffn2_w_in, ffn2_w_out))]
    ffn_weights = {0: (ffn1_w_in[0].astype(BF16), ffn1_w_out[0].astype(BF16))}
    uncast = list(range(1, len(ffn_stacks)))

    def side_cast(call, *args, extra=(), **kwargs):
        target = uncast.pop(0) if uncast else None
        pairs = (ffn_stacks[target] if target is not None else ()) + tuple(extra)
        out, cast = call(*args, cast_next=pairs, **kwargs)
        if target is not None:
            ffn_weights[target], cast = cast[:2], cast[2:]
        return (out, cast) if extra else out

    for i in range(depth):
        j = i // 2
        if i % 2 == 0:
            h, sgu_w = side_cast(_ffn, h, ln_ffn1[i], *ffn_weights[2 * i],
                                 extra=((sgu_w_in, j), (sgu_w_out, j)))
            h = side_cast(_sgu, h, ln_mix[i], sgu_w[0], sgu_v_gain[j], sgu_v_bias[j],
                          sgu_w_spatial[j], sgu_b_spatial[j], sgu_w[1])
        else:
            h = side_cast(_ffn, h, ln_ffn1[i], *ffn_weights[2 * i])
            h = _mla(h, positions, ln_mix[i], mla_w_in[j], mla_q_norm[j], mla_w_q_up[j],
                     mla_kv_norm[j], mla_w_kv_up[j], mla_w_out[j])
        last = i == depth - 1
        h = side_cast(_ffn, h, ln_ffn2[i], *ffn_weights[2 * i + 1],
                      final_gain=ln_final if last else None)
    return h.reshape(b, s, d)
```

```python
import functools
import math

import jax
import jax.numpy as jnp
from jax import lax
from jax.experimental import pallas as pl
from jax.experimental.pallas import tpu as pltpu

F32 = jnp.float32
BF16 = jnp.bfloat16

EPS = 1e-6
CHUNK = 64
SGU_BLOCK = 128
SGU_GROUPS = 8
MLA_HEADS = 16
Q_LORA = 512
KV_LORA = 512
QK_NOPE = 128
QK_ROPE = 64
V_DIM = 128
QK_DIM = QK_NOPE + QK_ROPE
ROPE_THETA = 10000.0

LANES = 128
BF16_SUBLANES = 16
HEAD_PAD = 2 * LANES
VMEM_LIMIT = 61 * 1024 * 1024

FFN_TM = 1024
FFN_TF = 512
FFN_ROWS = 64
FFN_COLS = 512
SGU_IN_TM = 256
SGU_OUT_TM = 256
SGU_COLS = 1024
SGU_OUT_COLS = 512
MLA_P_TM = 256
MLA_O_TM = 512
ATT_T = 512
ATT_HEADS = 4
ATT_UNROLL = 4
ATT_SUM_ROWS = 16


def _rms_normalize(x, gain):
    return x * lax.rsqrt(jnp.mean(x * x, axis=-1, keepdims=True) + EPS) * gain


def _params(*sem):
    return pltpu.CompilerParams(dimension_semantics=sem, vmem_limit_bytes=VMEM_LIMIT)


def _dot_nt(a, b):
    return lax.dot_general(a, b, (((1,), (1,)), ((), ())), preferred_element_type=F32)


def _cast_plan(stacked, layer, grid):
    _, rows, cols = stacked.shape
    steps = math.prod(grid)

    def split(n):
        for nb in range(1, n + 1):
            na = n // nb
            if (na * nb == n and rows % na == 0 and cols % nb == 0
                    and (rows // na) % BF16_SUBLANES == 0 and (cols // nb) % LANES == 0):
                return na, nb
        return None

    used = next(n for n in range(steps, 0, -1) if split(n))
    na, nb = split(used)
    block = (rows // na, cols // nb)

    def index(*ids):
        flat = ids[0]
        for extent, idx in zip(grid[1:], ids[1:]):
            flat = flat * extent + idx
        if used < steps:
            flat = jnp.minimum(flat, used - 1)
        return flat // nb, flat % nb

    src = pl.BlockSpec((None,) + block, lambda *ids: (layer,) + index(*ids))
    dst = pl.BlockSpec(block, index)
    return src, dst, jax.ShapeDtypeStruct((rows, cols), BF16)


def _cast_blocks(src_refs, dst_refs):
    for src, dst in zip(src_refs, dst_refs):
        dst[...] = src[...].astype(BF16)


def _ffn_kernel(x_ref, g_ref, wg_ref, wu_ref, wo_ref, *rest, final_norm, n_cast):
    rest = list(rest)
    gf_ref = rest.pop(0) if final_norm else None
    cast_src = [rest.pop(0) for _ in range(n_cast)]
    o_ref = rest.pop(0)
    cast_dst = [rest.pop(0) for _ in range(n_cast)]
    (xn_ref,) = rest
    j = pl.program_id(1)
    _cast_blocks(cast_src, cast_dst)

    tm, d = x_ref.shape
    row_chunks = [slice(r, r + FFN_ROWS) for r in range(0, tm, FFN_ROWS)]

    @pl.when(j == 0)
    def _():
        for rows in row_chunks:
            xn_ref[rows, :] = _rms_normalize(x_ref[rows, :], g_ref[...]).astype(BF16)
        o_ref[...] = jnp.zeros(o_ref.shape, F32)

    xn = xn_ref[...]
    gate = jnp.dot(xn, wg_ref[...], preferred_element_type=F32)
    up = jnp.dot(xn, wu_ref[...], preferred_element_type=F32)
    act = (gate * jax.nn.sigmoid(gate) * up).astype(BF16)
    for c in range(0, d, FFN_COLS):
        o_ref[:, c:c + FFN_COLS] += jnp.dot(act, wo_ref[:, c:c + FFN_COLS],
                                            preferred_element_type=F32)

    @pl.when(j == pl.num_programs(1) - 1)
    def _():
        for rows in row_chunks:
            y = x_ref[rows, :] + 0.5 * o_ref[rows, :]
            o_ref[rows, :] = y
            if final_norm:
                rstd = lax.rsqrt(jnp.mean(y * y, axis=-1, keepdims=True) + EPS)
                o_ref[rows, :] = o_ref[rows, :] * rstd * gf_ref[...]


def _ffn(x, gain, w_in, w_out, final_gain=None, cast_next=()):
    m, d = x.shape
    f = w_out.shape[0]
    nf = f // FFN_TF
    grid = (m // FFN_TM, nf)
    final_norm = final_gain is not None
    in_specs = [
        pl.BlockSpec((FFN_TM, d), lambda i, j: (i, 0)),
        pl.BlockSpec((1, d), lambda i, j: (0, 0)),
        pl.BlockSpec((d, FFN_TF), lambda i, j: (0, j)),
        pl.BlockSpec((d, FFN_TF), lambda i, j: (0, j + nf)),
        pl.BlockSpec((FFN_TF, d), lambda i, j: (j, 0)),
    ]
    args = [x, gain.reshape(1, d), w_in, w_in, w_out]
    if final_norm:
        in_specs.append(pl.BlockSpec((1, d), lambda i, j: (0, 0)))
        args.append(final_gain.reshape(1, d))
    out_specs = [pl.BlockSpec((FFN_TM, d), lambda i, j: (i, 0))]
    out_shape = [jax.ShapeDtypeStruct((m, d), F32)]
    stacks = [stacked for stacked, _ in cast_next]
    for stacked, layer in cast_next:
        src, dst, shape = _cast_plan(stacked, layer, grid)
        in_specs.append(src)
        args.append(stacked)
        out_specs.append(dst)
        out_shape.append(shape)
    out = pl.pallas_call(
        functools.partial(_ffn_kernel, final_norm=final_norm, n_cast=len(stacks)),
        grid=grid,
        in_specs=in_specs,
        out_specs=out_specs,
        out_shape=out_shape,
        scratch_shapes=[pltpu.VMEM((FFN_TM, d), BF16)],
        compiler_params=_params("parallel", "arbitrary"),
        name="ffn_final" if final_norm else "ffn",
    )(*args)
    return out[0], tuple(out[1:])


def _resident(shape):
    return pl.BlockSpec(shape, lambda i: (0,) * len(shape), pipeline_mode=pl.Buffered(1))


def _sgu_in_kernel(x_ref, g_ref, w_ref, vg_ref, vb_ref, *rest, n_cast):
    cast_src, (u_ref, vn_ref, *rest) = rest[:n_cast], rest[n_cast:]
    cast_dst, (xn_ref, v_ref) = rest[:n_cast], rest[n_cast:]
    _cast_blocks(cast_src, cast_dst)
    tm, width = u_ref.shape
    chunks = [slice(c, c + SGU_COLS) for c in range(0, width, SGU_COLS)]
    tiles = [slice(c, c + LANES) for c in range(0, width, LANES)]

    def project(cols):
        return jnp.dot(xn_ref[...], w_ref[:, cols], preferred_element_type=F32)

    xn_ref[...] = _rms_normalize(x_ref[...], g_ref[...]).astype(BF16)

    def mean_pass():
        total = jnp.zeros((tm, LANES), F32)
        for t in tiles:
            total += v_ref[:, t]
        return jnp.broadcast_to(jnp.sum(total, axis=-1, keepdims=True) / width, (tm, LANES))

    def rstd_pass(mu):
        sq = jnp.zeros((tm, LANES), F32)
        for t in tiles:
            dv = v_ref[:, t] - mu
            sq += dv * dv
        var = jnp.sum(sq, axis=-1, keepdims=True) / width
        return jnp.broadcast_to(lax.rsqrt(var + EPS), (tm, LANES))

    def normalize(mu, rstd):
        for t in tiles:
            vn_ref[:, t] = ((v_ref[:, t] - mu) * rstd * vg_ref[:, t] + vb_ref[:, t]).astype(BF16)

    for cols in chunks:
        v_ref[:, cols] = jax.nn.gelu(project(slice(width + cols.start, width + cols.stop)))
    stats = {}
    side_work = [lambda: stats.update(mu=mean_pass()),
                 lambda: stats.update(rstd=rstd_pass(stats["mu"])),
                 lambda: normalize(stats["mu"], stats["rstd"])]
    assert len(chunks) >= len(side_work)
    for k, cols in enumerate(chunks):
        u = project(cols)
        if k < len(side_work):
            side_work[k]()
        u_ref[:, cols] = jax.nn.gelu(u).astype(BF16)


def _sgu_out_kernel(x_ref, u_ref, vn_ref, ws_ref, bs_ref, wo_ref, *rest, n_cast):
    cast_src, (o_ref, *rest) = rest[:n_cast], rest[n_cast:]
    cast_dst, (gated_ref,) = rest[:n_cast], rest[n_cast:]
    _cast_blocks(cast_src, cast_dst)
    tm, width = u_ref.shape
    d = o_ref.shape[1]
    gw = width // SGU_GROUPS
    row_chunk = lax.broadcasted_iota(jnp.int32, (SGU_BLOCK, SGU_BLOCK), 0) // CHUNK
    col_chunk = lax.broadcasted_iota(jnp.int32, (SGU_BLOCK, SGU_BLOCK), 1) // CHUNK
    visible = row_chunk >= col_chunk
    for g in range(SGU_GROUPS):
        lanes = slice(g * gw, (g + 1) * gw)
        ws = jnp.where(visible, ws_ref[g], 0.0).astype(BF16)
        bias = jnp.tile(bs_ref[g], (1, gw // LANES))
        for b in range(tm // SGU_BLOCK):
            rows = slice(b * SGU_BLOCK, (b + 1) * SGU_BLOCK)
            mixed = jnp.dot(ws, vn_ref[rows, lanes], preferred_element_type=F32) + bias
            gated_ref[rows, lanes] = (u_ref[rows, lanes].astype(F32) * mixed).astype(BF16)
    gated = gated_ref[...]
    for c in range(0, d, SGU_OUT_COLS):
        cols = slice(c, c + SGU_OUT_COLS)
        o_ref[:, cols] = x_ref[:, cols] + jnp.dot(gated, wo_ref[:, cols],
                                                  preferred_element_type=F32)


def _sgu(x, gain, w_in, v_gain, v_bias, w_spatial, b_spatial, w_out, cast_next=()):
    m, d = x.shape
    width = w_out.shape[0]
    rows = lambda tm: (lambda i: (i, 0))

    def with_cast(in_specs, args, out_specs, out_shape, pairs, n_tiles):
        for stacked, layer in pairs:
            src, dst, shape = _cast_plan(stacked, layer, (n_tiles,))
            in_specs.append(src)
            args.append(stacked)
            out_specs.append(dst)
            out_shape.append(shape)
        return len(pairs)

    tm = SGU_IN_TM
    in_specs = [pl.BlockSpec((tm, d), rows(tm)), _resident((1, d)), _resident(w_in.shape),
                _resident((1, width)), _resident((1, width))]
    args = [x, gain.reshape(1, d), w_in, v_gain.reshape(1, width), v_bias.reshape(1, width)]
    out_specs = [pl.BlockSpec((tm, width), rows(tm)), pl.BlockSpec((tm, width), rows(tm))]
    out_shape = [jax.ShapeDtypeStruct((m, width), BF16), jax.ShapeDtypeStruct((m, width), BF16)]
    n_cast = with_cast(in_specs, args, out_specs, out_shape, cast_next[:1], m // tm)
    u, vn, *cast_in = pl.pallas_call(
        functools.partial(_sgu_in_kernel, n_cast=n_cast),
        grid=(m // tm,),
        in_specs=in_specs,
        out_specs=out_specs,
        out_shape=out_shape,
        scratch_shapes=[pltpu.VMEM((tm, d), BF16), pltpu.VMEM((tm, width), F32)],
        compiler_params=_params("parallel"),
        name="sgu_in",
    )(*args)

    tm = SGU_OUT_TM
    bias = jnp.broadcast_to(b_spatial[:, :, None], (SGU_GROUPS, SGU_BLOCK, LANES))
    in_specs = [pl.BlockSpec((tm, d), rows(tm)), pl.BlockSpec((tm, width), rows(tm)),
                pl.BlockSpec((tm, width), rows(tm)), _resident(w_spatial.shape),
                _resident(bias.shape), _resident(w_out.shape)]
    args = [x, u, vn, w_spatial, bias, w_out]
    out_specs = [pl.BlockSpec((tm, d), rows(tm))]
    out_shape = [jax.ShapeDtypeStruct((m, d), F32)]
    n_cast = with_cast(in_specs, args, out_specs, out_shape, cast_next[1:], m // tm)
    y, *cast_out = pl.pallas_call(
        functools.partial(_sgu_out_kernel, n_cast=n_cast),
        grid=(m // tm,),
        in_specs=in_specs,
        out_specs=out_specs,
        out_shape=out_shape,
        scratch_shapes=[pltpu.VMEM((tm, width), BF16)],
        compiler_params=_params("parallel"),
        name="sgu_out",
    )(*args)
    return y, tuple(cast_in + cast_out)


def _mla_proj_kernel(x_ref, posr_ref, freqc_ref, g_ref, wi_ref, gq_ref,
                     wqt_ref, gkv_ref, wkn_ref, wvt_ref, qt_ref, kn_ref, kr_ref, vt_ref):
    half = QK_ROPE // 2
    hn = _rms_normalize(x_ref[...], g_ref[...]).astype(BF16)
    proj = jnp.dot(hn, wi_ref[...], preferred_element_type=F32)
    qn = _rms_normalize(proj[:, :Q_LORA], gq_ref[...]).astype(BF16)
    kvn = _rms_normalize(proj[:, Q_LORA:Q_LORA + KV_LORA], gkv_ref[...]).astype(BF16)
    kr = proj[:, Q_LORA + KV_LORA:]

    ang_t = freqc_ref[...] * posr_ref[...].astype(F32)
    cos_t = jnp.cos(ang_t)
    sin_t = jnp.sin(ang_t)

    kr_t = kr.T
    k1, k2 = kr_t[:half], kr_t[half:QK_ROPE]
    kr_t = jnp.concatenate([k1 * cos_t - k2 * sin_t, k1 * sin_t + k2 * cos_t, kr_t[QK_ROPE:]],
                           axis=0)
    kr_ref[...] = kr_t.T.astype(BF16)

    kn_ref[...] = jnp.dot(kvn, wkn_ref[...], preferred_element_type=F32).astype(BF16)
    vt_ref[...] = _dot_nt(wvt_ref[...], kvn).astype(BF16)

    scale = QK_DIM ** -0.5 * math.log2(math.e)
    q_t = _dot_nt(wqt_ref[...], qn) * scale
    for h in range(MLA_HEADS):
        src = h * QK_DIM
        dst = h * HEAD_PAD
        x1 = q_t[src + QK_NOPE:src + QK_NOPE + half]
        x2 = q_t[src + QK_NOPE + half:src + QK_DIM]
        qt_ref[dst:dst + QK_NOPE, :] = q_t[src:src + QK_NOPE].astype(BF16)
        qt_ref[dst + QK_NOPE:dst + QK_NOPE + half, :] = (x1 * cos_t - x2 * sin_t).astype(BF16)
        qt_ref[dst + QK_NOPE + half:dst + QK_DIM, :] = (x1 * sin_t + x2 * cos_t).astype(BF16)
        qt_ref[dst + QK_DIM:dst + HEAD_PAD, :] = jnp.zeros((HEAD_PAD - QK_DIM, q_t.shape[1]), BF16)


def _mla_proj(x, positions, gain, w_in, gq, wqt, gkv, wkn, wvt):
    m, d = x.shape
    tm = MLA_P_TM
    r = ATT_T // tm
    nt = m // ATT_T
    half = QK_ROPE // 2
    inv_freq = 1.0 / (ROPE_THETA ** (jnp.arange(half, dtype=F32) / half))
    const = lambda i: (0, 0)

    def resident(shape):
        return pl.BlockSpec(shape, const, pipeline_mode=pl.Buffered(1))

    return pl.pallas_call(
        _mla_proj_kernel,
        grid=(m // tm,),
        in_specs=[
            pl.BlockSpec((tm, d), lambda i: (i, 0)),
            pl.BlockSpec((None, 1, tm), lambda i: (i, 0, 0)),
            resident((half, 1)),
            resident((1, d)),
            resident(w_in.shape),
            resident((1, Q_LORA)),
            resident(wqt.shape),
            resident((1, KV_LORA)),
            resident(wkn.shape),
            resident(wvt.shape),
        ],
        out_specs=[
            pl.BlockSpec((None, MLA_HEADS * HEAD_PAD, tm), lambda i: (i // r, 0, i % r)),
            pl.BlockSpec((tm, MLA_HEADS * QK_NOPE), lambda i: (i, 0)),
            pl.BlockSpec((tm, LANES), lambda i: (i, 0)),
            pl.BlockSpec((None, MLA_HEADS * V_DIM, tm), lambda i: (i // r, 0, i % r)),
        ],
        out_shape=[
            jax.ShapeDtypeStruct((nt, MLA_HEADS * HEAD_PAD, ATT_T), BF16),
            jax.ShapeDtypeStruct((m, MLA_HEADS * QK_NOPE), BF16),
            jax.ShapeDtypeStruct((m, LANES), BF16),
            jax.ShapeDtypeStruct((nt, MLA_HEADS * V_DIM, ATT_T), BF16),
        ],
        compiler_params=_params("parallel"),
        name="mla_proj",
    )(x, positions.reshape(m // tm, 1, tm), inv_freq.reshape(half, 1), gain.reshape(1, d),
      w_in, gq.reshape(1, -1), wqt, gkv.reshape(1, -1), wkn, wvt)


def _mla_attn_kernel(qt_ref, kn_ref, kr_ref, vt_ref, o_ref,
                     m_ref, acc_ref, s_ref, p_ref, alpha_ref):
    i = pl.program_id(2)
    t = qt_ref.shape[1]
    last = ATT_HEADS - 1

    def visible_only(s):
        key_chunk = lax.broadcasted_iota(jnp.int32, (t, 1), 0) // CHUNK
        qry_chunk = lax.broadcasted_iota(jnp.int32, (1, t), 1) // CHUNK
        return jnp.where(key_chunk <= qry_chunk, s, -jnp.inf)

    def scores(j, a):
        rows = pl.ds(pl.multiple_of(j * t, t), t)
        k = jnp.concatenate([kn_ref[rows, a * QK_NOPE:(a + 1) * QK_NOPE], kr_ref[rows, :]], axis=1)
        return jnp.dot(k, qt_ref[a * HEAD_PAD:(a + 1) * HEAD_PAD, :],
                       preferred_element_type=F32)

    ones_rows = jnp.ones((ATT_SUM_ROWS, t), BF16)

    def weighted_values(j, a, p, alpha):
        v_ext = jnp.concatenate([vt_ref[j, a * V_DIM:(a + 1) * V_DIM, :], ones_rows], axis=0)
        pv = jnp.dot(v_ext, p, preferred_element_type=F32)
        acc_ref[a] = alpha * acc_ref[a] + pv

    def softmax(a, s):
        m_prev = m_ref[a]
        m_new = jnp.maximum(m_prev, jnp.max(s, axis=0, keepdims=True))
        alpha = jnp.exp2(m_prev - m_new)
        p = jnp.exp2(s - m_new)
        m_ref[a] = m_new
        return p.astype(BF16), alpha

    def key_tile(j, diagonal, carry=None):
        s_cur, p_prev, alpha_prev = carry or (s_ref[...], p_ref[...], alpha_ref[...])
        for a in range(ATT_HEADS):
            if a < last:
                s_nxt = scores(j, a + 1)
                if diagonal:
                    s_nxt = visible_only(s_nxt)
            elif not diagonal:
                s_nxt = scores(j + 1, 0)
            weighted_values(jnp.maximum(j - 1, 0) if a == 0 else j, (a - 1) % ATT_HEADS,
                            p_prev, alpha_prev)
            p_prev, alpha_prev = softmax(a, s_cur)
            s_cur = s_nxt
        return s_nxt, p_prev, alpha_prev

    m_ref[...] = jnp.full(m_ref.shape, -jnp.inf, F32)
    acc_ref[...] = jnp.zeros(acc_ref.shape, F32)
    p_ref[...] = jnp.zeros(p_ref.shape, BF16)
    alpha_ref[...] = jnp.ones(alpha_ref.shape, F32)
    s_ref[...] = scores(0, 0)

    def save(carry):
        s_ref[...], p_ref[...], alpha_ref[...] = carry

    def run(first, count):
        carry = None
        for k in range(count):
            carry = key_tile(first + k, False, carry)
        save(carry)

    def unrolled(jj, carry):
        run(ATT_UNROLL * jj, ATT_UNROLL)
        return carry

    lax.fori_loop(0, i // ATT_UNROLL, unrolled, 0)
    count = ATT_UNROLL // 2
    while count:
        @pl.when(i % (2 * count) >= count)
        def _(count=count):
            run(i - i % (2 * count), count)
        count //= 2

    s_ref[...] = visible_only(s_ref[...])
    _, p, alpha = key_tile(i, diagonal=True)
    weighted_values(i, last, p, alpha)

    for a in range(ATT_HEADS):
        o = acc_ref[a, :V_DIM, :] / acc_ref[a, V_DIM:V_DIM + 1, :]
        o_ref[:, a * V_DIM:(a + 1) * V_DIM] = o.T.astype(o_ref.dtype)


def _mla_attn(qt, kn, kr, vt, batch):
    nt, _, t = qt.shape
    nq = nt // batch
    s = nq * t
    ha = ATT_HEADS
    return pl.pallas_call(
        _mla_attn_kernel,
        grid=(batch, MLA_HEADS // ha, nq),
        in_specs=[
            pl.BlockSpec((None, ha * HEAD_PAD, t), lambda b, h, i: (b * nq + i, h, 0)),
            pl.BlockSpec((None, s, ha * QK_NOPE), lambda b, h, i: (b, 0, h)),
            pl.BlockSpec((None, s, LANES), lambda b, h, i: (b, 0, 0)),
            pl.BlockSpec((None, nq, ha * V_DIM, t), lambda b, h, i: (b, 0, h, 0)),
        ],
        out_specs=pl.BlockSpec((None, t, ha * V_DIM), lambda b, h, i: (b, i, h)),
        out_shape=jax.ShapeDtypeStruct((batch, s, MLA_HEADS * V_DIM), BF16),
        scratch_shapes=[
            pltpu.VMEM((ha, 1, t), F32),
            pltpu.VMEM((ha, V_DIM + ATT_SUM_ROWS, t), F32),
            pltpu.VMEM((t, t), F32),
            pltpu.VMEM((t, t), BF16),
            pltpu.VMEM((1, t), F32),
        ],
        compiler_params=_params("parallel", "parallel", "arbitrary"),
        name="mla_attn",
    )(qt, kn.reshape(batch, s, -1), kr.reshape(batch, s, LANES),
      vt.reshape(batch, nq, MLA_HEADS * V_DIM, t))


def _mla_out_kernel(x_ref, o_ref, w_ref, y_ref):
    y_ref[...] = x_ref[...] + jnp.dot(o_ref[...], w_ref[...], preferred_element_type=F32)


def _mla_out(x, o, w):
    m, d = x.shape
    tm = MLA_O_TM
    return pl.pallas_call(
        _mla_out_kernel,
        grid=(m // tm,),
        in_specs=[
            pl.BlockSpec((tm, d), lambda i: (i, 0)),
            pl.BlockSpec((tm, o.shape[1]), lambda i: (i, 0)),
            pl.BlockSpec(w.shape, lambda i: (0, 0)),
        ],
        out_specs=pl.BlockSpec((tm, d), lambda i: (i, 0)),
        out_shape=jax.ShapeDtypeStruct((m, d), F32),
        compiler_params=_params("parallel"),
        name="mla_out",
    )(x, o, w)


def _mla(x, positions, gain, w_in, gq, w_q_up, gkv, w_kv_up, w_out):
    b, _ = positions.shape
    m, d = x.shape
    w_in_p = jnp.pad(w_in, ((0, 0), (0, LANES - QK_ROPE))).astype(BF16)
    wqt = w_q_up.T.astype(BF16)
    w_kv = w_kv_up.reshape(KV_LORA, MLA_HEADS, QK_NOPE + V_DIM)
    wkn = w_kv[:, :, :QK_NOPE].reshape(KV_LORA, -1).astype(BF16)
    wvt = w_kv[:, :, QK_NOPE:].reshape(KV_LORA, -1).T.astype(BF16)
    qt, kn, kr, vt = _mla_proj(x, positions, gain, w_in_p, gq, wqt, gkv, wkn, wvt)
    o = _mla_attn(qt, kn, kr, vt, b)
    return _mla_out(x, o.reshape(m, -1), w_out.astype(BF16))


def kernel(x, positions, ln_ffn1, ffn1_w_in, ffn1_w_out, ln_mix, ln_ffn2, ffn2_w_in, ffn2_w_out,
           sgu_w_in, sgu_v_gain, sgu_v_bias, sgu_w_spatial, sgu_b_spatial, sgu_w_out,
           mla_w_in, mla_q_norm, mla_w_q_up, mla_kv_norm, mla_w_kv_up, mla_w_out, ln_final):
    b, s, d = x.shape
    depth = ln_ffn1.shape[0]
    h = x.reshape(b * s, d)

    ffn_stacks = [((w_in, i), (w_out, i)) for i in range(depth)
                  for w_in, w_out in ((ffn1_w_in, ffn1_w_out), (ffn2_w_in, ffn2_w_out))]
    ffn_weights = {0: (ffn1_w_in[0].astype(BF16), ffn1_w_out[0].astype(BF16))}
    uncast = list(range(1, len(ffn_stacks)))

    def side_cast(call, *args, extra=(), **kwargs):
        target = uncast.pop(0) if uncast else None
        pairs = (ffn_stacks[target] if target is not None else ()) + tuple(extra)
        out, cast = call(*args, cast_next=pairs, **kwargs)
        if target is not None:
            ffn_weights[target], cast = cast[:2], cast[2:]
        return (out, cast) if extra else out

    for i in range(depth):
        j = i // 2
        if i % 2 == 0:
            h, sgu_w = side_cast(_ffn, h, ln_ffn1[i], *ffn_weights[2 * i],
                                 extra=((sgu_w_in, j), (sgu_w_out, j)))
            h = side_cast(_sgu, h, ln_mix[i], sgu_w[0], sgu_v_gain[j], sgu_v_bias[j],
                          sgu_w_spatial[j], sgu_b_spatial[j], sgu_w[1])
        else:
            h = side_cast(_ffn, h, ln_ffn1[i], *ffn_weights[2 * i])
            h = _mla(h, positions, ln_mix[i], mla_w_in[j], mla_q_norm[j], mla_w_q_up[j],
                     mla_kv_norm[j], mla_w_kv_up[j], mla_w_out[j])
        last = i == depth - 1
        h = side_cast(_ffn, h, ln_ffn2[i], *ffn_weights[2 * i + 1],
                      final_gain=ln_final if last else None)
    return h.reshape(b, s, d)
```

```python
import functools
import math

import jax
import jax.numpy as jnp
from jax import lax
from jax.experimental import pallas as pl
from jax.experimental.pallas import tpu as pltpu

F32 = jnp.float32
BF16 = jnp.bfloat16

EPS = 1e-6
CHUNK = 64
SGU_BLOCK = 128
SGU_GROUPS = 8
MLA_HEADS = 16
Q_LORA = 512
KV_LORA = 512
QK_NOPE = 128
QK_ROPE = 64
V_DIM = 128
QK_DIM = QK_NOPE + QK_ROPE
ROPE_THETA = 10000.0

LANES = 128
BF16_SUBLANES = 16
HEAD_PAD = 2 * LANES
VMEM_LIMIT = 61 * 1024 * 1024

FFN_TM = 1024
FFN_TF = 512
FFN_ROWS = 64
FFN_COLS = 512
SGU_IN_TM = 256
SGU_OUT_TM = 256
SGU_COLS = 1024
SGU_OUT_COLS = 512
MLA_P_TM = 256
MLA_O_TM = 512
ATT_T = 512
ATT_HEADS = 4
ATT_UNROLL = 4
ATT_SUM_ROWS = 16


def _rms_normalize(x, gain):
    return x * lax.rsqrt(jnp.mean(x * x, axis=-1, keepdims=True) + EPS) * gain


def _params(*sem):
    return pltpu.CompilerParams(dimension_semantics=sem, vmem_limit_bytes=VMEM_LIMIT)


def _dot_nt(a, b):
    return lax.dot_general(a, b, (((1,), (1,)), ((), ())), preferred_element_type=F32)


def _cast_plan(stacked, layer, grid):
    _, rows, cols = stacked.shape
    steps = math.prod(grid)

    def split(n):
        for nb in range(1, n + 1):
            na = n // nb
            if (na * nb == n and rows % na == 0 and cols % nb == 0
                    and (rows // na) % BF16_SUBLANES == 0 and (cols // nb) % LANES == 0):
                return na, nb
        return None

    used = next(n for n in range(steps, 0, -1) if split(n))
    na, nb = split(used)
    block = (rows // na, cols // nb)

    def index(*ids):
        flat = ids[0]
        for extent, idx in zip(grid[1:], ids[1:]):
            flat = flat * extent + idx
        if used < steps:
            flat = jnp.minimum(flat, used - 1)
        return flat // nb, flat % nb

    src = pl.BlockSpec((None,) + block, lambda *ids: (layer,) + index(*ids))
    dst = pl.BlockSpec(block, index)
    return src, dst, jax.ShapeDtypeStruct((rows, cols), BF16)


def _cast_blocks(src_refs, dst_refs):
    for src, dst in zip(src_refs, dst_refs):
        dst[...] = src[...].astype(BF16)


def _ffn_kernel(x_ref, g_ref, wg_ref, wu_ref, wo_ref, *rest, final_norm, n_cast):
    rest = list(rest)
    gf_ref = rest.pop(0) if final_norm else None
    cast_src = [rest.pop(0) for _ in range(n_cast)]
    o_ref = rest.pop(0)
    cast_dst = [rest.pop(0) for _ in range(n_cast)]
    (xn_ref,) = rest
    j = pl.program_id(1)
    _cast_blocks(cast_src, cast_dst)

    tm, d = x_ref.shape
    row_chunks = [slice(r, r + FFN_ROWS) for r in range(0, tm, FFN_ROWS)]

    @pl.when(j == 0)
    def _():
        for rows in row_chunks:
            x = x_ref[rows, :]
            xn_ref[rows, :] = _rms_normalize(x, g_ref[...]).astype(BF16)
            o_ref[rows, :] = x

    xn = xn_ref[...]
    gate = jnp.dot(xn, wg_ref[...], preferred_element_type=F32)
    up = jnp.dot(xn, wu_ref[...], preferred_element_type=F32)
    act = (0.5 * gate * jax.nn.sigmoid(gate) * up).astype(BF16)
    for c in range(0, d, FFN_COLS):
        o_ref[:, c:c + FFN_COLS] += jnp.dot(act, wo_ref[:, c:c + FFN_COLS],
                                            preferred_element_type=F32)

    if final_norm:
        @pl.when(j == pl.num_programs(1) - 1)
        def _():
            for rows in row_chunks:
                y = o_ref[rows, :]
                rstd = lax.rsqrt(jnp.mean(y * y, axis=-1, keepdims=True) + EPS)
                o_ref[rows, :] = y * rstd * gf_ref[...]


def _ffn(x, gain, w_in, w_out, final_gain=None, cast_next=()):
    m, d = x.shape
    f = w_out.shape[0]
    nf = f // FFN_TF
    grid = (m // FFN_TM, nf)
    final_norm = final_gain is not None
    in_specs = [
        pl.BlockSpec((FFN_TM, d), lambda i, j: (i, 0)),
        pl.BlockSpec((1, d), lambda i, j: (0, 0)),
        pl.BlockSpec((d, FFN_TF), lambda i, j: (0, j)),
        pl.BlockSpec((d, FFN_TF), lambda i, j: (0, j + nf)),
        pl.BlockSpec((FFN_TF, d), lambda i, j: (j, 0)),
    ]
    args = [x, gain.reshape(1, d), w_in, w_in, w_out]
    if final_norm:
        in_specs.append(pl.BlockSpec((1, d), lambda i, j: (0, 0)))
        args.append(final_gain.reshape(1, d))
    out_specs = [pl.BlockSpec((FFN_TM, d), lambda i, j: (i, 0))]
    out_shape = [jax.ShapeDtypeStruct((m, d), F32)]
    stacks = [stacked for stacked, _ in cast_next]
    for stacked, layer in cast_next:
        src, dst, shape = _cast_plan(stacked, layer, grid)
        in_specs.append(src)
        args.append(stacked)
        out_specs.append(dst)
        out_shape.append(shape)
    out = pl.pallas_call(
        functools.partial(_ffn_kernel, final_norm=final_norm, n_cast=len(stacks)),
        grid=grid,
        in_specs=in_specs,
        out_specs=out_specs,
        out_shape=out_shape,
        scratch_shapes=[pltpu.VMEM((FFN_TM, d), BF16)],
        compiler_params=_params("parallel", "arbitrary"),
        name="ffn_final" if final_norm else "ffn",
    )(*args)
    return out[0], tuple(out[1:])


def _resident(shape):
    return pl.BlockSpec(shape, lambda i: (0,) * len(shape), pipeline_mode=pl.Buffered(1))


def _sgu_in_kernel(x_ref, g_ref, w_ref, vg_ref, vb_ref, *rest, n_cast):
    cast_src, (u_ref, vn_ref, *rest) = rest[:n_cast], rest[n_cast:]
    cast_dst, (xn_ref, v_ref) = rest[:n_cast], rest[n_cast:]
    _cast_blocks(cast_src, cast_dst)
    tm, width = u_ref.shape
    chunks = [slice(c, c + SGU_COLS) for c in range(0, width, SGU_COLS)]
    tiles = [slice(c, c + LANES) for c in range(0, width, LANES)]

    def project(cols):
        return jnp.dot(xn_ref[...], w_ref[:, cols], preferred_element_type=F32)

    xn_ref[...] = _rms_normalize(x_ref[...], g_ref[...]).astype(BF16)

    def mean_pass():
        total = jnp.zeros((tm, LANES), F32)
        for t in tiles:
            total += v_ref[:, t]
        return jnp.broadcast_to(jnp.sum(total, axis=-1, keepdims=True) / width, (tm, LANES))

    def rstd_pass(mu):
        sq = jnp.zeros((tm, LANES), F32)
        for t in tiles:
            dv = v_ref[:, t] - mu
            sq += dv * dv
        var = jnp.sum(sq, axis=-1, keepdims=True) / width
        return jnp.broadcast_to(lax.rsqrt(var + EPS), (tm, LANES))

    def normalize(mu, rstd):
        for t in tiles:
            vn_ref[:, t] = ((v_ref[:, t] - mu) * rstd * vg_ref[:, t] + vb_ref[:, t]).astype(BF16)

    for cols in chunks:
        v_ref[:, cols] = jax.nn.gelu(project(slice(width + cols.start, width + cols.stop)))
    stats = {}
    side_work = [lambda: stats.update(mu=mean_pass()),
                 lambda: stats.update(rstd=rstd_pass(stats["mu"])),
                 lambda: normalize(stats["mu"], stats["rstd"])]
    assert len(chunks) >= len(side_work)
    for k, cols in enumerate(chunks):
        u = project(cols)
        if k < len(side_work):
            side_work[k]()
        u_ref[:, cols] = jax.nn.gelu(u).astype(BF16)


def _sgu_out_kernel(x_ref, u_ref, vn_ref, ws_ref, bs_ref, wo_ref, *rest, n_cast):
    cast_src, (o_ref, *rest) = rest[:n_cast], rest[n_cast:]
    cast_dst, (gated_ref,) = rest[:n_cast], rest[n_cast:]
    _cast_blocks(cast_src, cast_dst)
    tm, width = u_ref.shape
    d = o_ref.shape[1]
    gw = width // SGU_GROUPS
    row_chunk = lax.broadcasted_iota(jnp.int32, (SGU_BLOCK, SGU_BLOCK), 0) // CHUNK
    col_chunk = lax.broadcasted_iota(jnp.int32, (SGU_BLOCK, SGU_BLOCK), 1) // CHUNK
    visible = row_chunk >= col_chunk
    for g in range(SGU_GROUPS):
        lanes = slice(g * gw, (g + 1) * gw)
        ws = jnp.where(visible, ws_ref[g], 0.0).astype(BF16)
        bias = jnp.tile(bs_ref[g], (1, gw // LANES))
        for b in range(tm // SGU_BLOCK):
            rows = slice(b * SGU_BLOCK, (b + 1) * SGU_BLOCK)
            mixed = jnp.dot(ws, vn_ref[rows, lanes], preferred_element_type=F32) + bias
            gated_ref[rows, lanes] = (u_ref[rows, lanes].astype(F32) * mixed).astype(BF16)
    gated = gated_ref[...]
    for c in range(0, d, SGU_OUT_COLS):
        cols = slice(c, c + SGU_OUT_COLS)
        o_ref[:, cols] = x_ref[:, cols] + jnp.dot(gated, wo_ref[:, cols],
                                                  preferred_element_type=F32)


def _sgu(x, gain, w_in, v_gain, v_bias, w_spatial, b_spatial, w_out, cast_next=()):
    m, d = x.shape
    width = w_out.shape[0]
    rows = lambda tm: (lambda i: (i, 0))

    def with_cast(in_specs, args, out_specs, out_shape, pairs, n_tiles):
        for stacked, layer in pairs:
            src, dst, shape = _cast_plan(stacked, layer, (n_tiles,))
            in_specs.append(src)
            args.append(stacked)
            out_specs.append(dst)
            out_shape.append(shape)
        return len(pairs)

    tm = SGU_IN_TM
    in_specs = [pl.BlockSpec((tm, d), rows(tm)), _resident((1, d)), _resident(w_in.shape),
                _resident((1, width)), _resident((1, width))]
    args = [x, gain.reshape(1, d), w_in, v_gain.reshape(1, width), v_bias.reshape(1, width)]
    out_specs = [pl.BlockSpec((tm, width), rows(tm)), pl.BlockSpec((tm, width), rows(tm))]
    out_shape = [jax.ShapeDtypeStruct((m, width), BF16), jax.ShapeDtypeStruct((m, width), BF16)]
    n_cast = with_cast(in_specs, args, out_specs, out_shape, cast_next[:1], m // tm)
    u, vn, *cast_in = pl.pallas_call(
        functools.partial(_sgu_in_kernel, n_cast=n_cast),
        grid=(m // tm,),
        in_specs=in_specs,
        out_specs=out_specs,
        out_shape=out_shape,
        scratch_shapes=[pltpu.VMEM((tm, d), BF16), pltpu.VMEM((tm, width), F32)],
        compiler_params=_params("parallel"),
        name="sgu_in",
    )(*args)

    tm = SGU_OUT_TM
    bias = jnp.broadcast_to(b_spatial[:, :, None], (SGU_GROUPS, SGU_BLOCK, LANES))
    in_specs = [pl.BlockSpec((tm, d), rows(tm)), pl.BlockSpec((tm, width), rows(tm)),
                pl.BlockSpec((tm, width), rows(tm)), _resident(w_spatial.shape),
                _resident(bias.shape), _resident(w_out.shape)]
    args = [x, u, vn, w_spatial, bias, w_out]
    out_specs = [pl.BlockSpec((tm, d), rows(tm))]
    out_shape = [jax.ShapeDtypeStruct((m, d), F32)]
    n_cast = with_cast(in_specs, args, out_specs, out_shape, cast_next[1:], m // tm)
    y, *cast_out = pl.pallas_call(
        functools.partial(_sgu_out_kernel, n_cast=n_cast),
        grid=(m // tm,),
        in_specs=in_specs,
        out_specs=out_specs,
        out_shape=out_shape,
        scratch_shapes=[pltpu.VMEM((tm, width), BF16)],
        compiler_params=_params("parallel"),
        name="sgu_out",
    )(*args)
    return y, tuple(cast_in + cast_out)


def _mla_proj_kernel(x_ref, posr_ref, freqc_ref, g_ref, wi_ref, gq_ref,
                     wqt_ref, gkv_ref, wkn_ref, wvt_ref, qt_ref, kn_ref, kr_ref, vt_ref):
    half = QK_ROPE // 2
    hn = _rms_normalize(x_ref[...], g_ref[...]).astype(BF16)
    proj = jnp.dot(hn, wi_ref[...], preferred_element_type=F32)
    qn = _rms_normalize(proj[:, :Q_LORA], gq_ref[...]).astype(BF16)
    kvn = _rms_normalize(proj[:, Q_LORA:Q_LORA + KV_LORA], gkv_ref[...]).astype(BF16)
    kr = proj[:, Q_LORA + KV_LORA:]

    ang_t = freqc_ref[...] * posr_ref[...].astype(F32)
    cos_t = jnp.cos(ang_t)
    sin_t = jnp.sin(ang_t)

    kr_t = kr.T
    k1, k2 = kr_t[:half], kr_t[half:QK_ROPE]
    kr_t = jnp.concatenate([k1 * cos_t - k2 * sin_t, k1 * sin_t + k2 * cos_t, kr_t[QK_ROPE:]],
                           axis=0)
    kr_ref[...] = kr_t.T.astype(BF16)

    kn_ref[...] = jnp.dot(kvn, wkn_ref[...], preferred_element_type=F32).astype(BF16)
    vt_ref[...] = _dot_nt(wvt_ref[...], kvn).astype(BF16)

    scale = QK_DIM ** -0.5 * math.log2(math.e)
    q_t = _dot_nt(wqt_ref[...], qn) * scale
    for h in range(MLA_HEADS):
        src = h * QK_DIM
        dst = h * HEAD_PAD
        x1 = q_t[src + QK_NOPE:src + QK_NOPE + half]
        x2 = q_t[src + QK_NOPE + half:src + QK_DIM]
        qt_ref[dst:dst + QK_NOPE, :] = q_t[src:src + QK_NOPE].astype(BF16)
        qt_ref[dst + QK_NOPE:dst + QK_NOPE + half, :] = (x1 * cos_t - x2 * sin_t).astype(BF16)
        qt_ref[dst + QK_NOPE + half:dst + QK_DIM, :] = (x1 * sin_t + x2 * cos_t).astype(BF16)
        qt_ref[dst + QK_DIM:dst + HEAD_PAD, :] = jnp.zeros((HEAD_PAD - QK_DIM, q_t.shape[1]), BF16)


def _mla_proj(x, positions, gain, w_in, gq, wqt, gkv, wkn, wvt):
    m, d = x.shape
    tm = MLA_P_TM
    r = ATT_T // tm
    nt = m // ATT_T
    half = QK_ROPE // 2
    inv_freq = 1.0 / (ROPE_THETA ** (jnp.arange(half, dtype=F32) / half))
    const = lambda i: (0, 0)

    def resident(shape):
        return pl.BlockSpec(shape, const, pipeline_mode=pl.Buffered(1))

    return pl.pallas_call(
        _mla_proj_kernel,
        grid=(m // tm,),
        in_specs=[
            pl.BlockSpec((tm, d), lambda i: (i, 0)),
            pl.BlockSpec((None, 1, tm), lambda i: (i, 0, 0)),
            resident((half, 1)),
            resident((1, d)),
            resident(w_in.shape),
            resident((1, Q_LORA)),
            resident(wqt.shape),
            resident((1, KV_LORA)),
            resident(wkn.shape),
            resident(wvt.shape),
        ],
        out_specs=[
            pl.BlockSpec((None, MLA_HEADS * HEAD_PAD, tm), lambda i: (i // r, 0, i % r)),
            pl.BlockSpec((tm, MLA_HEADS * QK_NOPE), lambda i: (i, 0)),
            pl.BlockSpec((tm, LANES), lambda i: (i, 0)),
            pl.BlockSpec((None, MLA_HEADS * V_DIM, tm), lambda i: (i // r, 0, i % r)),
        ],
        out_shape=[
            jax.ShapeDtypeStruct((nt, MLA_HEADS * HEAD_PAD, ATT_T), BF16),
            jax.ShapeDtypeStruct((m, MLA_HEADS * QK_NOPE), BF16),
            jax.ShapeDtypeStruct((m, LANES), BF16),
            jax.ShapeDtypeStruct((nt, MLA_HEADS * V_DIM, ATT_T), BF16),
        ],
        compiler_params=_params("parallel"),
        name="mla_proj",
    )(x, positions.reshape(m // tm, 1, tm), inv_freq.reshape(half, 1), gain.reshape(1, d),
      w_in, gq.reshape(1, -1), wqt, gkv.reshape(1, -1), wkn, wvt)


def _mla_attn_kernel(qt_ref, kn_ref, kr_ref, vt_ref, o_ref,
                     m_ref, acc_ref, s_ref, p_ref, alpha_ref):
    i = pl.program_id(2)
    t = qt_ref.shape[1]
    last = ATT_HEADS - 1

    def visible_only(s):
        key_chunk = lax.broadcasted_iota(jnp.int32, (t, 1), 0) // CHUNK
        qry_chunk = lax.broadcasted_iota(jnp.int32, (1, t), 1) // CHUNK
        return jnp.where(key_chunk <= qry_chunk, s, -jnp.inf)

    def scores(j, a):
        rows = pl.ds(pl.multiple_of(j * t, t), t)
        k = jnp.concatenate([kn_ref[rows, a * QK_NOPE:(a + 1) * QK_NOPE], kr_ref[rows, :]], axis=1)
        return jnp.dot(k, qt_ref[a * HEAD_PAD:(a + 1) * HEAD_PAD, :],
                       preferred_element_type=F32)

    ones_rows = jnp.ones((ATT_SUM_ROWS, t), BF16)

    def weighted_values(j, a, p, alpha):
        v_ext = jnp.concatenate([vt_ref[j, a * V_DIM:(a + 1) * V_DIM, :], ones_rows], axis=0)
        pv = jnp.dot(v_ext, p, preferred_element_type=F32)
        acc_ref[a] = alpha * acc_ref[a] + pv

    def softmax(a, s):
        m_prev = m_ref[a]
        m_new = jnp.maximum(m_prev, jnp.max(s, axis=0, keepdims=True))
        alpha = jnp.exp2(m_prev - m_new)
        p = jnp.exp2(s - m_new)
        m_ref[a] = m_new
        return p.astype(BF16), alpha

    def key_tile(j, diagonal, carry=None):
        s_cur, p_prev, alpha_prev = carry or (s_ref[...], p_ref[...], alpha_ref[...])
        for a in range(ATT_HEADS):
            if a < last:
                s_nxt = scores(j, a + 1)
                if diagonal:
                    s_nxt = visible_only(s_nxt)
            elif not diagonal:
                s_nxt = scores(j + 1, 0)
            weighted_values(jnp.maximum(j - 1, 0) if a == 0 else j, (a - 1) % ATT_HEADS,
                            p_prev, alpha_prev)
            p_prev, alpha_prev = softmax(a, s_cur)
            s_cur = s_nxt
        return s_nxt, p_prev, alpha_prev

    m_ref[...] = jnp.full(m_ref.shape, -jnp.inf, F32)
    acc_ref[...] = jnp.zeros(acc_ref.shape, F32)
    p_ref[...] = jnp.zeros(p_ref.shape, BF16)
    alpha_ref[...] = jnp.ones(alpha_ref.shape, F32)
    s_ref[...] = scores(0, 0)

    def save(carry):
        s_ref[...], p_ref[...], alpha_ref[...] = carry

    def run(first, count):
        carry = None
        for k in range(count):
            carry = key_tile(first + k, False, carry)
        save(carry)

    def unrolled(jj, carry):
        run(ATT_UNROLL * jj, ATT_UNROLL)
        return carry

    lax.fori_loop(0, i // ATT_UNROLL, unrolled, 0)
    count = ATT_UNROLL // 2
    while count:
        @pl.when(i % (2 * count) >= count)
        def _(count=count):
            run(i - i % (2 * count), count)
        count //= 2

    s_ref[...] = visible_only(s_ref[...])
    _, p, alpha = key_tile(i, diagonal=True)
    weighted_values(i, last, p, alpha)

    for a in range(ATT_HEADS):
        o = acc_ref[a, :V_DIM, :] / acc_ref[a, V_DIM:V_DIM + 1, :]
        o_ref[:, a * V_DIM:(a + 1) * V_DIM] = o.T.astype(o_ref.dtype)


def _mla_attn(qt, kn, kr, vt, batch):
    nt, _, t = qt.shape
    nq = nt // batch
    s = nq * t
    ha = ATT_HEADS
    return pl.pallas_call(
        _mla_attn_kernel,
        grid=(batch, MLA_HEADS // ha, nq),
        in_specs=[
            pl.BlockSpec((None, ha * HEAD_PAD, t), lambda b, h, i: (b * nq + i, h, 0)),
            pl.BlockSpec((None, s, ha * QK_NOPE), lambda b, h, i: (b, 0, h)),
            pl.BlockSpec((None, s, LANES), lambda b, h, i: (b, 0, 0)),
            pl.BlockSpec((None, nq, ha * V_DIM, t), lambda b, h, i: (b, 0, h, 0)),
        ],
        out_specs=pl.BlockSpec((None, t, ha * V_DIM), lambda b, h, i: (b, i, h)),
        out_shape=jax.ShapeDtypeStruct((batch, s, MLA_HEADS * V_DIM), BF16),
        scratch_shapes=[
            pltpu.VMEM((ha, 1, t), F32),
            pltpu.VMEM((ha, V_DIM + ATT_SUM_ROWS, t), F32),
            pltpu.VMEM((t, t), F32),
            pltpu.VMEM((t, t), BF16),
            pltpu.VMEM((1, t), F32),
        ],
        compiler_params=_params("parallel", "parallel", "arbitrary"),
        name="mla_attn",
    )(qt, kn.reshape(batch, s, -1), kr.reshape(batch, s, LANES),
      vt.reshape(batch, nq, MLA_HEADS * V_DIM, t))


def _mla_out_kernel(x_ref, o_ref, w_ref, y_ref):
    y_ref[...] = x_ref[...] + jnp.dot(o_ref[...], w_ref[...], preferred_element_type=F32)


def _mla_out(x, o, w):
    m, d = x.shape
    tm = MLA_O_TM
    return pl.pallas_call(
        _mla_out_kernel,
        grid=(m // tm,),
        in_specs=[
            pl.BlockSpec((tm, d), lambda i: (i, 0)),
            pl.BlockSpec((tm, o.shape[1]), lambda i: (i, 0)),
            pl.BlockSpec(w.shape, lambda i: (0, 0)),
        ],
        out_specs=pl.BlockSpec((tm, d), lambda i: (i, 0)),
        out_shape=jax.ShapeDtypeStruct((m, d), F32),
        compiler_params=_params("parallel"),
        name="mla_out",
    )(x, o, w)


def _mla(x, positions, gain, w_in, gq, w_q_up, gkv, w_kv_up, w_out):
    b, _ = positions.shape
    m, d = x.shape
    w_in_p = jnp.pad(w_in, ((0, 0), (0, LANES - QK_ROPE))).astype(BF16)
    wqt = w_q_up.T.astype(BF16)
    w_kv = w_kv_up.reshape(KV_LORA, MLA_HEADS, QK_NOPE + V_DIM)
    wkn = w_kv[:, :, :QK_NOPE].reshape(KV_LORA, -1).astype(BF16)
    wvt = w_kv[:, :, QK_NOPE:].reshape(KV_LORA, -1).T.astype(BF16)
    qt, kn, kr, vt = _mla_proj(x, positions, gain, w_in_p, gq, wqt, gkv, wkn, wvt)
    o = _mla_attn(qt, kn, kr, vt, b)
    return _mla_out(x, o.reshape(m, -1), w_out.astype(BF16))


def kernel(x, positions, ln_ffn1, ffn1_w_in, ffn1_w_out, ln_mix, ln_ffn2, ffn2_w_in, ffn2_w_out,
           sgu_w_in, sgu_v_gain, sgu_v_bias, sgu_w_spatial, sgu_b_spatial, sgu_w_out,
           mla_w_in, mla_q_norm, mla_w_q_up, mla_kv_norm, mla_w_kv_up, mla_w_out, ln_final):
    b, s, d = x.shape
    depth = ln_ffn1.shape[0]
    h = x.reshape(b * s, d)

    ffn_stacks = [((w_in, i), (w_out, i)) for i in range(depth)
                  for w_in, w_out in ((ffn1_w_in, ffn1_w_out), (ffn2_w_in, ffn2_w_out))]
    ffn_weights = {0: (ffn1_w_in[0].astype(BF16), ffn1_w_out[0].astype(BF16))}
    uncast = list(range(1, len(ffn_stacks)))

    def side_cast(call, *args, extra=(), **kwargs):
        target = uncast.pop(0) if uncast else None
        pairs = (ffn_stacks[target] if target is not None else ()) + tuple(extra)
        out, cast = call(*args, cast_next=pairs, **kwargs)
        if target is not None:
            ffn_weights[target], cast = cast[:2], cast[2:]
        return (out, cast) if extra else out

    for i in range(depth):
        j = i // 2
        if i % 2 == 0:
            h, sgu_w = side_cast(_ffn, h, ln_ffn1[i], *ffn_weights[2 * i],
                                 extra=((sgu_w_in, j), (sgu_w_out, j)))
            h = side_cast(_sgu, h, ln_mix[i], sgu_w[0], sgu_v_gain[j], sgu_v_bias[j],
                          sgu_w_spatial[j], sgu_b_spatial[j], sgu_w[1])
        else:
            h = side_cast(_ffn, h, ln_ffn1[i], *ffn_weights[2 * i])
            h = _mla(h, positions, ln_mix[i], mla_w_in[j], mla_q_norm[j], mla_w_q_up[j],
                     mla_kv_norm[j], mla_w_kv_up[j], mla_w_out[j])
        last = i == depth - 1
        h = side_cast(_ffn, h, ln_ffn2[i], *ffn_weights[2 * i + 1],
                      final_gain=ln_final if last else None)
    return h.reshape(b, s, d)
```

```python
import functools
import math

import jax
import jax.numpy as jnp
from jax import lax
from jax.experimental import pallas as pl
from jax.experimental.pallas import tpu as pltpu

F32 = jnp.float32
BF16 = jnp.bfloat16

EPS = 1e-6
CHUNK = 64
SGU_BLOCK = 128
SGU_GROUPS = 8
MLA_HEADS = 16
Q_LORA = 512
KV_LORA = 512
QK_NOPE = 128
QK_ROPE = 64
V_DIM = 128
QK_DIM = QK_NOPE + QK_ROPE
ROPE_THETA = 10000.0

LANES = 128
BF16_SUBLANES = 16
HEAD_PAD = 2 * LANES
VMEM_LIMIT = 61 * 1024 * 1024

FFN_TM = 1024
FFN_TF = 512
FFN_ROWS = 64
FFN_COLS = 512
SGU_IN_TM = 256
SGU_OUT_TM = 256
SGU_COLS = 1024
SGU_OUT_COLS = 512
MLA_P_TM = 512
MLA_O_TM = 512
ATT_T = 512
ATT_HEADS = 4
ATT_UNROLL = 4
ATT_SUM_ROWS = 16


def _rms_normalize(x, gain):
    return x * lax.rsqrt(jnp.mean(x * x, axis=-1, keepdims=True) + EPS) * gain


def _params(*sem):
    return pltpu.CompilerParams(dimension_semantics=sem, vmem_limit_bytes=VMEM_LIMIT)


def _dot_nt(a, b):
    return lax.dot_general(a, b, (((1,), (1,)), ((), ())), preferred_element_type=F32)


def _cast_plan(stacked, layer, grid):
    _, rows, cols = stacked.shape
    steps = math.prod(grid)

    def split(n):
        for nb in range(1, n + 1):
            na = n // nb
            if (na * nb == n and rows % na == 0 and cols % nb == 0
                    and (rows // na) % BF16_SUBLANES == 0 and (cols // nb) % LANES == 0):
                return na, nb
        return None

    used = next(n for n in range(steps, 0, -1) if split(n))
    na, nb = split(used)
    block = (rows // na, cols // nb)

    def index(*ids):
        flat = ids[0]
        for extent, idx in zip(grid[1:], ids[1:]):
            flat = flat * extent + idx
        if used < steps:
            flat = jnp.minimum(flat, used - 1)
        return flat // nb, flat % nb

    src = pl.BlockSpec((None,) + block, lambda *ids: (layer,) + index(*ids))
    dst = pl.BlockSpec(block, index)
    return src, dst, jax.ShapeDtypeStruct((rows, cols), BF16)


def _cast_blocks(src_refs, dst_refs):
    for src, dst in zip(src_refs, dst_refs):
        dst[...] = src[...].astype(BF16)


def _ffn_kernel(x_ref, g_ref, wg_ref, wu_ref, wo_ref, *rest, final_norm, n_cast):
    rest = list(rest)
    gf_ref = rest.pop(0) if final_norm else None
    cast_src = [rest.pop(0) for _ in range(n_cast)]
    o_ref = rest.pop(0)
    cast_dst = [rest.pop(0) for _ in range(n_cast)]
    (xn_ref,) = rest
    j = pl.program_id(1)
    _cast_blocks(cast_src, cast_dst)

    tm, d = x_ref.shape
    row_chunks = [slice(r, r + FFN_ROWS) for r in range(0, tm, FFN_ROWS)]

    @pl.when(j == 0)
    def _():
        for rows in row_chunks:
            x = x_ref[rows, :]
            xn_ref[rows, :] = _rms_normalize(x, g_ref[...]).astype(BF16)
            o_ref[rows, :] = x

    xn = xn_ref[...]
    gate = jnp.dot(xn, wg_ref[...], preferred_element_type=F32)
    up = jnp.dot(xn, wu_ref[...], preferred_element_type=F32)
    act = (0.5 * gate * jax.nn.sigmoid(gate) * up).astype(BF16)
    for c in range(0, d, FFN_COLS):
        o_ref[:, c:c + FFN_COLS] += jnp.dot(act, wo_ref[:, c:c + FFN_COLS],
                                            preferred_element_type=F32)

    if final_norm:
        @pl.when(j == pl.num_programs(1) - 1)
        def _():
            for rows in row_chunks:
                y = o_ref[rows, :]
                rstd = lax.rsqrt(jnp.mean(y * y, axis=-1, keepdims=True) + EPS)
                o_ref[rows, :] = y * rstd * gf_ref[...]


def _ffn(x, gain, w_in, w_out, final_gain=None, cast_next=()):
    m, d = x.shape
    f = w_out.shape[0]
    nf = f // FFN_TF
    grid = (m // FFN_TM, nf)
    final_norm = final_gain is not None
    in_specs = [
        pl.BlockSpec((FFN_TM, d), lambda i, j: (i, 0)),
        pl.BlockSpec((1, d), lambda i, j: (0, 0)),
        pl.BlockSpec((d, FFN_TF), lambda i, j: (0, j)),
        pl.BlockSpec((d, FFN_TF), lambda i, j: (0, j + nf)),
        pl.BlockSpec((FFN_TF, d), lambda i, j: (j, 0)),
    ]
    args = [x, gain.reshape(1, d), w_in, w_in, w_out]
    if final_norm:
        in_specs.append(pl.BlockSpec((1, d), lambda i, j: (0, 0)))
        args.append(final_gain.reshape(1, d))
    out_specs = [pl.BlockSpec((FFN_TM, d), lambda i, j: (i, 0))]
    out_shape = [jax.ShapeDtypeStruct((m, d), F32)]
    stacks = [stacked for stacked, _ in cast_next]
    for stacked, layer in cast_next:
        src, dst, shape = _cast_plan(stacked, layer, grid)
        in_specs.append(src)
        args.append(stacked)
        out_specs.append(dst)
        out_shape.append(shape)
    out = pl.pallas_call(
        functools.partial(_ffn_kernel, final_norm=final_norm, n_cast=len(stacks)),
        grid=grid,
        in_specs=in_specs,
        out_specs=out_specs,
        out_shape=out_shape,
        scratch_shapes=[pltpu.VMEM((FFN_TM, d), BF16)],
        compiler_params=_params("parallel", "arbitrary"),
        name="ffn_final" if final_norm else "ffn",
    )(*args)
    return out[0], tuple(out[1:])


def _resident(shape):
    return pl.BlockSpec(shape, lambda i: (0,) * len(shape), pipeline_mode=pl.Buffered(1))


def _sgu_in_kernel(x_ref, g_ref, w_ref, vg_ref, vb_ref, *rest, n_cast):
    cast_src, (u_ref, vn_ref, *rest) = rest[:n_cast], rest[n_cast:]
    cast_dst, (xn_ref, v_ref) = rest[:n_cast], rest[n_cast:]
    _cast_blocks(cast_src, cast_dst)
    tm, width = u_ref.shape
    chunks = [slice(c, c + SGU_COLS) for c in range(0, width, SGU_COLS)]
    tiles = [slice(c, c + LANES) for c in range(0, width, LANES)]

    def project(cols):
        return jnp.dot(xn_ref[...], w_ref[:, cols], preferred_element_type=F32)

    xn_ref[...] = _rms_normalize(x_ref[...], g_ref[...]).astype(BF16)

    def mean_pass():
        total = jnp.zeros((tm, LANES), F32)
        for t in tiles:
            total += v_ref[:, t]
        return jnp.broadcast_to(jnp.sum(total, axis=-1, keepdims=True) / width, (tm, LANES))

    def rstd_pass(mu):
        sq = jnp.zeros((tm, LANES), F32)
        for t in tiles:
            dv = v_ref[:, t] - mu
            sq += dv * dv
        var = jnp.sum(sq, axis=-1, keepdims=True) / width
        return jnp.broadcast_to(lax.rsqrt(var + EPS), (tm, LANES))

    def normalize(mu, rstd):
        for t in tiles:
            vn_ref[:, t] = ((v_ref[:, t] - mu) * rstd * vg_ref[:, t] + vb_ref[:, t]).astype(BF16)

    for cols in chunks:
        v_ref[:, cols] = jax.nn.gelu(project(slice(width + cols.start, width + cols.stop)))
    stats = {}
    side_work = [lambda: stats.update(mu=mean_pass()),
                 lambda: stats.update(rstd=rstd_pass(stats["mu"])),
                 lambda: normalize(stats["mu"], stats["rstd"])]
    assert len(chunks) >= len(side_work)
    for k, cols in enumerate(chunks):
        u = project(cols)
        if k < len(side_work):
            side_work[k]()
        u_ref[:, cols] = jax.nn.gelu(u).astype(BF16)


def _sgu_out_kernel(x_ref, u_ref, vn_ref, ws_ref, bs_ref, wo_ref, *rest, n_cast):
    cast_src, (o_ref, *rest) = rest[:n_cast], rest[n_cast:]
    cast_dst, (gated_ref,) = rest[:n_cast], rest[n_cast:]
    _cast_blocks(cast_src, cast_dst)
    tm, width = u_ref.shape
    d = o_ref.shape[1]
    gw = width // SGU_GROUPS
    row_chunk = lax.broadcasted_iota(jnp.int32, (SGU_BLOCK, SGU_BLOCK), 0) // CHUNK
    col_chunk = lax.broadcasted_iota(jnp.int32, (SGU_BLOCK, SGU_BLOCK), 1) // CHUNK
    visible = row_chunk >= col_chunk
    for g in range(SGU_GROUPS):
        lanes = slice(g * gw, (g + 1) * gw)
        ws = jnp.where(visible, ws_ref[g], 0.0).astype(BF16)
        bias = jnp.tile(bs_ref[g], (1, gw // LANES))
        for b in range(tm // SGU_BLOCK):
            rows = slice(b * SGU_BLOCK, (b + 1) * SGU_BLOCK)
            mixed = jnp.dot(ws, vn_ref[rows, lanes], preferred_element_type=F32) + bias
            gated_ref[rows, lanes] = (u_ref[rows, lanes].astype(F32) * mixed).astype(BF16)
    gated = gated_ref[...]
    for c in range(0, d, SGU_OUT_COLS):
        cols = slice(c, c + SGU_OUT_COLS)
        o_ref[:, cols] = x_ref[:, cols] + jnp.dot(gated, wo_ref[:, cols],
                                                  preferred_element_type=F32)


def _sgu(x, gain, w_in, v_gain, v_bias, w_spatial, b_spatial, w_out, cast_next=()):
    m, d = x.shape
    width = w_out.shape[0]
    rows = lambda tm: (lambda i: (i, 0))

    def with_cast(in_specs, args, out_specs, out_shape, pairs, n_tiles):
        for stacked, layer in pairs:
            src, dst, shape = _cast_plan(stacked, layer, (n_tiles,))
            in_specs.append(src)
            args.append(stacked)
            out_specs.append(dst)
            out_shape.append(shape)
        return len(pairs)

    tm = SGU_IN_TM
    in_specs = [pl.BlockSpec((tm, d), rows(tm)), _resident((1, d)), _resident(w_in.shape),
                _resident((1, width)), _resident((1, width))]
    args = [x, gain.reshape(1, d), w_in, v_gain.reshape(1, width), v_bias.reshape(1, width)]
    out_specs = [pl.BlockSpec((tm, width), rows(tm)), pl.BlockSpec((tm, width), rows(tm))]
    out_shape = [jax.ShapeDtypeStruct((m, width), BF16), jax.ShapeDtypeStruct((m, width), BF16)]
    n_cast = with_cast(in_specs, args, out_specs, out_shape, cast_next[:1], m // tm)
    u, vn, *cast_in = pl.pallas_call(
        functools.partial(_sgu_in_kernel, n_cast=n_cast),
        grid=(m // tm,),
        in_specs=in_specs,
        out_specs=out_specs,
        out_shape=out_shape,
        scratch_shapes=[pltpu.VMEM((tm, d), BF16), pltpu.VMEM((tm, width), F32)],
        compiler_params=_params("parallel"),
        name="sgu_in",
    )(*args)

    tm = SGU_OUT_TM
    bias = jnp.broadcast_to(b_spatial[:, :, None], (SGU_GROUPS, SGU_BLOCK, LANES))
    in_specs = [pl.BlockSpec((tm, d), rows(tm)), pl.BlockSpec((tm, width), rows(tm)),
                pl.BlockSpec((tm, width), rows(tm)), _resident(w_spatial.shape),
                _resident(bias.shape), _resident(w_out.shape)]
    args = [x, u, vn, w_spatial, bias, w_out]
    out_specs = [pl.BlockSpec((tm, d), rows(tm))]
    out_shape = [jax.ShapeDtypeStruct((m, d), F32)]
    n_cast = with_cast(in_specs, args, out_specs, out_shape, cast_next[1:], m // tm)
    y, *cast_out = pl.pallas_call(
        functools.partial(_sgu_out_kernel, n_cast=n_cast),
        grid=(m // tm,),
        in_specs=in_specs,
        out_specs=out_specs,
        out_shape=out_shape,
        scratch_shapes=[pltpu.VMEM((tm, width), BF16)],
        compiler_params=_params("parallel"),
        name="sgu_out",
    )(*args)
    return y, tuple(cast_in + cast_out)


def _mla_proj_kernel(x_ref, posr_ref, freqc_ref, g_ref, wi_ref, gq_ref,
                     wqt_ref, gkv_ref, wkn_ref, wvt_ref, qt_ref, kn_ref, kr_ref, vt_ref):
    half = QK_ROPE // 2
    hn = _rms_normalize(x_ref[...], g_ref[...]).astype(BF16)
    proj = jnp.dot(hn, wi_ref[...], preferred_element_type=F32)
    qn = _rms_normalize(proj[:, :Q_LORA], gq_ref[...]).astype(BF16)
    kvn = _rms_normalize(proj[:, Q_LORA:Q_LORA + KV_LORA], gkv_ref[...]).astype(BF16)
    kr = proj[:, Q_LORA + KV_LORA:]

    ang_t = freqc_ref[...] * posr_ref[...].astype(F32)
    cos_t = jnp.cos(ang_t)
    sin_t = jnp.sin(ang_t)

    kr_t = kr.T
    k1, k2 = kr_t[:half], kr_t[half:QK_ROPE]
    kr_t = jnp.concatenate([k1 * cos_t - k2 * sin_t, k1 * sin_t + k2 * cos_t, kr_t[QK_ROPE:]],
                           axis=0)
    kr_ref[...] = kr_t.T.astype(BF16)

    kn_ref[...] = jnp.dot(kvn, wkn_ref[...], preferred_element_type=F32).astype(BF16)
    vt_ref[...] = _dot_nt(wvt_ref[...], kvn).astype(BF16)

    scale = QK_DIM ** -0.5 * math.log2(math.e)
    q_t = _dot_nt(wqt_ref[...], qn) * scale
    for h in range(MLA_HEADS):
        src = h * QK_DIM
        dst = h * HEAD_PAD
        x1 = q_t[src + QK_NOPE:src + QK_NOPE + half]
        x2 = q_t[src + QK_NOPE + half:src + QK_DIM]
        qt_ref[dst:dst + QK_NOPE, :] = q_t[src:src + QK_NOPE].astype(BF16)
        qt_ref[dst + QK_NOPE:dst + QK_NOPE + half, :] = (x1 * cos_t - x2 * sin_t).astype(BF16)
        qt_ref[dst + QK_NOPE + half:dst + QK_DIM, :] = (x1 * sin_t + x2 * cos_t).astype(BF16)
        qt_ref[dst + QK_DIM:dst + HEAD_PAD, :] = jnp.zeros((HEAD_PAD - QK_DIM, q_t.shape[1]), BF16)


def _mla_proj(x, positions, gain, w_in, gq, wqt, gkv, wkn, wvt):
    m, d = x.shape
    tm = MLA_P_TM
    r = ATT_T // tm
    nt = m // ATT_T
    half = QK_ROPE // 2
    inv_freq = 1.0 / (ROPE_THETA ** (jnp.arange(half, dtype=F32) / half))
    const = lambda i: (0, 0)

    def resident(shape):
        return pl.BlockSpec(shape, const, pipeline_mode=pl.Buffered(1))

    return pl.pallas_call(
        _mla_proj_kernel,
        grid=(m // tm,),
        in_specs=[
            pl.BlockSpec((tm, d), lambda i: (i, 0)),
            pl.BlockSpec((None, 1, tm), lambda i: (i, 0, 0)),
            resident((half, 1)),
            resident((1, d)),
            resident(w_in.shape),
            resident((1, Q_LORA)),
            resident(wqt.shape),
            resident((1, KV_LORA)),
            resident(wkn.shape),
            resident(wvt.shape),
        ],
        out_specs=[
            pl.BlockSpec((None, MLA_HEADS * HEAD_PAD, tm), lambda i: (i // r, 0, i % r)),
            pl.BlockSpec((tm, MLA_HEADS * QK_NOPE), lambda i: (i, 0)),
            pl.BlockSpec((tm, LANES), lambda i: (i, 0)),
            pl.BlockSpec((None, MLA_HEADS * V_DIM, tm), lambda i: (i // r, 0, i % r)),
        ],
        out_shape=[
            jax.ShapeDtypeStruct((nt, MLA_HEADS * HEAD_PAD, ATT_T), BF16),
            jax.ShapeDtypeStruct((m, MLA_HEADS * QK_NOPE), BF16),
            jax.ShapeDtypeStruct((m, LANES), BF16),
            jax.ShapeDtypeStruct((nt, MLA_HEADS * V_DIM, ATT_T), BF16),
        ],
        compiler_params=_params("parallel"),
        name="mla_proj",
    )(x, positions.reshape(m // tm, 1, tm), inv_freq.reshape(half, 1), gain.reshape(1, d),
      w_in, gq.reshape(1, -1), wqt, gkv.reshape(1, -1), wkn, wvt)


def _mla_attn_kernel(qt_ref, kn_ref, kr_ref, vt_ref, o_ref,
                     m_ref, acc_ref, s_ref, p_ref, alpha_ref):
    i = pl.program_id(2)
    t = qt_ref.shape[1]
    last = ATT_HEADS - 1

    def visible_only(s):
        key_chunk = lax.broadcasted_iota(jnp.int32, (t, 1), 0) // CHUNK
        qry_chunk = lax.broadcasted_iota(jnp.int32, (1, t), 1) // CHUNK
        return jnp.where(key_chunk <= qry_chunk, s, -jnp.inf)

    def scores(j, a):
        rows = pl.ds(pl.multiple_of(j * t, t), t)
        k = jnp.concatenate([kn_ref[rows, a * QK_NOPE:(a + 1) * QK_NOPE], kr_ref[rows, :]], axis=1)
        return jnp.dot(k, qt_ref[a * HEAD_PAD:(a + 1) * HEAD_PAD, :],
                       preferred_element_type=F32)

    ones_rows = jnp.ones((ATT_SUM_ROWS, t), BF16)

    def weighted_values(j, a, p, alpha):
        v_ext = jnp.concatenate([vt_ref[j, a * V_DIM:(a + 1) * V_DIM, :], ones_rows], axis=0)
        pv = jnp.dot(v_ext, p, preferred_element_type=F32)
        acc_ref[a] = alpha * acc_ref[a] + pv

    def softmax(a, s):
        m_prev = m_ref[a]
        m_new = jnp.maximum(m_prev, jnp.max(s, axis=0, keepdims=True))
        alpha = jnp.exp2(m_prev - m_new)
        p = jnp.exp2(s - m_new)
        m_ref[a] = m_new
        return p.astype(BF16), alpha

    def key_tile(j, diagonal, carry=None):
        s_cur, p_prev, alpha_prev = carry or (s_ref[...], p_ref[...], alpha_ref[...])
        for a in range(ATT_HEADS):
            if a < last:
                s_nxt = scores(j, a + 1)
                if diagonal:
                    s_nxt = visible_only(s_nxt)
            elif not diagonal:
                s_nxt = scores(j + 1, 0)
            weighted_values(jnp.maximum(j - 1, 0) if a == 0 else j, (a - 1) % ATT_HEADS,
                            p_prev, alpha_prev)
            p_prev, alpha_prev = softmax(a, s_cur)
            s_cur = s_nxt
        return s_nxt, p_prev, alpha_prev

    m_ref[...] = jnp.full(m_ref.shape, -jnp.inf, F32)
    acc_ref[...] = jnp.zeros(acc_ref.shape, F32)
    p_ref[...] = jnp.zeros(p_ref.shape, BF16)
    alpha_ref[...] = jnp.ones(alpha_ref.shape, F32)
    s_ref[...] = scores(0, 0)

    def save(carry):
        s_ref[...], p_ref[...], alpha_ref[...] = carry

    def run(first, count):
        carry = None
        for k in range(count):
            carry = key_tile(first + k, False, carry)
        save(carry)

    def unrolled(jj, carry):
        run(ATT_UNROLL * jj, ATT_UNROLL)
        return carry

    lax.fori_loop(0, i // ATT_UNROLL, unrolled, 0)
    count = ATT_UNROLL // 2
    while count:
        @pl.when(i % (2 * count) >= count)
        def _(count=count):
            run(i - i % (2 * count), count)
        count //= 2

    s_ref[...] = visible_only(s_ref[...])
    _, p, alpha = key_tile(i, diagonal=True)
    weighted_values(i, last, p, alpha)

    for a in range(ATT_HEADS):
        o = acc_ref[a, :V_DIM, :] / acc_ref[a, V_DIM:V_DIM + 1, :]
        o_ref[:, a * V_DIM:(a + 1) * V_DIM] = o.T.astype(o_ref.dtype)


def _mla_attn(qt, kn, kr, vt, batch):
    nt, _, t = qt.shape
    nq = nt // batch
    s = nq * t
    ha = ATT_HEADS
    return pl.pallas_call(
        _mla_attn_kernel,
        grid=(batch, MLA_HEADS // ha, nq),
        in_specs=[
            pl.BlockSpec((None, ha * HEAD_PAD, t), lambda b, h, i: (b * nq + i, h, 0)),
            pl.BlockSpec((None, s, ha * QK_NOPE), lambda b, h, i: (b, 0, h)),
            pl.BlockSpec((None, s, LANES), lambda b, h, i: (b, 0, 0)),
            pl.BlockSpec((None, nq, ha * V_DIM, t), lambda b, h, i: (b, 0, h, 0)),
        ],
        out_specs=pl.BlockSpec((None, t, ha * V_DIM), lambda b, h, i: (b, i, h)),
        out_shape=jax.ShapeDtypeStruct((batch, s, MLA_HEADS * V_DIM), BF16),
        scratch_shapes=[
            pltpu.VMEM((ha, 1, t), F32),
            pltpu.VMEM((ha, V_DIM + ATT_SUM_ROWS, t), F32),
            pltpu.VMEM((t, t), F32),
            pltpu.VMEM((t, t), BF16),
            pltpu.VMEM((1, t), F32),
        ],
        compiler_params=_params("parallel", "parallel", "arbitrary"),
        name="mla_attn",
    )(qt, kn.reshape(batch, s, -1), kr.reshape(batch, s, LANES),
      vt.reshape(batch, nq, MLA_HEADS * V_DIM, t))


def _mla_out_kernel(x_ref, o_ref, w_ref, y_ref):
    y_ref[...] = x_ref[...] + jnp.dot(o_ref[...], w_ref[...], preferred_element_type=F32)


def _mla_out(x, o, w):
    m, d = x.shape
    tm = MLA_O_TM
    return pl.pallas_call(
        _mla_out_kernel,
        grid=(m // tm,),
        in_specs=[
            pl.BlockSpec((tm, d), lambda i: (i, 0)),
            pl.BlockSpec((tm, o.shape[1]), lambda i: (i, 0)),
            pl.BlockSpec(w.shape, lambda i: (0, 0)),
        ],
        out_specs=pl.BlockSpec((tm, d), lambda i: (i, 0)),
        out_shape=jax.ShapeDtypeStruct((m, d), F32),
        compiler_params=_params("parallel"),
        name="mla_out",
    )(x, o, w)


def _mla(x, positions, gain, w_in, gq, w_q_up, gkv, w_kv_up, w_out):
    b, _ = positions.shape
    m, d = x.shape
    w_in_p = jnp.pad(w_in, ((0, 0), (0, LANES - QK_ROPE))).astype(BF16)
    wqt = w_q_up.T.astype(BF16)
    w_kv = w_kv_up.reshape(KV_LORA, MLA_HEADS, QK_NOPE + V_DIM)
    wkn = w_kv[:, :, :QK_NOPE].reshape(KV_LORA, -1).astype(BF16)
    wvt = w_kv[:, :, QK_NOPE:].reshape(KV_LORA, -1).T.astype(BF16)
    qt, kn, kr, vt = _mla_proj(x, positions, gain, w_in_p, gq, wqt, gkv, wkn, wvt)
    o = _mla_attn(qt, kn, kr, vt, b)
    return _mla_out(x, o.reshape(m, -1), w_out.astype(BF16))


def kernel(x, positions, ln_ffn1, ffn1_w_in, ffn1_w_out, ln_mix, ln_ffn2, ffn2_w_in, ffn2_w_out,
           sgu_w_in, sgu_v_gain, sgu_v_bias, sgu_w_spatial, sgu_b_spatial, sgu_w_out,
           mla_w_in, mla_q_norm, mla_w_q_up, mla_kv_norm, mla_w_kv_up, mla_w_out, ln_final):
    b, s, d = x.shape
    depth = ln_ffn1.shape[0]
    h = x.reshape(b * s, d)

    ffn_stacks = [((w_in, i), (w_out, i)) for i in range(depth)
                  for w_in, w_out in ((ffn1_w_in, ffn1_w_out), (ffn2_w_in, ffn2_w_out))]
    ffn_weights = {0: (ffn1_w_in[0].astype(BF16), ffn1_w_out[0].astype(BF16))}
    uncast = list(range(1, len(ffn_stacks)))

    def side_cast(call, *args, extra=(), **kwargs):
        target = uncast.pop(0) if uncast else None
        pairs = (ffn_stacks[target] if target is not None else ()) + tuple(extra)
        out, cast = call(*args, cast_next=pairs, **kwargs)
        if target is not None:
            ffn_weights[target], cast = cast[:2], cast[2:]
        return (out, cast) if extra else out

    for i in range(depth):
        j = i // 2
        if i % 2 == 0:
            h, sgu_w = side_cast(_ffn, h, ln_ffn1[i], *ffn_weights[2 * i],
                                 extra=((sgu_w_in, j), (sgu_w_out, j)))
            h = side_cast(_sgu, h, ln_mix[i], sgu_w[0], sgu_v_gain[j], sgu_v_bias[j],
                          sgu_w_spatial[j], sgu_b_spatial[j], sgu_w[1])
        else:
            h = side_cast(_ffn, h, ln_ffn1[i], *ffn_weights[2 * i])
            h = _mla(h, positions, ln_mix[i], mla_w_in[j], mla_q_norm[j], mla_w_q_up[j],
                     mla_kv_norm[j], mla_w_kv_up[j], mla_w_out[j])
        last = i == depth - 1
        h = side_cast(_ffn, h, ln_ffn2[i], *ffn_weights[2 * i + 1],
                      final_gain=ln_final if last else None)
    return h.reshape(b, s, d)
```

```python
import functools
import math

import jax
import jax.numpy as jnp
from jax import lax
from jax.experimental import pallas as pl
from jax.experimental.pallas import tpu as pltpu

F32 = jnp.float32
BF16 = jnp.bfloat16

EPS = 1e-6
CHUNK = 64
SGU_BLOCK = 128
SGU_GROUPS = 8
MLA_HEADS = 16
Q_LORA = 512
KV_LORA = 512
QK_NOPE = 128
QK_ROPE = 64
V_DIM = 128
QK_DIM = QK_NOPE + QK_ROPE
ROPE_THETA = 10000.0

LANES = 128
BF16_SUBLANES = 16
HEAD_PAD = 2 * LANES
VMEM_LIMIT = 61 * 1024 * 1024

FFN_TM = 1024
FFN_TF = 512
FFN_SUB = 256
FFN_ROWS = 64
FFN_COLS = 512
SGU_IN_TM = 256
SGU_OUT_TM = 256
SGU_COLS = 1024
SGU_OUT_COLS = 512
MLA_P_TM = 512
MLA_O_TM = 512
ATT_T = 512
ATT_HEADS = 4
ATT_UNROLL = 4
ATT_SUM_ROWS = 16


def _rms_normalize(x, gain):
    return x * lax.rsqrt(jnp.mean(x * x, axis=-1, keepdims=True) + EPS) * gain


def _params(*sem):
    return pltpu.CompilerParams(dimension_semantics=sem, vmem_limit_bytes=VMEM_LIMIT)


def _dot_nt(a, b):
    return lax.dot_general(a, b, (((1,), (1,)), ((), ())), preferred_element_type=F32)


def _cast_plan(stacked, layer, grid):
    _, rows, cols = stacked.shape
    steps = math.prod(grid)

    def split(n):
        for nb in range(1, n + 1):
            na = n // nb
            if (na * nb == n and rows % na == 0 and cols % nb == 0
                    and (rows // na) % BF16_SUBLANES == 0 and (cols // nb) % LANES == 0):
                return na, nb
        return None

    used = next(n for n in range(steps, 0, -1) if split(n))
    na, nb = split(used)
    block = (rows // na, cols // nb)

    def index(*ids):
        flat = ids[0]
        for extent, idx in zip(grid[1:], ids[1:]):
            flat = flat * extent + idx
        if used < steps:
            flat = jnp.minimum(flat, used - 1)
        return flat // nb, flat % nb

    src = pl.BlockSpec((None,) + block, lambda *ids: (layer,) + index(*ids))
    dst = pl.BlockSpec(block, index)
    return src, dst, jax.ShapeDtypeStruct((rows, cols), BF16)


def _gate_up_layout(w_in):
    d, f2 = w_in.shape
    nf = f2 // (2 * FFN_TF)
    return (w_in.reshape(d, 2, nf, FFN_TF // FFN_SUB, FFN_SUB).transpose(2, 0, 3, 1, 4)
            .reshape(nf, d, 2 * FFN_TF))


def _cast_plan_gate_up(stacked, layer, grid):
    _, rows, cols = stacked.shape
    steps = math.prod(grid)
    ncb = cols // FFN_SUB
    subs = FFN_TF // FFN_SUB
    nrb = steps // ncb
    assert nrb * ncb == steps and rows % nrb == 0 and (rows // nrb) % BF16_SUBLANES == 0
    block = (rows // nrb, FFN_SUB)

    def flat(*ids):
        out = ids[0]
        for extent, idx in zip(grid[1:], ids[1:]):
            out = out * extent + idx
        return out

    def src_index(*ids):
        return layer, flat(*ids) // ncb, flat(*ids) % ncb

    def dst_index(*ids):
        rb, cb = flat(*ids) // ncb, flat(*ids) % ncb
        half, rem = cb // (ncb // 2), cb % (ncb // 2)
        return rem // subs, rb, (rem % subs) * 2 + half

    src = pl.BlockSpec((None,) + block, src_index)
    dst = pl.BlockSpec((None,) + block, dst_index)
    shape = jax.ShapeDtypeStruct((cols // (2 * FFN_TF), rows, 2 * FFN_TF), BF16)
    return src, dst, shape


def _cast_blocks(src_refs, dst_refs):
    for src, dst in zip(src_refs, dst_refs):
        dst[...] = src[...].astype(BF16)


def _ffn_kernel(x_ref, g_ref, wgu_ref, wo_ref, *rest, final_norm, n_cast):
    rest = list(rest)
    gf_ref = rest.pop(0) if final_norm else None
    cast_src = [rest.pop(0) for _ in range(n_cast)]
    o_ref = rest.pop(0)
    cast_dst = [rest.pop(0) for _ in range(n_cast)]
    (xn_ref,) = rest
    j = pl.program_id(1)
    _cast_blocks(cast_src, cast_dst)

    tm, d = x_ref.shape
    row_chunks = [slice(r, r + FFN_ROWS) for r in range(0, tm, FFN_ROWS)]

    @pl.when(j == 0)
    def _():
        for rows in row_chunks:
            x = x_ref[rows, :]
            xn_ref[rows, :] = _rms_normalize(x, g_ref[...]).astype(BF16)
            o_ref[rows, :] = x

    gu = jnp.dot(xn_ref[...], wgu_ref[...], preferred_element_type=F32)
    acts = []
    for c in range(0, gu.shape[1], 2 * FFN_SUB):
        gate, up = gu[:, c:c + FFN_SUB], gu[:, c + FFN_SUB:c + 2 * FFN_SUB]
        acts.append((0.5 * gate * jax.nn.sigmoid(gate) * up).astype(BF16))
    act = jnp.concatenate(acts, axis=1)
    for c in range(0, d, FFN_COLS):
        o_ref[:, c:c + FFN_COLS] += jnp.dot(act, wo_ref[:, c:c + FFN_COLS],
                                            preferred_element_type=F32)

    if final_norm:
        @pl.when(j == pl.num_programs(1) - 1)
        def _():
            for rows in row_chunks:
                y = o_ref[rows, :]
                rstd = lax.rsqrt(jnp.mean(y * y, axis=-1, keepdims=True) + EPS)
                o_ref[rows, :] = y * rstd * gf_ref[...]


def _ffn(x, gain, w_gu, w_out, final_gain=None, cast_next=()):
    m, d = x.shape
    f = w_out.shape[0]
    nf = f // FFN_TF
    grid = (m // FFN_TM, nf)
    final_norm = final_gain is not None
    in_specs = [
        pl.BlockSpec((FFN_TM, d), lambda i, j: (i, 0)),
        pl.BlockSpec((1, d), lambda i, j: (0, 0)),
        pl.BlockSpec((None, d, 2 * FFN_TF), lambda i, j: (j, 0, 0)),
        pl.BlockSpec((FFN_TF, d), lambda i, j: (j, 0)),
    ]
    args = [x, gain.reshape(1, d), w_gu, w_out]
    if final_norm:
        in_specs.append(pl.BlockSpec((1, d), lambda i, j: (0, 0)))
        args.append(final_gain.reshape(1, d))
    out_specs = [pl.BlockSpec((FFN_TM, d), lambda i, j: (i, 0))]
    out_shape = [jax.ShapeDtypeStruct((m, d), F32)]
    stacks = [stacked for stacked, _, _ in cast_next]
    for stacked, layer, gate_up in cast_next:
        src, dst, shape = (_cast_plan_gate_up if gate_up else _cast_plan)(stacked, layer, grid)
        in_specs.append(src)
        args.append(stacked)
        out_specs.append(dst)
        out_shape.append(shape)
    out = pl.pallas_call(
        functools.partial(_ffn_kernel, final_norm=final_norm, n_cast=len(stacks)),
        grid=grid,
        in_specs=in_specs,
        out_specs=out_specs,
        out_shape=out_shape,
        scratch_shapes=[pltpu.VMEM((FFN_TM, d), BF16)],
        compiler_params=_params("parallel", "arbitrary"),
        name="ffn_final" if final_norm else "ffn",
    )(*args)
    return out[0], tuple(out[1:])


def _resident(shape):
    return pl.BlockSpec(shape, lambda i: (0,) * len(shape), pipeline_mode=pl.Buffered(1))


def _sgu_in_kernel(x_ref, g_ref, w_ref, vg_ref, vb_ref, *rest, n_cast):
    cast_src, (u_ref, vn_ref, *rest) = rest[:n_cast], rest[n_cast:]
    cast_dst, (xn_ref, v_ref) = rest[:n_cast], rest[n_cast:]
    _cast_blocks(cast_src, cast_dst)
    tm, width = u_ref.shape
    chunks = [slice(c, c + SGU_COLS) for c in range(0, width, SGU_COLS)]
    tiles = [slice(c, c + LANES) for c in range(0, width, LANES)]

    def project(cols):
        return jnp.dot(xn_ref[...], w_ref[:, cols], preferred_element_type=F32)

    xn_ref[...] = _rms_normalize(x_ref[...], g_ref[...]).astype(BF16)

    def mean_pass():
        total = jnp.zeros((tm, LANES), F32)
        for t in tiles:
            total += v_ref[:, t]
        return jnp.broadcast_to(jnp.sum(total, axis=-1, keepdims=True) / width, (tm, LANES))

    def rstd_pass(mu):
        sq = jnp.zeros((tm, LANES), F32)
        for t in tiles:
            dv = v_ref[:, t] - mu
            sq += dv * dv
        var = jnp.sum(sq, axis=-1, keepdims=True) / width
        return jnp.broadcast_to(lax.rsqrt(var + EPS), (tm, LANES))

    def normalize(mu, rstd):
        for t in tiles:
            vn_ref[:, t] = ((v_ref[:, t] - mu) * rstd * vg_ref[:, t] + vb_ref[:, t]).astype(BF16)

    for cols in chunks:
        v_ref[:, cols] = jax.nn.gelu(project(slice(width + cols.start, width + cols.stop)))
    stats = {}
    side_work = [lambda: stats.update(mu=mean_pass()),
                 lambda: stats.update(rstd=rstd_pass(stats["mu"])),
                 lambda: normalize(stats["mu"], stats["rstd"])]
    assert len(chunks) >= len(side_work)
    for k, cols in enumerate(chunks):
        u = project(cols)
        if k < len(side_work):
            side_work[k]()
        u_ref[:, cols] = jax.nn.gelu(u).astype(BF16)


def _sgu_out_kernel(x_ref, u_ref, vn_ref, ws_ref, bs_ref, wo_ref, *rest, n_cast):
    cast_src, (o_ref, *rest) = rest[:n_cast], rest[n_cast:]
    cast_dst, (gated_ref,) = rest[:n_cast], rest[n_cast:]
    _cast_blocks(cast_src, cast_dst)
    tm, width = u_ref.shape
    d = o_ref.shape[1]
    gw = width // SGU_GROUPS
    row_chunk = lax.broadcasted_iota(jnp.int32, (SGU_BLOCK, SGU_BLOCK), 0) // CHUNK
    col_chunk = lax.broadcasted_iota(jnp.int32, (SGU_BLOCK, SGU_BLOCK), 1) // CHUNK
    visible = row_chunk >= col_chunk
    for g in range(SGU_GROUPS):
        lanes = slice(g * gw, (g + 1) * gw)
        ws = jnp.where(visible, ws_ref[g], 0.0).astype(BF16)
        bias = jnp.tile(bs_ref[g], (1, gw // LANES))
        for b in range(tm // SGU_BLOCK):
            rows = slice(b * SGU_BLOCK, (b + 1) * SGU_BLOCK)
            mixed = jnp.dot(ws, vn_ref[rows, lanes], preferred_element_type=F32) + bias
            gated_ref[rows, lanes] = (u_ref[rows, lanes].astype(F32) * mixed).astype(BF16)
    gated = gated_ref[...]
    for c in range(0, d, SGU_OUT_COLS):
        cols = slice(c, c + SGU_OUT_COLS)
        o_ref[:, cols] = x_ref[:, cols] + jnp.dot(gated, wo_ref[:, cols],
                                                  preferred_element_type=F32)


def _sgu(x, gain, w_in, v_gain, v_bias, w_spatial, b_spatial, w_out, cast_next=()):
    m, d = x.shape
    width = w_out.shape[0]
    rows = lambda tm: (lambda i: (i, 0))

    def with_cast(in_specs, args, out_specs, out_shape, pairs, n_tiles):
        for stacked, layer, *_ in pairs:
            src, dst, shape = _cast_plan(stacked, layer, (n_tiles,))
            in_specs.append(src)
            args.append(stacked)
            out_specs.append(dst)
            out_shape.append(shape)
        return len(pairs)

    tm = SGU_IN_TM
    in_specs = [pl.BlockSpec((tm, d), rows(tm)), _resident((1, d)), _resident(w_in.shape),
                _resident((1, width)), _resident((1, width))]
    args = [x, gain.reshape(1, d), w_in, v_gain.reshape(1, width), v_bias.reshape(1, width)]
    out_specs = [pl.BlockSpec((tm, width), rows(tm)), pl.BlockSpec((tm, width), rows(tm))]
    out_shape = [jax.ShapeDtypeStruct((m, width), BF16), jax.ShapeDtypeStruct((m, width), BF16)]
    n_cast = with_cast(in_specs, args, out_specs, out_shape, cast_next[:1], m // tm)
    u, vn, *cast_in = pl.pallas_call(
        functools.partial(_sgu_in_kernel, n_cast=n_cast),
        grid=(m // tm,),
        in_specs=in_specs,
        out_specs=out_specs,
        out_shape=out_shape,
        scratch_shapes=[pltpu.VMEM((tm, d), BF16), pltpu.VMEM((tm, width), F32)],
        compiler_params=_params("parallel"),
        name="sgu_in",
    )(*args)

    tm = SGU_OUT_TM
    bias = jnp.broadcast_to(b_spatial[:, :, None], (SGU_GROUPS, SGU_BLOCK, LANES))
    in_specs = [pl.BlockSpec((tm, d), rows(tm)), pl.BlockSpec((tm, width), rows(tm)),
                pl.BlockSpec((tm, width), rows(tm)), _resident(w_spatial.shape),
                _resident(bias.shape), _resident(w_out.shape)]
    args = [x, u, vn, w_spatial, bias, w_out]
    out_specs = [pl.BlockSpec((tm, d), rows(tm))]
    out_shape = [jax.ShapeDtypeStruct((m, d), F32)]
    n_cast = with_cast(in_specs, args, out_specs, out_shape, cast_next[1:], m // tm)
    y, *cast_out = pl.pallas_call(
        functools.partial(_sgu_out_kernel, n_cast=n_cast),
        grid=(m // tm,),
        in_specs=in_specs,
        out_specs=out_specs,
        out_shape=out_shape,
        scratch_shapes=[pltpu.VMEM((tm, width), BF16)],
        compiler_params=_params("parallel"),
        name="sgu_out",
    )(*args)
    return y, tuple(cast_in + cast_out)


def _mla_proj_kernel(x_ref, posr_ref, freqc_ref, g_ref, wi_ref, gq_ref,
                     wqt_ref, gkv_ref, wkn_ref, wvt_ref, qt_ref, kn_ref, kr_ref, vt_ref):
    half = QK_ROPE // 2
    hn = _rms_normalize(x_ref[...], g_ref[...]).astype(BF16)
    proj = jnp.dot(hn, wi_ref[...], preferred_element_type=F32)
    qn = _rms_normalize(proj[:, :Q_LORA], gq_ref[...]).astype(BF16)
    kvn = _rms_normalize(proj[:, Q_LORA:Q_LORA + KV_LORA], gkv_ref[...]).astype(BF16)
    kr = proj[:, Q_LORA + KV_LORA:]

    ang_t = freqc_ref[...] * posr_ref[...].astype(F32)
    cos_t = jnp.cos(ang_t)
    sin_t = jnp.sin(ang_t)

    kr_t = kr.T
    k1, k2 = kr_t[:half], kr_t[half:QK_ROPE]
    kr_t = jnp.concatenate([k1 * cos_t - k2 * sin_t, k1 * sin_t + k2 * cos_t, kr_t[QK_ROPE:]],
                           axis=0)
    kr_ref[...] = kr_t.T.astype(BF16)

    kn_ref[...] = jnp.dot(kvn, wkn_ref[...], preferred_element_type=F32).astype(BF16)
    vt_ref[...] = _dot_nt(wvt_ref[...], kvn).astype(BF16)

    scale = QK_DIM ** -0.5 * math.log2(math.e)
    q_t = _dot_nt(wqt_ref[...], qn) * scale
    for h in range(MLA_HEADS):
        src = h * QK_DIM
        dst = h * HEAD_PAD
        x1 = q_t[src + QK_NOPE:src + QK_NOPE + half]
        x2 = q_t[src + QK_NOPE + half:src + QK_DIM]
        qt_ref[dst:dst + QK_NOPE, :] = q_t[src:src + QK_NOPE].astype(BF16)
        qt_ref[dst + QK_NOPE:dst + QK_NOPE + half, :] = (x1 * cos_t - x2 * sin_t).astype(BF16)
        qt_ref[dst + QK_NOPE + half:dst + QK_DIM, :] = (x1 * sin_t + x2 * cos_t).astype(BF16)
        qt_ref[dst + QK_DIM:dst + HEAD_PAD, :] = jnp.zeros((HEAD_PAD - QK_DIM, q_t.shape[1]), BF16)


def _mla_proj(x, positions, gain, w_in, gq, wqt, gkv, wkn, wvt):
    m, d = x.shape
    tm = MLA_P_TM
    r = ATT_T // tm
    nt = m // ATT_T
    half = QK_ROPE // 2
    inv_freq = 1.0 / (ROPE_THETA ** (jnp.arange(half, dtype=F32) / half))
    const = lambda i: (0, 0)

    def resident(shape):
        return pl.BlockSpec(shape, const, pipeline_mode=pl.Buffered(1))

    return pl.pallas_call(
        _mla_proj_kernel,
        grid=(m // tm,),
        in_specs=[
            pl.BlockSpec((tm, d), lambda i: (i, 0)),
            pl.BlockSpec((None, 1, tm), lambda i: (i, 0, 0)),
            resident((half, 1)),
            resident((1, d)),
            resident(w_in.shape),
            resident((1, Q_LORA)),
            resident(wqt.shape),
            resident((1, KV_LORA)),
            resident(wkn.shape),
            resident(wvt.shape),
        ],
        out_specs=[
            pl.BlockSpec((None, MLA_HEADS * HEAD_PAD, tm), lambda i: (i // r, 0, i % r)),
            pl.BlockSpec((tm, MLA_HEADS * QK_NOPE), lambda i: (i, 0)),
            pl.BlockSpec((tm, LANES), lambda i: (i, 0)),
            pl.BlockSpec((None, MLA_HEADS * V_DIM, tm), lambda i: (i // r, 0, i % r)),
        ],
        out_shape=[
            jax.ShapeDtypeStruct((nt, MLA_HEADS * HEAD_PAD, ATT_T), BF16),
            jax.ShapeDtypeStruct((m, MLA_HEADS * QK_NOPE), BF16),
            jax.ShapeDtypeStruct((m, LANES), BF16),
            jax.ShapeDtypeStruct((nt, MLA_HEADS * V_DIM, ATT_T), BF16),
        ],
        compiler_params=_params("parallel"),
        name="mla_proj",
    )(x, positions.reshape(m // tm, 1, tm), inv_freq.reshape(half, 1), gain.reshape(1, d),
      w_in, gq.reshape(1, -1), wqt, gkv.reshape(1, -1), wkn, wvt)


def _mla_attn_kernel(qt_ref, kn_ref, kr_ref, vt_ref, o_ref,
                     m_ref, acc_ref, s_ref, p_ref, alpha_ref):
    i = pl.program_id(2)
    t = qt_ref.shape[1]
    last = ATT_HEADS - 1

    def visible_only(s):
        key_chunk = lax.broadcasted_iota(jnp.int32, (t, 1), 0) // CHUNK
        qry_chunk = lax.broadcasted_iota(jnp.int32, (1, t), 1) // CHUNK
        return jnp.where(key_chunk <= qry_chunk, s, -jnp.inf)

    def scores(j, a):
        rows = pl.ds(pl.multiple_of(j * t, t), t)
        k = jnp.concatenate([kn_ref[rows, a * QK_NOPE:(a + 1) * QK_NOPE], kr_ref[rows, :]], axis=1)
        return jnp.dot(k, qt_ref[a * HEAD_PAD:(a + 1) * HEAD_PAD, :],
                       preferred_element_type=F32)

    ones_rows = jnp.ones((ATT_SUM_ROWS, t), BF16)

    def weighted_values(j, a, p, alpha):
        v_ext = jnp.concatenate([vt_ref[j, a * V_DIM:(a + 1) * V_DIM, :], ones_rows], axis=0)
        pv = jnp.dot(v_ext, p, preferred_element_type=F32)
        acc_ref[a] = alpha * acc_ref[a] + pv

    def softmax(a, s):
        m_prev = m_ref[a]
        m_new = jnp.maximum(m_prev, jnp.max(s, axis=0, keepdims=True))
        alpha = jnp.exp2(m_prev - m_new)
        p = jnp.exp2(s - m_new)
        m_ref[a] = m_new
        return p.astype(BF16), alpha

    def key_tile(j, diagonal, carry=None):
        s_cur, p_prev, alpha_prev = carry or (s_ref[...], p_ref[...], alpha_ref[...])
        for a in range(ATT_HEADS):
            if a < last:
                s_nxt = scores(j, a + 1)
                if diagonal:
                    s_nxt = visible_only(s_nxt)
            elif not diagonal:
                s_nxt = scores(j + 1, 0)
            weighted_values(jnp.maximum(j - 1, 0) if a == 0 else j, (a - 1) % ATT_HEADS,
                            p_prev, alpha_prev)
            p_prev, alpha_prev = softmax(a, s_cur)
            s_cur = s_nxt
        return s_nxt, p_prev, alpha_prev

    m_ref[...] = jnp.full(m_ref.shape, -jnp.inf, F32)
    acc_ref[...] = jnp.zeros(acc_ref.shape, F32)
    p_ref[...] = jnp.zeros(p_ref.shape, BF16)
    alpha_ref[...] = jnp.ones(alpha_ref.shape, F32)
    s_ref[...] = scores(0, 0)

    def save(carry):
        s_ref[...], p_ref[...], alpha_ref[...] = carry

    def run(first, count):
        carry = None
        for k in range(count):
            carry = key_tile(first + k, False, carry)
        save(carry)

    def unrolled(jj, carry):
        run(ATT_UNROLL * jj, ATT_UNROLL)
        return carry

    lax.fori_loop(0, i // ATT_UNROLL, unrolled, 0)
    count = ATT_UNROLL // 2
    while count:
        @pl.when(i % (2 * count) >= count)
        def _(count=count):
            run(i - i % (2 * count), count)
        count //= 2

    s_ref[...] = visible_only(s_ref[...])
    _, p, alpha = key_tile(i, diagonal=True)
    weighted_values(i, last, p, alpha)

    for a in range(ATT_HEADS):
        o = acc_ref[a, :V_DIM, :] / acc_ref[a, V_DIM:V_DIM + 1, :]
        o_ref[:, a * V_DIM:(a + 1) * V_DIM] = o.T.astype(o_ref.dtype)


def _mla_attn(qt, kn, kr, vt, batch):
    nt, _, t = qt.shape
    nq = nt // batch
    s = nq * t
    ha = ATT_HEADS
    return pl.pallas_call(
        _mla_attn_kernel,
        grid=(batch, MLA_HEADS // ha, nq),
        in_specs=[
            pl.BlockSpec((None, ha * HEAD_PAD, t), lambda b, h, i: (b * nq + i, h, 0)),
            pl.BlockSpec((None, s, ha * QK_NOPE), lambda b, h, i: (b, 0, h)),
            pl.BlockSpec((None, s, LANES), lambda b, h, i: (b, 0, 0)),
            pl.BlockSpec((None, nq, ha * V_DIM, t), lambda b, h, i: (b, 0, h, 0)),
        ],
        out_specs=pl.BlockSpec((None, t, ha * V_DIM), lambda b, h, i: (b, i, h)),
        out_shape=jax.ShapeDtypeStruct((batch, s, MLA_HEADS * V_DIM), BF16),
        scratch_shapes=[
            pltpu.VMEM((ha, 1, t), F32),
            pltpu.VMEM((ha, V_DIM + ATT_SUM_ROWS, t), F32),
            pltpu.VMEM((t, t), F32),
            pltpu.VMEM((t, t), BF16),
            pltpu.VMEM((1, t), F32),
        ],
        compiler_params=_params("parallel", "parallel", "arbitrary"),
        name="mla_attn",
    )(qt, kn.reshape(batch, s, -1), kr.reshape(batch, s, LANES),
      vt.reshape(batch, nq, MLA_HEADS * V_DIM, t))


def _mla_out_kernel(x_ref, o_ref, w_ref, y_ref):
    y_ref[...] = x_ref[...] + jnp.dot(o_ref[...], w_ref[...], preferred_element_type=F32)


def _mla_out(x, o, w):
    m, d = x.shape
    tm = MLA_O_TM
    return pl.pallas_call(
        _mla_out_kernel,
        grid=(m // tm,),
        in_specs=[
            pl.BlockSpec((tm, d), lambda i: (i, 0)),
            pl.BlockSpec((tm, o.shape[1]), lambda i: (i, 0)),
            pl.BlockSpec(w.shape, lambda i: (0, 0)),
        ],
        out_specs=pl.BlockSpec((tm, d), lambda i: (i, 0)),
        out_shape=jax.ShapeDtypeStruct((m, d), F32),
        compiler_params=_params("parallel"),
        name="mla_out",
    )(x, o, w)


def _mla(x, positions, gain, w_in, gq, w_q_up, gkv, w_kv_up, w_out):
    b, _ = positions.shape
    m, d = x.shape
    w_in_p = jnp.pad(w_in, ((0, 0), (0, LANES - QK_ROPE))).astype(BF16)
    wqt = w_q_up.T.astype(BF16)
    w_kv = w_kv_up.reshape(KV_LORA, MLA_HEADS, QK_NOPE + V_DIM)
    wkn = w_kv[:, :, :QK_NOPE].reshape(KV_LORA, -1).astype(BF16)
    wvt = w_kv[:, :, QK_NOPE:].reshape(KV_LORA, -1).T.astype(BF16)
    qt, kn, kr, vt = _mla_proj(x, positions, gain, w_in_p, gq, wqt, gkv, wkn, wvt)
    o = _mla_attn(qt, kn, kr, vt, b)
    return _mla_out(x, o.reshape(m, -1), w_out.astype(BF16))


def kernel(x, positions, ln_ffn1, ffn1_w_in, ffn1_w_out, ln_mix, ln_ffn2, ffn2_w_in, ffn2_w_out,
           sgu_w_in, sgu_v_gain, sgu_v_bias, sgu_w_spatial, sgu_b_spatial, sgu_w_out,
           mla_w_in, mla_q_norm, mla_w_q_up, mla_kv_norm, mla_w_kv_up, mla_w_out, ln_final):
    b, s, d = x.shape
    depth = ln_ffn1.shape[0]
    h = x.reshape(b * s, d)

    ffn_stacks = [((w_in, i, True), (w_out, i, False)) for i in range(depth)
                  for w_in, w_out in ((ffn1_w_in, ffn1_w_out), (ffn2_w_in, ffn2_w_out))]
    ffn_weights = {0: (_gate_up_layout(ffn1_w_in[0]).astype(BF16), ffn1_w_out[0].astype(BF16))}
    uncast = list(range(1, len(ffn_stacks)))

    def side_cast(call, *args, extra=(), **kwargs):
        target = uncast.pop(0) if uncast else None
        pairs = (ffn_stacks[target] if target is not None else ()) + tuple(extra)
        out, cast = call(*args, cast_next=pairs, **kwargs)
        if target is not None:
            ffn_weights[target], cast = cast[:2], cast[2:]
        return (out, cast) if extra else out

    for i in range(depth):
        j = i // 2
        if i % 2 == 0:
            h, sgu_w = side_cast(_ffn, h, ln_ffn1[i], *ffn_weights[2 * i],
                                 extra=((sgu_w_in, j, False), (sgu_w_out, j, False)))
            h, _ = _sgu(h, ln_mix[i], sgu_w[0], sgu_v_gain[j], sgu_v_bias[j],
                        sgu_w_spatial[j], sgu_b_spatial[j], sgu_w[1])
        else:
            h = side_cast(_ffn, h, ln_ffn1[i], *ffn_weights[2 * i])
            h = _mla(h, positions, ln_mix[i], mla_w_in[j], mla_q_norm[j], mla_w_q_up[j],
                     mla_kv_norm[j], mla_w_kv_up[j], mla_w_out[j])
        last = i == depth - 1
        h = side_cast(_ffn, h, ln_ffn2[i], *ffn_weights[2 * i + 1],
                      final_gain=ln_final if last else None)
    return h.reshape(b, s, d)
```

```python
import functools
import math

import jax
import jax.numpy as jnp
from jax import lax
from jax.experimental import pallas as pl
from jax.experimental.pallas import tpu as pltpu

F32 = jnp.float32
BF16 = jnp.bfloat16

EPS = 1e-6
CHUNK = 64
SGU_BLOCK = 128
SGU_GROUPS = 8
MLA_HEADS = 16
Q_LORA = 512
KV_LORA = 512
QK_NOPE = 128
QK_ROPE = 64
V_DIM = 128
QK_DIM = QK_NOPE + QK_ROPE
ROPE_THETA = 10000.0

LANES = 128
BF16_SUBLANES = 16
HEAD_PAD = 2 * LANES
VMEM_LIMIT = 61 * 1024 * 1024

FFN_TM = 1024
FFN_TF = 512
FFN_SUB = 256
FFN_ROWS = 64
FFN_COLS = 512
SGU_IN_TM = 256
SGU_OUT_TM = 256
SGU_COLS = 1024
SGU_OUT_COLS = 512
MLA_P_TM = 512
MLA_O_TM = 512
ATT_T = 512
ATT_HEADS = 4
ATT_UNROLL = 4
ATT_SUM_ROWS = 16


def _rms_normalize(x, gain):
    return x * lax.rsqrt(jnp.mean(x * x, axis=-1, keepdims=True) + EPS) * gain


def _params(*sem):
    return pltpu.CompilerParams(dimension_semantics=sem, vmem_limit_bytes=VMEM_LIMIT)


def _dot_nt(a, b):
    return lax.dot_general(a, b, (((1,), (1,)), ((), ())), preferred_element_type=F32)


def _cast_plan(stacked, layer, grid):
    _, rows, cols = stacked.shape
    steps = math.prod(grid)

    def split(n):
        for nb in range(1, n + 1):
            na = n // nb
            if (na * nb == n and rows % na == 0 and cols % nb == 0
                    and (rows // na) % BF16_SUBLANES == 0 and (cols // nb) % LANES == 0):
                return na, nb
        return None

    used = next(n for n in range(steps, 0, -1) if split(n))
    na, nb = split(used)
    block = (rows // na, cols // nb)

    def index(*ids):
        flat = ids[0]
        for extent, idx in zip(grid[1:], ids[1:]):
            flat = flat * extent + idx
        if used < steps:
            flat = jnp.minimum(flat, used - 1)
        return flat // nb, flat % nb

    src = pl.BlockSpec((None,) + block, lambda *ids: (layer,) + index(*ids))
    dst = pl.BlockSpec(block, index)
    return src, dst, jax.ShapeDtypeStruct((rows, cols), BF16)


def _cast_blocks(src_refs, dst_refs):
    for src, dst in zip(src_refs, dst_refs):
        dst[...] = src[...].astype(BF16)


def _ffn_kernel(x_ref, g_ref, *rest, final_norm, n_cast):
    rest = list(rest)
    gate_up_refs = [(rest.pop(0), rest.pop(0)) for _ in range(FFN_TF // FFN_SUB)]
    wo_ref = rest.pop(0)
    gf_ref = rest.pop(0) if final_norm else None
    cast_src = [rest.pop(0) for _ in range(n_cast)]
    o_ref = rest.pop(0)
    cast_dst = [rest.pop(0) for _ in range(n_cast)]
    (xn_ref,) = rest
    j = pl.program_id(1)
    _cast_blocks(cast_src, cast_dst)

    tm, d = x_ref.shape
    row_chunks = [slice(r, r + FFN_ROWS) for r in range(0, tm, FFN_ROWS)]

    @pl.when(j == 0)
    def _():
        for rows in row_chunks:
            x = x_ref[rows, :]
            xn_ref[rows, :] = _rms_normalize(x, g_ref[...]).astype(BF16)
            o_ref[rows, :] = x

    xn = xn_ref[...]
    acts = []
    for wg_ref, wu_ref in gate_up_refs:
        gate = jnp.dot(xn, wg_ref[...], preferred_element_type=F32)
        up = jnp.dot(xn, wu_ref[...], preferred_element_type=F32)
        acts.append((0.5 * gate * jax.nn.sigmoid(gate) * up).astype(BF16))
    act = jnp.concatenate(acts, axis=1)
    for c in range(0, d, FFN_COLS):
        o_ref[:, c:c + FFN_COLS] += jnp.dot(act, wo_ref[:, c:c + FFN_COLS],
                                            preferred_element_type=F32)

    if final_norm:
        @pl.when(j == pl.num_programs(1) - 1)
        def _():
            for rows in row_chunks:
                y = o_ref[rows, :]
                rstd = lax.rsqrt(jnp.mean(y * y, axis=-1, keepdims=True) + EPS)
                o_ref[rows, :] = y * rstd * gf_ref[...]


def _ffn(x, gain, w_in, w_out, final_gain=None, cast_next=()):
    m, d = x.shape
    f = w_out.shape[0]
    nf = f // FFN_TF
    grid = (m // FFN_TM, nf)
    final_norm = final_gain is not None
    in_specs = [
        pl.BlockSpec((FFN_TM, d), lambda i, j: (i, 0)),
        pl.BlockSpec((1, d), lambda i, j: (0, 0)),
    ]
    subs = FFN_TF // FFN_SUB
    for sub in range(subs):
        for half in range(2):
            in_specs.append(pl.BlockSpec(
                (d, FFN_SUB), lambda i, j, sub=sub, half=half: (0, (half * nf + j) * subs + sub)))
    in_specs.append(pl.BlockSpec((FFN_TF, d), lambda i, j: (j, 0)))
    args = [x, gain.reshape(1, d)] + [w_in] * (2 * subs) + [w_out]
    if final_norm:
        in_specs.append(pl.BlockSpec((1, d), lambda i, j: (0, 0)))
        args.append(final_gain.reshape(1, d))
    out_specs = [pl.BlockSpec((FFN_TM, d), lambda i, j: (i, 0))]
    out_shape = [jax.ShapeDtypeStruct((m, d), F32)]
    stacks = [stacked for stacked, _ in cast_next]
    for stacked, layer in cast_next:
        src, dst, shape = _cast_plan(stacked, layer, grid)
        in_specs.append(src)
        args.append(stacked)
        out_specs.append(dst)
        out_shape.append(shape)
    out = pl.pallas_call(
        functools.partial(_ffn_kernel, final_norm=final_norm, n_cast=len(stacks)),
        grid=grid,
        in_specs=in_specs,
        out_specs=out_specs,
        out_shape=out_shape,
        scratch_shapes=[pltpu.VMEM((FFN_TM, d), BF16)],
        compiler_params=_params("parallel", "arbitrary"),
        name="ffn_final" if final_norm else "ffn",
    )(*args)
    return out[0], tuple(out[1:])


def _resident(shape):
    return pl.BlockSpec(shape, lambda i: (0,) * len(shape), pipeline_mode=pl.Buffered(1))


def _sgu_in_kernel(x_ref, g_ref, w_ref, vg_ref, vb_ref, *rest, n_cast):
    cast_src, (u_ref, vn_ref, *rest) = rest[:n_cast], rest[n_cast:]
    cast_dst, (xn_ref, v_ref) = rest[:n_cast], rest[n_cast:]
    _cast_blocks(cast_src, cast_dst)
    tm, width = u_ref.shape
    chunks = [slice(c, c + SGU_COLS) for c in range(0, width, SGU_COLS)]
    tiles = [slice(c, c + LANES) for c in range(0, width, LANES)]

    def project(cols):
        return jnp.dot(xn_ref[...], w_ref[:, cols], preferred_element_type=F32)

    xn_ref[...] = _rms_normalize(x_ref[...], g_ref[...]).astype(BF16)

    def mean_pass():
        total = jnp.zeros((tm, LANES), F32)
        for t in tiles:
            total += v_ref[:, t]
        return jnp.broadcast_to(jnp.sum(total, axis=-1, keepdims=True) / width, (tm, LANES))

    def rstd_pass(mu):
        sq = jnp.zeros((tm, LANES), F32)
        for t in tiles:
            dv = v_ref[:, t] - mu
            sq += dv * dv
        var = jnp.sum(sq, axis=-1, keepdims=True) / width
        return jnp.broadcast_to(lax.rsqrt(var + EPS), (tm, LANES))

    def normalize(mu, rstd):
        for t in tiles:
            vn_ref[:, t] = ((v_ref[:, t] - mu) * rstd * vg_ref[:, t] + vb_ref[:, t]).astype(BF16)

    for cols in chunks:
        v_ref[:, cols] = jax.nn.gelu(project(slice(width + cols.start, width + cols.stop)))
    stats = {}
    side_work = [lambda: stats.update(mu=mean_pass()),
                 lambda: stats.update(rstd=rstd_pass(stats["mu"])),
                 lambda: normalize(stats["mu"], stats["rstd"])]
    assert len(chunks) >= len(side_work)
    for k, cols in enumerate(chunks):
        u = project(cols)
        if k < len(side_work):
            side_work[k]()
        u_ref[:, cols] = jax.nn.gelu(u).astype(BF16)


def _sgu_out_kernel(x_ref, u_ref, vn_ref, ws_ref, bs_ref, wo_ref, *rest, n_cast):
    cast_src, (o_ref, *rest) = rest[:n_cast], rest[n_cast:]
    cast_dst, (gated_ref,) = rest[:n_cast], rest[n_cast:]
    _cast_blocks(cast_src, cast_dst)
    tm, width = u_ref.shape
    d = o_ref.shape[1]
    gw = width // SGU_GROUPS
    row_chunk = lax.broadcasted_iota(jnp.int32, (SGU_BLOCK, SGU_BLOCK), 0) // CHUNK
    col_chunk = lax.broadcasted_iota(jnp.int32, (SGU_BLOCK, SGU_BLOCK), 1) // CHUNK
    visible = row_chunk >= col_chunk
    for g in range(SGU_GROUPS):
        lanes = slice(g * gw, (g + 1) * gw)
        ws = jnp.where(visible, ws_ref[g], 0.0).astype(BF16)
        bias = jnp.tile(bs_ref[g], (1, gw // LANES))
        for b in range(tm // SGU_BLOCK):
            rows = slice(b * SGU_BLOCK, (b + 1) * SGU_BLOCK)
            mixed = jnp.dot(ws, vn_ref[rows, lanes], preferred_element_type=F32) + bias
            gated_ref[rows, lanes] = (u_ref[rows, lanes].astype(F32) * mixed).astype(BF16)
    gated = gated_ref[...]
    for c in range(0, d, SGU_OUT_COLS):
        cols = slice(c, c + SGU_OUT_COLS)
        o_ref[:, cols] = x_ref[:, cols] + jnp.dot(gated, wo_ref[:, cols],
                                                  preferred_element_type=F32)


def _sgu(x, gain, w_in, v_gain, v_bias, w_spatial, b_spatial, w_out, cast_next=()):
    m, d = x.shape
    width = w_out.shape[0]
    rows = lambda tm: (lambda i: (i, 0))

    def with_cast(in_specs, args, out_specs, out_shape, pairs, n_tiles):
        for stacked, layer in pairs:
            src, dst, shape = _cast_plan(stacked, layer, (n_tiles,))
            in_specs.append(src)
            args.append(stacked)
            out_specs.append(dst)
            out_shape.append(shape)
        return len(pairs)

    tm = SGU_IN_TM
    in_specs = [pl.BlockSpec((tm, d), rows(tm)), _resident((1, d)), _resident(w_in.shape),
                _resident((1, width)), _resident((1, width))]
    args = [x, gain.reshape(1, d), w_in, v_gain.reshape(1, width), v_bias.reshape(1, width)]
    out_specs = [pl.BlockSpec((tm, width), rows(tm)), pl.BlockSpec((tm, width), rows(tm))]
    out_shape = [jax.ShapeDtypeStruct((m, width), BF16), jax.ShapeDtypeStruct((m, width), BF16)]
    n_cast = with_cast(in_specs, args, out_specs, out_shape, cast_next[:1], m // tm)
    u, vn, *cast_in = pl.pallas_call(
        functools.partial(_sgu_in_kernel, n_cast=n_cast),
        grid=(m // tm,),
        in_specs=in_specs,
        out_specs=out_specs,
        out_shape=out_shape,
        scratch_shapes=[pltpu.VMEM((tm, d), BF16), pltpu.VMEM((tm, width), F32)],
        compiler_params=_params("parallel"),
        name="sgu_in",
    )(*args)

    tm = SGU_OUT_TM
    bias = jnp.broadcast_to(b_spatial[:, :, None], (SGU_GROUPS, SGU_BLOCK, LANES))
    in_specs = [pl.BlockSpec((tm, d), rows(tm)), pl.BlockSpec((tm, width), rows(tm)),
                pl.BlockSpec((tm, width), rows(tm)), _resident(w_spatial.shape),
                _resident(bias.shape), _resident(w_out.shape)]
    args = [x, u, vn, w_spatial, bias, w_out]
    out_specs = [pl.BlockSpec((tm, d), rows(tm))]
    out_shape = [jax.ShapeDtypeStruct((m, d), F32)]
    n_cast = with_cast(in_specs, args, out_specs, out_shape, cast_next[1:], m // tm)
    y, *cast_out = pl.pallas_call(
        functools.partial(_sgu_out_kernel, n_cast=n_cast),
        grid=(m // tm,),
        in_specs=in_specs,
        out_specs=out_specs,
        out_shape=out_shape,
        scratch_shapes=[pltpu.VMEM((tm, width), BF16)],
        compiler_params=_params("parallel"),
        name="sgu_out",
    )(*args)
    return y, tuple(cast_in + cast_out)


def _mla_proj_kernel(x_ref, posr_ref, freqc_ref, g_ref, wi_ref, gq_ref,
                     wqt_ref, gkv_ref, wkn_ref, wvt_ref, qt_ref, kn_ref, kr_ref, vt_ref):
    half = QK_ROPE // 2
    hn = _rms_normalize(x_ref[...], g_ref[...]).astype(BF16)
    proj = jnp.dot(hn, wi_ref[...], preferred_element_type=F32)
    qn = _rms_normalize(proj[:, :Q_LORA], gq_ref[...]).astype(BF16)
    kvn = _rms_normalize(proj[:, Q_LORA:Q_LORA + KV_LORA], gkv_ref[...]).astype(BF16)
    kr = proj[:, Q_LORA + KV_LORA:]

    ang_t = freqc_ref[...] * posr_ref[...].astype(F32)
    cos_t = jnp.cos(ang_t)
    sin_t = jnp.sin(ang_t)

    kr_t = kr.T
    k1, k2 = kr_t[:half], kr_t[half:QK_ROPE]
    kr_t = jnp.concatenate([k1 * cos_t - k2 * sin_t, k1 * sin_t + k2 * cos_t, kr_t[QK_ROPE:]],
                           axis=0)
    kr_ref[...] = kr_t.T.astype(BF16)

    kn_ref[...] = jnp.dot(kvn, wkn_ref[...], preferred_element_type=F32).astype(BF16)
    vt_ref[...] = _dot_nt(wvt_ref[...], kvn).astype(BF16)

    scale = QK_DIM ** -0.5 * math.log2(math.e)
    q_t = _dot_nt(wqt_ref[...], qn) * scale
    for h in range(MLA_HEADS):
        src = h * QK_DIM
        dst = h * HEAD_PAD
        x1 = q_t[src + QK_NOPE:src + QK_NOPE + half]
        x2 = q_t[src + QK_NOPE + half:src + QK_DIM]
        qt_ref[dst:dst + QK_NOPE, :] = q_t[src:src + QK_NOPE].astype(BF16)
        qt_ref[dst + QK_NOPE:dst + QK_NOPE + half, :] = (x1 * cos_t - x2 * sin_t).astype(BF16)
        qt_ref[dst + QK_NOPE + half:dst + QK_DIM, :] = (x1 * sin_t + x2 * cos_t).astype(BF16)
        qt_ref[dst + QK_DIM:dst + HEAD_PAD, :] = jnp.zeros((HEAD_PAD - QK_DIM, q_t.shape[1]), BF16)


def _mla_proj(x, positions, gain, w_in, gq, wqt, gkv, wkn, wvt):
    m, d = x.shape
    tm = MLA_P_TM
    r = ATT_T // tm
    nt = m // ATT_T
    half = QK_ROPE // 2
    inv_freq = 1.0 / (ROPE_THETA ** (jnp.arange(half, dtype=F32) / half))
    const = lambda i: (0, 0)

    def resident(shape):
        return pl.BlockSpec(shape, const, pipeline_mode=pl.Buffered(1))

    return pl.pallas_call(
        _mla_proj_kernel,
        grid=(m // tm,),
        in_specs=[
            pl.BlockSpec((tm, d), lambda i: (i, 0)),
            pl.BlockSpec((None, 1, tm), lambda i: (i, 0, 0)),
            resident((half, 1)),
            resident((1, d)),
            resident(w_in.shape),
            resident((1, Q_LORA)),
            resident(wqt.shape),
            resident((1, KV_LORA)),
            resident(wkn.shape),
            resident(wvt.shape),
        ],
        out_specs=[
            pl.BlockSpec((None, MLA_HEADS * HEAD_PAD, tm), lambda i: (i // r, 0, i % r)),
            pl.BlockSpec((tm, MLA_HEADS * QK_NOPE), lambda i: (i, 0)),
            pl.BlockSpec((tm, LANES), lambda i: (i, 0)),
            pl.BlockSpec((None, MLA_HEADS * V_DIM, tm), lambda i: (i // r, 0, i % r)),
        ],
        out_shape=[
            jax.ShapeDtypeStruct((nt, MLA_HEADS * HEAD_PAD, ATT_T), BF16),
            jax.ShapeDtypeStruct((m, MLA_HEADS * QK_NOPE), BF16),
            jax.ShapeDtypeStruct((m, LANES), BF16),
            jax.ShapeDtypeStruct((nt, MLA_HEADS * V_DIM, ATT_T), BF16),
        ],
        compiler_params=_params("parallel"),
        name="mla_proj",
    )(x, positions.reshape(m // tm, 1, tm), inv_freq.reshape(half, 1), gain.reshape(1, d),
      w_in, gq.reshape(1, -1), wqt, gkv.reshape(1, -1), wkn, wvt)


def _mla_attn_kernel(qt_ref, kn_ref, kr_ref, vt_ref, o_ref,
                     m_ref, acc_ref, s_ref, p_ref, alpha_ref):
    i = pl.program_id(2)
    t = qt_ref.shape[1]
    last = ATT_HEADS - 1

    def visible_only(s):
        key_chunk = lax.broadcasted_iota(jnp.int32, (t, 1), 0) // CHUNK
        qry_chunk = lax.broadcasted_iota(jnp.int32, (1, t), 1) // CHUNK
        return jnp.where(key_chunk <= qry_chunk, s, -jnp.inf)

    def scores(j, a):
        rows = pl.ds(pl.multiple_of(j * t, t), t)
        k = jnp.concatenate([kn_ref[rows, a * QK_NOPE:(a + 1) * QK_NOPE], kr_ref[rows, :]], axis=1)
        return jnp.dot(k, qt_ref[a * HEAD_PAD:(a + 1) * HEAD_PAD, :],
                       preferred_element_type=F32)

    ones_rows = jnp.ones((ATT_SUM_ROWS, t), BF16)

    def weighted_values(j, a, p, alpha):
        v_ext = jnp.concatenate([vt_ref[j, a * V_DIM:(a + 1) * V_DIM, :], ones_rows], axis=0)
        pv = jnp.dot(v_ext, p, preferred_element_type=F32)
        acc_ref[a] = alpha * acc_ref[a] + pv

    def softmax(a, s):
        m_prev = m_ref[a]
        m_new = jnp.maximum(m_prev, jnp.max(s, axis=0, keepdims=True))
        alpha = jnp.exp2(m_prev - m_new)
        p = jnp.exp2(s - m_new)
        m_ref[a] = m_new
        return p.astype(BF16), alpha

    def key_tile(j, diagonal, carry=None):
        s_cur, p_prev, alpha_prev = carry or (s_ref[...], p_ref[...], alpha_ref[...])
        for a in range(ATT_HEADS):
            if a < last:
                s_nxt = scores(j, a + 1)
                if diagonal:
                    s_nxt = visible_only(s_nxt)
            elif not diagonal:
                s_nxt = scores(j + 1, 0)
            weighted_values(jnp.maximum(j - 1, 0) if a == 0 else j, (a - 1) % ATT_HEADS,
                            p_prev, alpha_prev)
            p_prev, alpha_prev = softmax(a, s_cur)
            s_cur = s_nxt
        return s_nxt, p_prev, alpha_prev

    m_ref[...] = jnp.full(m_ref.shape, -jnp.inf, F32)
    acc_ref[...] = jnp.zeros(acc_ref.shape, F32)
    p_ref[...] = jnp.zeros(p_ref.shape, BF16)
    alpha_ref[...] = jnp.ones(alpha_ref.shape, F32)
    s_ref[...] = scores(0, 0)

    def save(carry):
        s_ref[...], p_ref[...], alpha_ref[...] = carry

    def run(first, count):
        carry = None
        for k in range(count):
            carry = key_tile(first + k, False, carry)
        save(carry)

    def unrolled(jj, carry):
        run(ATT_UNROLL * jj, ATT_UNROLL)
        return carry

    lax.fori_loop(0, i // ATT_UNROLL, unrolled, 0)
    count = ATT_UNROLL // 2
    while count:
        @pl.when(i % (2 * count) >= count)
        def _(count=count):
            run(i - i % (2 * count), count)
        count //= 2

    s_ref[...] = visible_only(s_ref[...])
    _, p, alpha = key_tile(i, diagonal=True)
    weighted_values(i, last, p, alpha)

    for a in range(ATT_HEADS):
        o = acc_ref[a, :V_DIM, :] / acc_ref[a, V_DIM:V_DIM + 1, :]
        o_ref[:, a * V_DIM:(a + 1) * V_DIM] = o.T.astype(o_ref.dtype)


def _mla_attn(qt, kn, kr, vt, batch):
    nt, _, t = qt.shape
    nq = nt // batch
    s = nq * t
    ha = ATT_HEADS
    return pl.pallas_call(
        _mla_attn_kernel,
        grid=(batch, MLA_HEADS // ha, nq),
        in_specs=[
            pl.BlockSpec((None, ha * HEAD_PAD, t), lambda b, h, i: (b * nq + i, h, 0)),
            pl.BlockSpec((None, s, ha * QK_NOPE), lambda b, h, i: (b, 0, h)),
            pl.BlockSpec((None, s, LANES), lambda b, h, i: (b, 0, 0)),
            pl.BlockSpec((None, nq, ha * V_DIM, t), lambda b, h, i: (b, 0, h, 0)),
        ],
        out_specs=pl.BlockSpec((None, t, ha * V_DIM), lambda b, h, i: (b, i, h)),
        out_shape=jax.ShapeDtypeStruct((batch, s, MLA_HEADS * V_DIM), BF16),
        scratch_shapes=[
            pltpu.VMEM((ha, 1, t), F32),
            pltpu.VMEM((ha, V_DIM + ATT_SUM_ROWS, t), F32),
            pltpu.VMEM((t, t), F32),
            pltpu.VMEM((t, t), BF16),
            pltpu.VMEM((1, t), F32),
        ],
        compiler_params=_params("parallel", "parallel", "arbitrary"),
        name="mla_attn",
    )(qt, kn.reshape(batch, s, -1), kr.reshape(batch, s, LANES),
      vt.reshape(batch, nq, MLA_HEADS * V_DIM, t))


def _mla_out_kernel(x_ref, o_ref, w_ref, y_ref):
    y_ref[...] = x_ref[...] + jnp.dot(o_ref[...], w_ref[...], preferred_element_type=F32)


def _mla_out(x, o, w):
    m, d = x.shape
    tm = MLA_O_TM
    return pl.pallas_call(
        _mla_out_kernel,
        grid=(m // tm,),
        in_specs=[
            pl.BlockSpec((tm, d), lambda i: (i, 0)),
            pl.BlockSpec((tm, o.shape[1]), lambda i: (i, 0)),
            pl.BlockSpec(w.shape, lambda i: (0, 0)),
        ],
        out_specs=pl.BlockSpec((tm, d), lambda i: (i, 0)),
        out_shape=jax.ShapeDtypeStruct((m, d), F32),
        compiler_params=_params("parallel"),
        name="mla_out",
    )(x, o, w)


def _mla(x, positions, gain, w_in, gq, w_q_up, gkv, w_kv_up, w_out):
    b, _ = positions.shape
    m, d = x.shape
    w_in_p = jnp.pad(w_in, ((0, 0), (0, LANES - QK_ROPE))).astype(BF16)
    wqt = w_q_up.T.astype(BF16)
    w_kv = w_kv_up.reshape(KV_LORA, MLA_HEADS, QK_NOPE + V_DIM)
    wkn = w_kv[:, :, :QK_NOPE].reshape(KV_LORA, -1).astype(BF16)
    wvt = w_kv[:, :, QK_NOPE:].reshape(KV_LORA, -1).T.astype(BF16)
    qt, kn, kr, vt = _mla_proj(x, positions, gain, w_in_p, gq, wqt, gkv, wkn, wvt)
    o = _mla_attn(qt, kn, kr, vt, b)
    return _mla_out(x, o.reshape(m, -1), w_out.astype(BF16))


def kernel(x, positions, ln_ffn1, ffn1_w_in, ffn1_w_out, ln_mix, ln_ffn2, ffn2_w_in, ffn2_w_out,
           sgu_w_in, sgu_v_gain, sgu_v_bias, sgu_w_spatial, sgu_b_spatial, sgu_w_out,
           mla_w_in, mla_q_norm, mla_w_q_up, mla_kv_norm, mla_w_kv_up, mla_w_out, ln_final):
    b, s, d = x.shape
    depth = ln_ffn1.shape[0]
    h = x.reshape(b * s, d)

    ffn_stacks = [((w_in, i), (w_out, i)) for i in range(depth)
                  for w_in, w_out in ((ffn1_w_in, ffn1_w_out), (ffn2_w_in, ffn2_w_out))]
    ffn_weights = {0: (ffn1_w_in[0].astype(BF16), ffn1_w_out[0].astype(BF16))}
    uncast = list(range(1, len(ffn_stacks)))

    def side_cast(call, *args, extra=(), **kwargs):
        target = uncast.pop(0) if uncast else None
        pairs = (ffn_stacks[target] if target is not None else ()) + tuple(extra)
        out, cast = call(*args, cast_next=pairs, **kwargs)
        if target is not None:
            ffn_weights[target], cast = cast[:2], cast[2:]
        return (out, cast) if extra else out

    for i in range(depth):
        j = i // 2
        if i % 2 == 0:
            h, sgu_w = side_cast(_ffn, h, ln_ffn1[i], *ffn_weights[2 * i],
                                 extra=((sgu_w_in, j), (sgu_w_out, j)))
            h = side_cast(_sgu, h, ln_mix[i], sgu_w[0], sgu_v_gain[j], sgu_v_bias[j],
                          sgu_w_spatial[j], sgu_b_spatial[j], sgu_w[1])
        else:
            h = side_cast(_ffn, h, ln_ffn1[i], *ffn_weights[2 * i])
            h = _mla(h, positions, ln_mix[i], mla_w_in[j], mla_q_norm[j], mla_w_q_up[j],
                     mla_kv_norm[j], mla_w_kv_up[j], mla_w_out[j])
        last = i == depth - 1
        h = side_cast(_ffn, h, ln_ffn2[i], *ffn_weights[2 * i + 1],
                      final_gain=ln_final if last else None)
    return h.reshape(b, s, d)
```

```python
import functools
import math

import jax
import jax.numpy as jnp
from jax import lax
from jax.experimental import pallas as pl
from jax.experimental.pallas import tpu as pltpu

F32 = jnp.float32
BF16 = jnp.bfloat16

EPS = 1e-6
CHUNK = 64
SGU_BLOCK = 128
SGU_GROUPS = 8
MLA_HEADS = 16
Q_LORA = 512
KV_LORA = 512
QK_NOPE = 128
QK_ROPE = 64
V_DIM = 128
QK_DIM = QK_NOPE + QK_ROPE
ROPE_THETA = 10000.0

LANES = 128
BF16_SUBLANES = 16
HEAD_PAD = 2 * LANES
VMEM_LIMIT = 61 * 1024 * 1024

FFN_TM = 1024
FFN_TF = 512
FFN_SUB = 256
FFN_ROWS = 64
FFN_COLS = 512
SGU_IN_TM = 256
SGU_OUT_TM = 256
SGU_COLS = 512
SGU_OUT_COLS = 512
MLA_P_TM = 512
MLA_O_TM = 512
ATT_T = 512
ATT_HEADS = 4
ATT_UNROLL = 4
ATT_SUM_ROWS = 16


def _rms_normalize(x, gain):
    return x * lax.rsqrt(jnp.mean(x * x, axis=-1, keepdims=True) + EPS) * gain


def _params(*sem):
    return pltpu.CompilerParams(dimension_semantics=sem, vmem_limit_bytes=VMEM_LIMIT)


def _dot_nt(a, b):
    return lax.dot_general(a, b, (((1,), (1,)), ((), ())), preferred_element_type=F32)


def _cast_plan(stacked, layer, grid):
    _, rows, cols = stacked.shape
    steps = math.prod(grid)

    def split(n):
        for nb in range(1, n + 1):
            na = n // nb
            if (na * nb == n and rows % na == 0 and cols % nb == 0
                    and (rows // na) % BF16_SUBLANES == 0 and (cols // nb) % LANES == 0):
                return na, nb
        return None

    used = next(n for n in range(steps, 0, -1) if split(n))
    na, nb = split(used)
    block = (rows // na, cols // nb)

    def index(*ids):
        flat = ids[0]
        for extent, idx in zip(grid[1:], ids[1:]):
            flat = flat * extent + idx
        if used < steps:
            flat = jnp.minimum(flat, used - 1)
        return flat // nb, flat % nb

    src = pl.BlockSpec((None,) + block, lambda *ids: (layer,) + index(*ids))
    dst = pl.BlockSpec(block, index)
    return src, dst, jax.ShapeDtypeStruct((rows, cols), BF16)


def _cast_blocks(src_refs, dst_refs):
    for src, dst in zip(src_refs, dst_refs):
        dst[...] = src[...].astype(BF16)


def _ffn_kernel(x_ref, g_ref, *rest, final_norm, n_cast):
    rest = list(rest)
    gate_up_refs = [(rest.pop(0), rest.pop(0)) for _ in range(FFN_TF // FFN_SUB)]
    wo_ref = rest.pop(0)
    gf_ref = rest.pop(0) if final_norm else None
    cast_src = [rest.pop(0) for _ in range(n_cast)]
    o_ref = rest.pop(0)
    cast_dst = [rest.pop(0) for _ in range(n_cast)]
    (xn_ref,) = rest
    j = pl.program_id(1)
    _cast_blocks(cast_src, cast_dst)

    tm, d = x_ref.shape
    row_chunks = [slice(r, r + FFN_ROWS) for r in range(0, tm, FFN_ROWS)]

    @pl.when(j == 0)
    def _():
        for rows in row_chunks:
            x = x_ref[rows, :]
            xn_ref[rows, :] = _rms_normalize(x, g_ref[...]).astype(BF16)
            o_ref[rows, :] = x

    xn = xn_ref[...]
    acts = []
    for wg_ref, wu_ref in gate_up_refs:
        gate = jnp.dot(xn, wg_ref[...], preferred_element_type=F32)
        up = jnp.dot(xn, wu_ref[...], preferred_element_type=F32)
        acts.append((0.5 * gate * jax.nn.sigmoid(gate) * up).astype(BF16))
    act = jnp.concatenate(acts, axis=1)
    for c in range(0, d, FFN_COLS):
        o_ref[:, c:c + FFN_COLS] += jnp.dot(act, wo_ref[:, c:c + FFN_COLS],
                                            preferred_element_type=F32)

    if final_norm:
        @pl.when(j == pl.num_programs(1) - 1)
        def _():
            for rows in row_chunks:
                y = o_ref[rows, :]
                rstd = lax.rsqrt(jnp.mean(y * y, axis=-1, keepdims=True) + EPS)
                o_ref[rows, :] = y * rstd * gf_ref[...]


def _ffn(x, gain, w_in, w_out, final_gain=None, cast_next=()):
    m, d = x.shape
    f = w_out.shape[0]
    nf = f // FFN_TF
    grid = (m // FFN_TM, nf)
    final_norm = final_gain is not None
    in_specs = [
        pl.BlockSpec((FFN_TM, d), lambda i, j: (i, 0)),
        pl.BlockSpec((1, d), lambda i, j: (0, 0)),
    ]
    subs = FFN_TF // FFN_SUB
    for sub in range(subs):
        for half in range(2):
            in_specs.append(pl.BlockSpec(
                (d, FFN_SUB), lambda i, j, sub=sub, half=half: (0, (half * nf + j) * subs + sub)))
    in_specs.append(pl.BlockSpec((FFN_TF, d), lambda i, j: (j, 0)))
    args = [x, gain.reshape(1, d)] + [w_in] * (2 * subs) + [w_out]
    if final_norm:
        in_specs.append(pl.BlockSpec((1, d), lambda i, j: (0, 0)))
        args.append(final_gain.reshape(1, d))
    out_specs = [pl.BlockSpec((FFN_TM, d), lambda i, j: (i, 0))]
    out_shape = [jax.ShapeDtypeStruct((m, d), F32)]
    stacks = [stacked for stacked, _ in cast_next]
    for stacked, layer in cast_next:
        src, dst, shape = _cast_plan(stacked, layer, grid)
        in_specs.append(src)
        args.append(stacked)
        out_specs.append(dst)
        out_shape.append(shape)
    out = pl.pallas_call(
        functools.partial(_ffn_kernel, final_norm=final_norm, n_cast=len(stacks)),
        grid=grid,
        in_specs=in_specs,
        out_specs=out_specs,
        out_shape=out_shape,
        scratch_shapes=[pltpu.VMEM((FFN_TM, d), BF16)],
        compiler_params=_params("parallel", "arbitrary"),
        name="ffn_final" if final_norm else "ffn",
    )(*args)
    return out[0], tuple(out[1:])


def _resident(shape):
    return pl.BlockSpec(shape, lambda i: (0,) * len(shape), pipeline_mode=pl.Buffered(1))


def _sgu_in_kernel(x_ref, g_ref, *rest, n_cast, n_blocks):
    w_refs, (vg_ref, vb_ref, *rest) = rest[:n_blocks], rest[n_blocks:]
    cast_src, (u_ref, vn_ref, *rest) = rest[:n_cast], rest[n_cast:]
    cast_dst, (xn_ref, v_ref) = rest[:n_cast], rest[n_cast:]
    _cast_blocks(cast_src, cast_dst)
    tm, width = u_ref.shape
    chunks = [slice(c, c + SGU_COLS) for c in range(0, width, SGU_COLS)]
    tiles = [slice(c, c + LANES) for c in range(0, width, LANES)]

    def project(cols):
        return jnp.dot(xn_ref[...], w_refs[cols.start // SGU_COLS][...],
                       preferred_element_type=F32)

    xn_ref[...] = _rms_normalize(x_ref[...], g_ref[...]).astype(BF16)

    def mean_pass():
        total = jnp.zeros((tm, LANES), F32)
        for t in tiles:
            total += v_ref[:, t]
        return jnp.broadcast_to(jnp.sum(total, axis=-1, keepdims=True) / width, (tm, LANES))

    def rstd_pass(mu):
        sq = jnp.zeros((tm, LANES), F32)
        for t in tiles:
            dv = v_ref[:, t] - mu
            sq += dv * dv
        var = jnp.sum(sq, axis=-1, keepdims=True) / width
        return jnp.broadcast_to(lax.rsqrt(var + EPS), (tm, LANES))

    def normalize(mu, rstd):
        for t in tiles:
            vn_ref[:, t] = ((v_ref[:, t] - mu) * rstd * vg_ref[:, t] + vb_ref[:, t]).astype(BF16)

    for cols in chunks:
        v_ref[:, cols] = jax.nn.gelu(project(slice(width + cols.start, width + cols.stop)))
    stats = {}
    side_work = [lambda: stats.update(mu=mean_pass()),
                 lambda: stats.update(rstd=rstd_pass(stats["mu"])),
                 lambda: normalize(stats["mu"], stats["rstd"])]
    assert len(chunks) >= len(side_work)
    for k, cols in enumerate(chunks):
        u = project(cols)
        if k < len(side_work):
            side_work[k]()
        u_ref[:, cols] = jax.nn.gelu(u).astype(BF16)


def _sgu_out_kernel(x_ref, u_ref, vn_ref, ws_ref, bs_ref, wo_ref, *rest, n_cast):
    cast_src, (o_ref, *rest) = rest[:n_cast], rest[n_cast:]
    cast_dst, (gated_ref,) = rest[:n_cast], rest[n_cast:]
    _cast_blocks(cast_src, cast_dst)
    tm, width = u_ref.shape
    d = o_ref.shape[1]
    gw = width // SGU_GROUPS
    row_chunk = lax.broadcasted_iota(jnp.int32, (SGU_BLOCK, SGU_BLOCK), 0) // CHUNK
    col_chunk = lax.broadcasted_iota(jnp.int32, (SGU_BLOCK, SGU_BLOCK), 1) // CHUNK
    visible = row_chunk >= col_chunk
    for g in range(SGU_GROUPS):
        lanes = slice(g * gw, (g + 1) * gw)
        ws = jnp.where(visible, ws_ref[g], 0.0).astype(BF16)
        bias = jnp.tile(bs_ref[g], (1, gw // LANES))
        for b in range(tm // SGU_BLOCK):
            rows = slice(b * SGU_BLOCK, (b + 1) * SGU_BLOCK)
            mixed = jnp.dot(ws, vn_ref[rows, lanes], preferred_element_type=F32) + bias
            gated_ref[rows, lanes] = (u_ref[rows, lanes].astype(F32) * mixed).astype(BF16)
    gated = gated_ref[...]
    for c in range(0, d, SGU_OUT_COLS):
        cols = slice(c, c + SGU_OUT_COLS)
        o_ref[:, cols] = x_ref[:, cols] + jnp.dot(gated, wo_ref[:, cols],
                                                  preferred_element_type=F32)


def _sgu(x, gain, w_in, v_gain, v_bias, w_spatial, b_spatial, w_out, cast_next=()):
    m, d = x.shape
    width = w_out.shape[0]
    rows = lambda tm: (lambda i: (i, 0))

    def with_cast(in_specs, args, out_specs, out_shape, pairs, n_tiles):
        for stacked, layer in pairs:
            src, dst, shape = _cast_plan(stacked, layer, (n_tiles,))
            in_specs.append(src)
            args.append(stacked)
            out_specs.append(dst)
            out_shape.append(shape)
        return len(pairs)

    tm = SGU_IN_TM
    n_blocks = w_in.shape[1] // SGU_COLS
    w_blocks = [pl.BlockSpec((d, SGU_COLS), lambda i, c=c: (0, c), pipeline_mode=pl.Buffered(1))
                for c in range(n_blocks)]
    in_specs = [pl.BlockSpec((tm, d), rows(tm)), _resident((1, d)), *w_blocks,
                _resident((1, width)), _resident((1, width))]
    args = [x, gain.reshape(1, d), *[w_in] * n_blocks,
            v_gain.reshape(1, width), v_bias.reshape(1, width)]
    out_specs = [pl.BlockSpec((tm, width), rows(tm)), pl.BlockSpec((tm, width), rows(tm))]
    out_shape = [jax.ShapeDtypeStruct((m, width), BF16), jax.ShapeDtypeStruct((m, width), BF16)]
    n_cast = with_cast(in_specs, args, out_specs, out_shape, cast_next[:1], m // tm)
    u, vn, *cast_in = pl.pallas_call(
        functools.partial(_sgu_in_kernel, n_cast=n_cast, n_blocks=n_blocks),
        grid=(m // tm,),
        in_specs=in_specs,
        out_specs=out_specs,
        out_shape=out_shape,
        scratch_shapes=[pltpu.VMEM((tm, d), BF16), pltpu.VMEM((tm, width), F32)],
        compiler_params=_params("parallel"),
        name="sgu_in",
    )(*args)

    tm = SGU_OUT_TM
    bias = jnp.broadcast_to(b_spatial[:, :, None], (SGU_GROUPS, SGU_BLOCK, LANES))
    in_specs = [pl.BlockSpec((tm, d), rows(tm)), pl.BlockSpec((tm, width), rows(tm)),
                pl.BlockSpec((tm, width), rows(tm)), _resident(w_spatial.shape),
                _resident(bias.shape), _resident(w_out.shape)]
    args = [x, u, vn, w_spatial, bias, w_out]
    out_specs = [pl.BlockSpec((tm, d), rows(tm))]
    out_shape = [jax.ShapeDtypeStruct((m, d), F32)]
    n_cast = with_cast(in_specs, args, out_specs, out_shape, cast_next[1:], m // tm)
    y, *cast_out = pl.pallas_call(
        functools.partial(_sgu_out_kernel, n_cast=n_cast),
        grid=(m // tm,),
        in_specs=in_specs,
        out_specs=out_specs,
        out_shape=out_shape,
        scratch_shapes=[pltpu.VMEM((tm, width), BF16)],
        compiler_params=_params("parallel"),
        name="sgu_out",
    )(*args)
    return y, tuple(cast_in + cast_out)


def _mla_proj_kernel(x_ref, posr_ref, freqc_ref, g_ref, wi_ref, gq_ref,
                     wqt_ref, gkv_ref, wkn_ref, wvt_ref, qt_ref, kn_ref, kr_ref, vt_ref):
    half = QK_ROPE // 2
    hn = _rms_normalize(x_ref[...], g_ref[...]).astype(BF16)
    proj = jnp.dot(hn, wi_ref[...], preferred_element_type=F32)
    qn = _rms_normalize(proj[:, :Q_LORA], gq_ref[...]).astype(BF16)
    kvn = _rms_normalize(proj[:, Q_LORA:Q_LORA + KV_LORA], gkv_ref[...]).astype(BF16)
    kr = proj[:, Q_LORA + KV_LORA:]

    ang_t = freqc_ref[...] * posr_ref[...].astype(F32)
    cos_t = jnp.cos(ang_t)
    sin_t = jnp.sin(ang_t)

    kr_t = kr.T
    k1, k2 = kr_t[:half], kr_t[half:QK_ROPE]
    kr_t = jnp.concatenate([k1 * cos_t - k2 * sin_t, k1 * sin_t + k2 * cos_t, kr_t[QK_ROPE:]],
                           axis=0)
    kr_ref[...] = kr_t.T.astype(BF16)

    kn_ref[...] = jnp.dot(kvn, wkn_ref[...], preferred_element_type=F32).astype(BF16)
    vt_ref[...] = _dot_nt(wvt_ref[...], kvn).astype(BF16)

    scale = QK_DIM ** -0.5 * math.log2(math.e)
    q_t = _dot_nt(wqt_ref[...], qn) * scale
    for h in range(MLA_HEADS):
        src = h * QK_DIM
        dst = h * HEAD_PAD
        x1 = q_t[src + QK_NOPE:src + QK_NOPE + half]
        x2 = q_t[src + QK_NOPE + half:src + QK_DIM]
        qt_ref[dst:dst + QK_NOPE, :] = q_t[src:src + QK_NOPE].astype(BF16)
        qt_ref[dst + QK_NOPE:dst + QK_NOPE + half, :] = (x1 * cos_t - x2 * sin_t).astype(BF16)
        qt_ref[dst + QK_NOPE + half:dst + QK_DIM, :] = (x1 * sin_t + x2 * cos_t).astype(BF16)
        qt_ref[dst + QK_DIM:dst + HEAD_PAD, :] = jnp.zeros((HEAD_PAD - QK_DIM, q_t.shape[1]), BF16)


def _mla_proj(x, positions, gain, w_in, gq, wqt, gkv, wkn, wvt):
    m, d = x.shape
    tm = MLA_P_TM
    r = ATT_T // tm
    nt = m // ATT_T
    half = QK_ROPE // 2
    inv_freq = 1.0 / (ROPE_THETA ** (jnp.arange(half, dtype=F32) / half))
    const = lambda i: (0, 0)

    def resident(shape):
        return pl.BlockSpec(shape, const, pipeline_mode=pl.Buffered(1))

    return pl.pallas_call(
        _mla_proj_kernel,
        grid=(m // tm,),
        in_specs=[
            pl.BlockSpec((tm, d), lambda i: (i, 0)),
            pl.BlockSpec((None, 1, tm), lambda i: (i, 0, 0)),
            resident((half, 1)),
            resident((1, d)),
            resident(w_in.shape),
            resident((1, Q_LORA)),
            resident(wqt.shape),
            resident((1, KV_LORA)),
            resident(wkn.shape),
            resident(wvt.shape),
        ],
        out_specs=[
            pl.BlockSpec((None, MLA_HEADS * HEAD_PAD, tm), lambda i: (i // r, 0, i % r)),
            pl.BlockSpec((tm, MLA_HEADS * QK_NOPE), lambda i: (i, 0)),
            pl.BlockSpec((tm, LANES), lambda i: (i, 0)),
            pl.BlockSpec((None, MLA_HEADS * V_DIM, tm), lambda i: (i // r, 0, i % r)),
        ],
        out_shape=[
            jax.ShapeDtypeStruct((nt, MLA_HEADS * HEAD_PAD, ATT_T), BF16),
            jax.ShapeDtypeStruct((m, MLA_HEADS * QK_NOPE), BF16),
            jax.ShapeDtypeStruct((m, LANES), BF16),
            jax.ShapeDtypeStruct((nt, MLA_HEADS * V_DIM, ATT_T), BF16),
        ],
        compiler_params=_params("parallel"),
        name="mla_proj",
    )(x, positions.reshape(m // tm, 1, tm), inv_freq.reshape(half, 1), gain.reshape(1, d),
      w_in, gq.reshape(1, -1), wqt, gkv.reshape(1, -1), wkn, wvt)


def _mla_attn_kernel(qt_ref, kn_ref, kr_ref, vt_ref, o_ref,
                     m_ref, acc_ref, s_ref, p_ref, alpha_ref):
    i = pl.program_id(2)
    t = qt_ref.shape[1]
    last = ATT_HEADS - 1

    def visible_only(s):
        key_chunk = lax.broadcasted_iota(jnp.int32, (t, 1), 0) // CHUNK
        qry_chunk = lax.broadcasted_iota(jnp.int32, (1, t), 1) // CHUNK
        return jnp.where(key_chunk <= qry_chunk, s, -jnp.inf)

    def scores(j, a):
        rows = pl.ds(pl.multiple_of(j * t, t), t)
        k = jnp.concatenate([kn_ref[rows, a * QK_NOPE:(a + 1) * QK_NOPE], kr_ref[rows, :]], axis=1)
        return jnp.dot(k, qt_ref[a * HEAD_PAD:(a + 1) * HEAD_PAD, :],
                       preferred_element_type=F32)

    ones_rows = jnp.ones((ATT_SUM_ROWS, t), BF16)

    def weighted_values(j, a, p, alpha):
        v_ext = jnp.concatenate([vt_ref[j, a * V_DIM:(a + 1) * V_DIM, :], ones_rows], axis=0)
        pv = jnp.dot(v_ext, p, preferred_element_type=F32)
        acc_ref[a] = alpha * acc_ref[a] + pv

    def softmax(a, s):
        m_prev = m_ref[a]
        m_new = jnp.maximum(m_prev, jnp.max(s, axis=0, keepdims=True))
        alpha = jnp.exp2(m_prev - m_new)
        p = jnp.exp2(s - m_new)
        m_ref[a] = m_new
        return p.astype(BF16), alpha

    def key_tile(j, diagonal, carry=None):
        s_cur, p_prev, alpha_prev = carry or (s_ref[...], p_ref[...], alpha_ref[...])
        for a in range(ATT_HEADS):
            if a < last:
                s_nxt = scores(j, a + 1)
                if diagonal:
                    s_nxt = visible_only(s_nxt)
            elif not diagonal:
                s_nxt = scores(j + 1, 0)
            weighted_values(jnp.maximum(j - 1, 0) if a == 0 else j, (a - 1) % ATT_HEADS,
                            p_prev, alpha_prev)
            p_prev, alpha_prev = softmax(a, s_cur)
            s_cur = s_nxt
        return s_nxt, p_prev, alpha_prev

    m_ref[...] = jnp.full(m_ref.shape, -jnp.inf, F32)
    acc_ref[...] = jnp.zeros(acc_ref.shape, F32)
    p_ref[...] = jnp.zeros(p_ref.shape, BF16)
    alpha_ref[...] = jnp.ones(alpha_ref.shape, F32)
    s_ref[...] = scores(0, 0)

    def save(carry):
        s_ref[...], p_ref[...], alpha_ref[...] = carry

    def run(first, count):
        carry = None
        for k in range(count):
            carry = key_tile(first + k, False, carry)
        save(carry)

    def unrolled(jj, carry):
        run(ATT_UNROLL * jj, ATT_UNROLL)
        return carry

    lax.fori_loop(0, i // ATT_UNROLL, unrolled, 0)
    count = ATT_UNROLL // 2
    while count:
        @pl.when(i % (2 * count) >= count)
        def _(count=count):
            run(i - i % (2 * count), count)
        count //= 2

    s_ref[...] = visible_only(s_ref[...])
    _, p, alpha = key_tile(i, diagonal=True)
    weighted_values(i, last, p, alpha)

    for a in range(ATT_HEADS):
        o = acc_ref[a, :V_DIM, :] / acc_ref[a, V_DIM:V_DIM + 1, :]
        o_ref[:, a * V_DIM:(a + 1) * V_DIM] = o.T.astype(o_ref.dtype)


def _mla_attn(qt, kn, kr, vt, batch):
    nt, _, t = qt.shape
    nq = nt // batch
    s = nq * t
    ha = ATT_HEADS
    return pl.pallas_call(
        _mla_attn_kernel,
        grid=(batch, MLA_HEADS // ha, nq),
        in_specs=[
            pl.BlockSpec((None, ha * HEAD_PAD, t), lambda b, h, i: (b * nq + i, h, 0)),
            pl.BlockSpec((None, s, ha * QK_NOPE), lambda b, h, i: (b, 0, h)),
            pl.BlockSpec((None, s, LANES), lambda b, h, i: (b, 0, 0)),
            pl.BlockSpec((None, nq, ha * V_DIM, t), lambda b, h, i: (b, 0, h, 0)),
        ],
        out_specs=pl.BlockSpec((None, t, ha * V_DIM), lambda b, h, i: (b, i, h)),
        out_shape=jax.ShapeDtypeStruct((batch, s, MLA_HEADS * V_DIM), BF16),
        scratch_shapes=[
            pltpu.VMEM((ha, 1, t), F32),
            pltpu.VMEM((ha, V_DIM + ATT_SUM_ROWS, t), F32),
            pltpu.VMEM((t, t), F32),
            pltpu.VMEM((t, t), BF16),
            pltpu.VMEM((1, t), F32),
        ],
        compiler_params=_params("parallel", "parallel", "arbitrary"),
        name="mla_attn",
    )(qt, kn.reshape(batch, s, -1), kr.reshape(batch, s, LANES),
      vt.reshape(batch, nq, MLA_HEADS * V_DIM, t))


def _mla_out_kernel(x_ref, o_ref, w_ref, y_ref):
    y_ref[...] = x_ref[...] + jnp.dot(o_ref[...], w_ref[...], preferred_element_type=F32)


def _mla_out(x, o, w):
    m, d = x.shape
    tm = MLA_O_TM
    return pl.pallas_call(
        _mla_out_kernel,
        grid=(m // tm,),
        in_specs=[
            pl.BlockSpec((tm, d), lambda i: (i, 0)),
            pl.BlockSpec((tm, o.shape[1]), lambda i: (i, 0)),
            pl.BlockSpec(w.shape, lambda i: (0, 0)),
        ],
        out_specs=pl.BlockSpec((tm, d), lambda i: (i, 0)),
        out_shape=jax.ShapeDtypeStruct((m, d), F32),
        compiler_params=_params("parallel"),
        name="mla_out",
    )(x, o, w)


def _mla(x, positions, gain, w_in, gq, w_q_up, gkv, w_kv_up, w_out):
    b, _ = positions.shape
    m, d = x.shape
    w_in_p = jnp.pad(w_in, ((0, 0), (0, LANES - QK_ROPE))).astype(BF16)
    wqt = w_q_up.T.astype(BF16)
    w_kv = w_kv_up.reshape(KV_LORA, MLA_HEADS, QK_NOPE + V_DIM)
    wkn = w_kv[:, :, :QK_NOPE].reshape(KV_LORA, -1).astype(BF16)
    wvt = w_kv[:, :, QK_NOPE:].reshape(KV_LORA, -1).T.astype(BF16)
    qt, kn, kr, vt = _mla_proj(x, positions, gain, w_in_p, gq, wqt, gkv, wkn, wvt)
    o = _mla_attn(qt, kn, kr, vt, b)
    return _mla_out(x, o.reshape(m, -1), w_out.astype(BF16))


def kernel(x, positions, ln_ffn1, ffn1_w_in, ffn1_w_out, ln_mix, ln_ffn2, ffn2_w_in, ffn2_w_out,
           sgu_w_in, sgu_v_gain, sgu_v_bias, sgu_w_spatial, sgu_b_spatial, sgu_w_out,
           mla_w_in, mla_q_norm, mla_w_q_up, mla_kv_norm, mla_w_kv_up, mla_w_out, ln_final):
    b, s, d = x.shape
    depth = ln_ffn1.shape[0]
    h = x.reshape(b * s, d)

    ffn_stacks = [((w_in, i), (w_out, i)) for i in range(depth)
                  for w_in, w_out in ((ffn1_w_in, ffn1_w_out), (ffn2_w_in, ffn2_w_out))]
    ffn_weights = {0: (ffn1_w_in[0].astype(BF16), ffn1_w_out[0].astype(BF16))}
    uncast = list(range(1, len(ffn_stacks)))

    def side_cast(call, *args, extra=(), **kwargs):
        target = uncast.pop(0) if uncast else None
        pairs = (ffn_stacks[target] if target is not None else ()) + tuple(extra)
        out, cast = call(*args, cast_next=pairs, **kwargs)
        if target is not None:
            ffn_weights[target], cast = cast[:2], cast[2:]
        return (out, cast) if extra else out

    for i in range(depth):
        j = i // 2
        if i % 2 == 0:
            h, sgu_w = side_cast(_ffn, h, ln_ffn1[i], *ffn_weights[2 * i],
                                 extra=((sgu_w_in, j), (sgu_w_out, j)))
            h = side_cast(_sgu, h, ln_mix[i], sgu_w[0], sgu_v_gain[j], sgu_v_bias[j],
                          sgu_w_spatial[j], sgu_b_spatial[j], sgu_w[1])
        else:
            h = side_cast(_ffn, h, ln_ffn1[i], *ffn_weights[2 * i])
            h = _mla(h, positions, ln_mix[i], mla_w_in[j], mla_q_norm[j], mla_w_q_up[j],
                     mla_kv_norm[j], mla_w_kv_up[j], mla_w_out[j])
        last = i == depth - 1
        h = side_cast(_ffn, h, ln_ffn2[i], *ffn_weights[2 * i + 1],
                      final_gain=ln_final if last else None)
    return h.reshape(b, s, d)
```

```python
import functools
import math

import jax
import jax.numpy as jnp
from jax import lax
from jax.experimental import pallas as pl
from jax.experimental.pallas import tpu as pltpu

F32 = jnp.float32
BF16 = jnp.bfloat16

EPS = 1e-6
CHUNK = 64
SGU_BLOCK = 128
SGU_GROUPS = 8
MLA_HEADS = 16
Q_LORA = 512
KV_LORA = 512
QK_NOPE = 128
QK_ROPE = 64
V_DIM = 128
QK_DIM = QK_NOPE + QK_ROPE
ROPE_THETA = 10000.0

LANES = 128
BF16_SUBLANES = 16
HEAD_PAD = 2 * LANES
VMEM_LIMIT = 61 * 1024 * 1024

FFN_TM = 1024
FFN_TF = 512
FFN_SUB = 256
FFN_ROWS = 64
FFN_COLS = 512
SGU_IN_TM = 256
SGU_OUT_TM = 256
SGU_COLS = 512
SGU_OUT_COLS = 512
MLA_P_TM = 512
MLA_O_TM = 512
ATT_T = 512
ATT_HEADS = 4
ATT_UNROLL = 4
ATT_SUM_ROWS = 16


def _rms_normalize(x, gain):
    return x * lax.rsqrt(jnp.mean(x * x, axis=-1, keepdims=True) + EPS) * gain


def _params(*sem):
    return pltpu.CompilerParams(dimension_semantics=sem, vmem_limit_bytes=VMEM_LIMIT)


def _dot_nt(a, b):
    return lax.dot_general(a, b, (((1,), (1,)), ((), ())), preferred_element_type=F32)


def _cast_plan(stacked, layer, grid):
    _, rows, cols = stacked.shape
    steps = math.prod(grid)

    def split(n):
        for nb in range(1, n + 1):
            na = n // nb
            if (na * nb == n and rows % na == 0 and cols % nb == 0
                    and (rows // na) % BF16_SUBLANES == 0 and (cols // nb) % LANES == 0):
                return na, nb
        return None

    used = next(n for n in range(steps, 0, -1) if split(n))
    na, nb = split(used)
    block = (rows // na, cols // nb)

    def index(*ids):
        flat = ids[0]
        for extent, idx in zip(grid[1:], ids[1:]):
            flat = flat * extent + idx
        if used < steps:
            flat = jnp.minimum(flat, used - 1)
        return flat // nb, flat % nb

    src = pl.BlockSpec((None,) + block, lambda *ids: (layer,) + index(*ids))
    dst = pl.BlockSpec(block, index)
    return src, dst, jax.ShapeDtypeStruct((rows, cols), BF16)


def _cast_blocks(src_refs, dst_refs):
    for src, dst in zip(src_refs, dst_refs):
        dst[...] = src[...].astype(BF16)


def _ffn_kernel(x_ref, g_ref, *rest, final_norm, n_cast):
    rest = list(rest)
    gate_up_refs = [(rest.pop(0), rest.pop(0)) for _ in range(FFN_TF // FFN_SUB)]
    wo_ref = rest.pop(0)
    gf_ref = rest.pop(0) if final_norm else None
    cast_src = [rest.pop(0) for _ in range(n_cast)]
    o_ref = rest.pop(0)
    cast_dst = [rest.pop(0) for _ in range(n_cast)]
    (xn_ref,) = rest
    j = pl.program_id(1)
    _cast_blocks(cast_src, cast_dst)

    tm, d = x_ref.shape
    row_chunks = [slice(r, r + FFN_ROWS) for r in range(0, tm, FFN_ROWS)]

    @pl.when(j == 0)
    def _():
        for rows in row_chunks:
            x = x_ref[rows, :]
            xn_ref[rows, :] = _rms_normalize(x, g_ref[...]).astype(BF16)
            o_ref[rows, :] = x

    xn = xn_ref[...]
    acts = []
    for wg_ref, wu_ref in gate_up_refs:
        gate = jnp.dot(xn, wg_ref[...], preferred_element_type=F32)
        up = jnp.dot(xn, wu_ref[...], preferred_element_type=F32)
        acts.append((0.5 * gate * jax.nn.sigmoid(gate) * up).astype(BF16))
    act = jnp.concatenate(acts, axis=1)
    for c in range(0, d, FFN_COLS):
        o_ref[:, c:c + FFN_COLS] += jnp.dot(act, wo_ref[:, c:c + FFN_COLS],
                                            preferred_element_type=F32)

    if final_norm:
        @pl.when(j == pl.num_programs(1) - 1)
        def _():
            for rows in row_chunks:
                y = o_ref[rows, :]
                rstd = lax.rsqrt(jnp.mean(y * y, axis=-1, keepdims=True) + EPS)
                o_ref[rows, :] = y * rstd * gf_ref[...]


def _ffn(x, gain, w_in, w_out, final_gain=None, cast_next=()):
    m, d = x.shape
    f = w_out.shape[0]
    nf = f // FFN_TF
    grid = (m // FFN_TM, nf)
    final_norm = final_gain is not None
    in_specs = [
        pl.BlockSpec((FFN_TM, d), lambda i, j: (i, 0)),
        pl.BlockSpec((1, d), lambda i, j: (0, 0)),
    ]
    subs = FFN_TF // FFN_SUB
    for sub in range(subs):
        for half in range(2):
            in_specs.append(pl.BlockSpec(
                (d, FFN_SUB), lambda i, j, sub=sub, half=half: (0, (half * nf + j) * subs + sub)))
    in_specs.append(pl.BlockSpec((FFN_TF, d), lambda i, j: (j, 0)))
    args = [x, gain.reshape(1, d)] + [w_in] * (2 * subs) + [w_out]
    if final_norm:
        in_specs.append(pl.BlockSpec((1, d), lambda i, j: (0, 0)))
        args.append(final_gain.reshape(1, d))
    out_specs = [pl.BlockSpec((FFN_TM, d), lambda i, j: (i, 0))]
    out_shape = [jax.ShapeDtypeStruct((m, d), F32)]
    stacks = [stacked for stacked, _ in cast_next]
    for stacked, layer in cast_next:
        src, dst, shape = _cast_plan(stacked, layer, grid)
        in_specs.append(src)
        args.append(stacked)
        out_specs.append(dst)
        out_shape.append(shape)
    out = pl.pallas_call(
        functools.partial(_ffn_kernel, final_norm=final_norm, n_cast=len(stacks)),
        grid=grid,
        in_specs=in_specs,
        out_specs=out_specs,
        out_shape=out_shape,
        scratch_shapes=[pltpu.VMEM((FFN_TM, d), BF16)],
        compiler_params=_params("parallel", "arbitrary"),
        name="ffn_final" if final_norm else "ffn",
    )(*args)
    return out[0], tuple(out[1:])


def _resident(shape):
    return pl.BlockSpec(shape, lambda i: (0,) * len(shape), pipeline_mode=pl.Buffered(1))


def _sgu_in_kernel(x_ref, g_ref, *rest, n_cast, n_blocks):
    w_refs, (vg_ref, vb_ref, *rest) = rest[:n_blocks], rest[n_blocks:]
    cast_src, (u_ref, vn_ref, *rest) = rest[:n_cast], rest[n_cast:]
    cast_dst, (xn_ref, v_ref) = rest[:n_cast], rest[n_cast:]
    _cast_blocks(cast_src, cast_dst)
    tm, width = u_ref.shape
    chunks = [slice(c, c + SGU_COLS) for c in range(0, width, SGU_COLS)]
    tiles = [slice(c, c + LANES) for c in range(0, width, LANES)]

    def project(cols):
        return jnp.dot(xn_ref[...], w_refs[cols.start // SGU_COLS][...],
                       preferred_element_type=F32)

    xn_ref[...] = _rms_normalize(x_ref[...], g_ref[...]).astype(BF16)

    def mean_pass():
        total = jnp.zeros((tm, LANES), F32)
        for t in tiles:
            total += v_ref[:, t]
        return jnp.broadcast_to(jnp.sum(total, axis=-1, keepdims=True) / width, (tm, LANES))

    def rstd_pass(mu):
        sq = jnp.zeros((tm, LANES), F32)
        for t in tiles:
            dv = v_ref[:, t] - mu
            sq += dv * dv
        var = jnp.sum(sq, axis=-1, keepdims=True) / width
        return jnp.broadcast_to(lax.rsqrt(var + EPS), (tm, LANES))

    def normalize(mu, rstd):
        for t in tiles:
            vn_ref[:, t] = ((v_ref[:, t] - mu) * rstd * vg_ref[:, t] + vb_ref[:, t]).astype(BF16)

    for cols in chunks:
        v_ref[:, cols] = jax.nn.gelu(project(slice(width + cols.start, width + cols.stop)))
    stats = {}
    side_work = [lambda: stats.update(mu=mean_pass()),
                 lambda: stats.update(rstd=rstd_pass(stats["mu"])),
                 lambda: normalize(stats["mu"], stats["rstd"])]
    assert len(chunks) >= len(side_work)
    for k, cols in enumerate(chunks):
        u = project(cols)
        if k < len(side_work):
            side_work[k]()
        u_ref[:, cols] = jax.nn.gelu(u).astype(BF16)


def _sgu_out_kernel(x_ref, u_ref, vn_ref, ws_ref, bs_ref, *rest, n_cast, n_blocks):
    wo_refs, rest = rest[:n_blocks], rest[n_blocks:]
    cast_src, (o_ref, *rest) = rest[:n_cast], rest[n_cast:]
    cast_dst, (gated_ref,) = rest[:n_cast], rest[n_cast:]
    _cast_blocks(cast_src, cast_dst)
    tm, width = u_ref.shape
    d = o_ref.shape[1]
    gw = width // SGU_GROUPS
    row_chunk = lax.broadcasted_iota(jnp.int32, (SGU_BLOCK, SGU_BLOCK), 0) // CHUNK
    col_chunk = lax.broadcasted_iota(jnp.int32, (SGU_BLOCK, SGU_BLOCK), 1) // CHUNK
    visible = row_chunk >= col_chunk
    for g in range(SGU_GROUPS):
        lanes = slice(g * gw, (g + 1) * gw)
        ws = jnp.where(visible, ws_ref[g], 0.0).astype(BF16)
        bias = jnp.tile(bs_ref[g], (1, gw // LANES))
        for b in range(tm // SGU_BLOCK):
            rows = slice(b * SGU_BLOCK, (b + 1) * SGU_BLOCK)
            mixed = jnp.dot(ws, vn_ref[rows, lanes], preferred_element_type=F32) + bias
            gated_ref[rows, lanes] = (u_ref[rows, lanes].astype(F32) * mixed).astype(BF16)
    gated = gated_ref[...]
    for k, wo_ref in enumerate(wo_refs):
        cols = slice(k * SGU_OUT_COLS, (k + 1) * SGU_OUT_COLS)
        o_ref[:, cols] = x_ref[:, cols] + jnp.dot(gated, wo_ref[...],
                                                  preferred_element_type=F32)


def _sgu(x, gain, w_in, v_gain, v_bias, w_spatial, b_spatial, w_out, cast_next=()):
    m, d = x.shape
    width = w_out.shape[0]
    rows = lambda tm: (lambda i: (i, 0))

    def with_cast(in_specs, args, out_specs, out_shape, pairs, n_tiles):
        for stacked, layer in pairs:
            src, dst, shape = _cast_plan(stacked, layer, (n_tiles,))
            in_specs.append(src)
            args.append(stacked)
            out_specs.append(dst)
            out_shape.append(shape)
        return len(pairs)

    tm = SGU_IN_TM
    n_blocks = w_in.shape[1] // SGU_COLS
    w_blocks = [pl.BlockSpec((d, SGU_COLS), lambda i, c=c: (0, c), pipeline_mode=pl.Buffered(1))
                for c in range(n_blocks)]
    in_specs = [pl.BlockSpec((tm, d), rows(tm)), _resident((1, d)), *w_blocks,
                _resident((1, width)), _resident((1, width))]
    args = [x, gain.reshape(1, d), *[w_in] * n_blocks,
            v_gain.reshape(1, width), v_bias.reshape(1, width)]
    out_specs = [pl.BlockSpec((tm, width), rows(tm)), pl.BlockSpec((tm, width), rows(tm))]
    out_shape = [jax.ShapeDtypeStruct((m, width), BF16), jax.ShapeDtypeStruct((m, width), BF16)]
    n_cast = with_cast(in_specs, args, out_specs, out_shape, cast_next[:1], m // tm)
    u, vn, *cast_in = pl.pallas_call(
        functools.partial(_sgu_in_kernel, n_cast=n_cast, n_blocks=n_blocks),
        grid=(m // tm,),
        in_specs=in_specs,
        out_specs=out_specs,
        out_shape=out_shape,
        scratch_shapes=[pltpu.VMEM((tm, d), BF16), pltpu.VMEM((tm, width), F32)],
        compiler_params=_params("parallel"),
        name="sgu_in",
    )(*args)

    tm = SGU_OUT_TM
    bias = jnp.broadcast_to(b_spatial[:, :, None], (SGU_GROUPS, SGU_BLOCK, LANES))
    n_blocks = d // SGU_OUT_COLS
    wo_blocks = [pl.BlockSpec((width, SGU_OUT_COLS), lambda i, c=c: (0, c),
                              pipeline_mode=pl.Buffered(1)) for c in range(n_blocks)]
    in_specs = [pl.BlockSpec((tm, d), rows(tm)), pl.BlockSpec((tm, width), rows(tm)),
                pl.BlockSpec((tm, width), rows(tm)), _resident(w_spatial.shape),
                _resident(bias.shape), *wo_blocks]
    args = [x, u, vn, w_spatial, bias, *[w_out] * n_blocks]
    out_specs = [pl.BlockSpec((tm, d), rows(tm))]
    out_shape = [jax.ShapeDtypeStruct((m, d), F32)]
    n_cast = with_cast(in_specs, args, out_specs, out_shape, cast_next[1:], m // tm)
    y, *cast_out = pl.pallas_call(
        functools.partial(_sgu_out_kernel, n_cast=n_cast, n_blocks=n_blocks),
        grid=(m // tm,),
        in_specs=in_specs,
        out_specs=out_specs,
        out_shape=out_shape,
        scratch_shapes=[pltpu.VMEM((tm, width), BF16)],
        compiler_params=_params("parallel"),
        name="sgu_out",
    )(*args)
    return y, tuple(cast_in + cast_out)


def _mla_proj_kernel(x_ref, posr_ref, freqc_ref, g_ref, wi_ref, gq_ref,
                     wqt_ref, gkv_ref, wkn_ref, wvt_ref, qt_ref, kn_ref, kr_ref, vt_ref):
    half = QK_ROPE // 2
    hn = _rms_normalize(x_ref[...], g_ref[...]).astype(BF16)
    proj = jnp.dot(hn, wi_ref[...], preferred_element_type=F32)
    qn = _rms_normalize(proj[:, :Q_LORA], gq_ref[...]).astype(BF16)
    kvn = _rms_normalize(proj[:, Q_LORA:Q_LORA + KV_LORA], gkv_ref[...]).astype(BF16)
    kr = proj[:, Q_LORA + KV_LORA:]

    ang_t = freqc_ref[...] * posr_ref[...].astype(F32)
    cos_t = jnp.cos(ang_t)
    sin_t = jnp.sin(ang_t)

    kr_t = kr.T
    k1, k2 = kr_t[:half], kr_t[half:QK_ROPE]
    kr_t = jnp.concatenate([k1 * cos_t - k2 * sin_t, k1 * sin_t + k2 * cos_t, kr_t[QK_ROPE:]],
                           axis=0)
    kr_ref[...] = kr_t.T.astype(BF16)

    kn_ref[...] = jnp.dot(kvn, wkn_ref[...], preferred_element_type=F32).astype(BF16)
    vt_ref[...] = _dot_nt(wvt_ref[...], kvn).astype(BF16)

    scale = QK_DIM ** -0.5 * math.log2(math.e)
    q_t = _dot_nt(wqt_ref[...], qn) * scale
    for h in range(MLA_HEADS):
        src = h * QK_DIM
        dst = h * HEAD_PAD
        x1 = q_t[src + QK_NOPE:src + QK_NOPE + half]
        x2 = q_t[src + QK_NOPE + half:src + QK_DIM]
        qt_ref[dst:dst + QK_NOPE, :] = q_t[src:src + QK_NOPE].astype(BF16)
        qt_ref[dst + QK_NOPE:dst + QK_NOPE + half, :] = (x1 * cos_t - x2 * sin_t).astype(BF16)
        qt_ref[dst + QK_NOPE + half:dst + QK_DIM, :] = (x1 * sin_t + x2 * cos_t).astype(BF16)
        qt_ref[dst + QK_DIM:dst + HEAD_PAD, :] = jnp.zeros((HEAD_PAD - QK_DIM, q_t.shape[1]), BF16)


def _mla_proj(x, positions, gain, w_in, gq, wqt, gkv, wkn, wvt):
    m, d = x.shape
    tm = MLA_P_TM
    r = ATT_T // tm
    nt = m // ATT_T
    half = QK_ROPE // 2
    inv_freq = 1.0 / (ROPE_THETA ** (jnp.arange(half, dtype=F32) / half))
    const = lambda i: (0, 0)

    def resident(shape):
        return pl.BlockSpec(shape, const, pipeline_mode=pl.Buffered(1))

    return pl.pallas_call(
        _mla_proj_kernel,
        grid=(m // tm,),
        in_specs=[
            pl.BlockSpec((tm, d), lambda i: (i, 0)),
            pl.BlockSpec((None, 1, tm), lambda i: (i, 0, 0)),
            resident((half, 1)),
            resident((1, d)),
            resident(w_in.shape),
            resident((1, Q_LORA)),
            resident(wqt.shape),
            resident((1, KV_LORA)),
            resident(wkn.shape),
            resident(wvt.shape),
        ],
        out_specs=[
            pl.BlockSpec((None, MLA_HEADS * HEAD_PAD, tm), lambda i: (i // r, 0, i % r)),
            pl.BlockSpec((tm, MLA_HEADS * QK_NOPE), lambda i: (i, 0)),
            pl.BlockSpec((tm, LANES), lambda i: (i, 0)),
            pl.BlockSpec((None, MLA_HEADS * V_DIM, tm), lambda i: (i // r, 0, i % r)),
        ],
        out_shape=[
            jax.ShapeDtypeStruct((nt, MLA_HEADS * HEAD_PAD, ATT_T), BF16),
            jax.ShapeDtypeStruct((m, MLA_HEADS * QK_NOPE), BF16),
            jax.ShapeDtypeStruct((m, LANES), BF16),
            jax.ShapeDtypeStruct((nt, MLA_HEADS * V_DIM, ATT_T), BF16),
        ],
        compiler_params=_params("parallel"),
        name="mla_proj",
    )(x, positions.reshape(m // tm, 1, tm), inv_freq.reshape(half, 1), gain.reshape(1, d),
      w_in, gq.reshape(1, -1), wqt, gkv.reshape(1, -1), wkn, wvt)


def _mla_attn_kernel(qt_ref, kn_ref, kr_ref, vt_ref, o_ref,
                     m_ref, acc_ref, s_ref, p_ref, alpha_ref):
    i = pl.program_id(2)
    t = qt_ref.shape[1]
    last = ATT_HEADS - 1

    def visible_only(s):
        key_chunk = lax.broadcasted_iota(jnp.int32, (t, 1), 0) // CHUNK
        qry_chunk = lax.broadcasted_iota(jnp.int32, (1, t), 1) // CHUNK
        return jnp.where(key_chunk <= qry_chunk, s, -jnp.inf)

    def scores(j, a):
        rows = pl.ds(pl.multiple_of(j * t, t), t)
        k = jnp.concatenate([kn_ref[rows, a * QK_NOPE:(a + 1) * QK_NOPE], kr_ref[rows, :]], axis=1)
        return jnp.dot(k, qt_ref[a * HEAD_PAD:(a + 1) * HEAD_PAD, :],
                       preferred_element_type=F32)

    ones_rows = jnp.ones((ATT_SUM_ROWS, t), BF16)

    def weighted_values(j, a, p, alpha):
        v_ext = jnp.concatenate([vt_ref[j, a * V_DIM:(a + 1) * V_DIM, :], ones_rows], axis=0)
        pv = jnp.dot(v_ext, p, preferred_element_type=F32)
        acc_ref[a] = alpha * acc_ref[a] + pv

    def softmax(a, s):
        m_prev = m_ref[a]
        m_new = jnp.maximum(m_prev, jnp.max(s, axis=0, keepdims=True))
        alpha = jnp.exp2(m_prev - m_new)
        p = jnp.exp2(s - m_new)
        m_ref[a] = m_new
        return p.astype(BF16), alpha

    def key_tile(j, diagonal, carry=None):
        s_cur, p_prev, alpha_prev = carry or (s_ref[...], p_ref[...], alpha_ref[...])
        for a in range(ATT_HEADS):
            if a < last:
                s_nxt = scores(j, a + 1)
                if diagonal:
                    s_nxt = visible_only(s_nxt)
            elif not diagonal:
                s_nxt = scores(j + 1, 0)
            weighted_values(jnp.maximum(j - 1, 0) if a == 0 else j, (a - 1) % ATT_HEADS,
                            p_prev, alpha_prev)
            p_prev, alpha_prev = softmax(a, s_cur)
            s_cur = s_nxt
        return s_nxt, p_prev, alpha_prev

    m_ref[...] = jnp.full(m_ref.shape, -jnp.inf, F32)
    acc_ref[...] = jnp.zeros(acc_ref.shape, F32)
    p_ref[...] = jnp.zeros(p_ref.shape, BF16)
    alpha_ref[...] = jnp.ones(alpha_ref.shape, F32)
    s_ref[...] = scores(0, 0)

    def save(carry):
        s_ref[...], p_ref[...], alpha_ref[...] = carry

    def run(first, count):
        carry = None
        for k in range(count):
            carry = key_tile(first + k, False, carry)
        save(carry)

    def unrolled(jj, carry):
        run(ATT_UNROLL * jj, ATT_UNROLL)
        return carry

    lax.fori_loop(0, i // ATT_UNROLL, unrolled, 0)
    count = ATT_UNROLL // 2
    while count:
        @pl.when(i % (2 * count) >= count)
        def _(count=count):
            run(i - i % (2 * count), count)
        count //= 2

    s_ref[...] = visible_only(s_ref[...])
    _, p, alpha = key_tile(i, diagonal=True)
    weighted_values(i, last, p, alpha)

    for a in range(ATT_HEADS):
        o = acc_ref[a, :V_DIM, :] / acc_ref[a, V_DIM:V_DIM + 1, :]
        o_ref[:, a * V_DIM:(a + 1) * V_DIM] = o.T.astype(o_ref.dtype)


def _mla_attn(qt, kn, kr, vt, batch):
    nt, _, t = qt.shape
    nq = nt // batch
    s = nq * t
    ha = ATT_HEADS
    return pl.pallas_call(
        _mla_attn_kernel,
        grid=(batch, MLA_HEADS // ha, nq),
        in_specs=[
            pl.BlockSpec((None, ha * HEAD_PAD, t), lambda b, h, i: (b * nq + i, h, 0)),
            pl.BlockSpec((None, s, ha * QK_NOPE), lambda b, h, i: (b, 0, h)),
            pl.BlockSpec((None, s, LANES), lambda b, h, i: (b, 0, 0)),
            pl.BlockSpec((None, nq, ha * V_DIM, t), lambda b, h, i: (b, 0, h, 0)),
        ],
        out_specs=pl.BlockSpec((None, t, ha * V_DIM), lambda b, h, i: (b, i, h)),
        out_shape=jax.ShapeDtypeStruct((batch, s, MLA_HEADS * V_DIM), BF16),
        scratch_shapes=[
            pltpu.VMEM((ha, 1, t), F32),
            pltpu.VMEM((ha, V_DIM + ATT_SUM_ROWS, t), F32),
            pltpu.VMEM((t, t), F32),
            pltpu.VMEM((t, t), BF16),
            pltpu.VMEM((1, t), F32),
        ],
        compiler_params=_params("parallel", "parallel", "arbitrary"),
        name="mla_attn",
    )(qt, kn.reshape(batch, s, -1), kr.reshape(batch, s, LANES),
      vt.reshape(batch, nq, MLA_HEADS * V_DIM, t))


def _mla_out_kernel(x_ref, o_ref, w_ref, y_ref):
    y_ref[...] = x_ref[...] + jnp.dot(o_ref[...], w_ref[...], preferred_element_type=F32)


def _mla_out(x, o, w):
    m, d = x.shape
    tm = MLA_O_TM
    return pl.pallas_call(
        _mla_out_kernel,
        grid=(m // tm,),
        in_specs=[
            pl.BlockSpec((tm, d), lambda i: (i, 0)),
            pl.BlockSpec((tm, o.shape[1]), lambda i: (i, 0)),
            pl.BlockSpec(w.shape, lambda i: (0, 0)),
        ],
        out_specs=pl.BlockSpec((tm, d), lambda i: (i, 0)),
        out_shape=jax.ShapeDtypeStruct((m, d), F32),
        compiler_params=_params("parallel"),
        name="mla_out",
    )(x, o, w)


def _mla(x, positions, gain, w_in, gq, w_q_up, gkv, w_kv_up, w_out):
    b, _ = positions.shape
    m, d = x.shape
    w_in_p = jnp.pad(w_in, ((0, 0), (0, LANES - QK_ROPE))).astype(BF16)
    wqt = w_q_up.T.astype(BF16)
    w_kv = w_kv_up.reshape(KV_LORA, MLA_HEADS, QK_NOPE + V_DIM)
    wkn = w_kv[:, :, :QK_NOPE].reshape(KV_LORA, -1).astype(BF16)
    wvt = w_kv[:, :, QK_NOPE:].reshape(KV_LORA, -1).T.astype(BF16)
    qt, kn, kr, vt = _mla_proj(x, positions, gain, w_in_p, gq, wqt, gkv, wkn, wvt)
    o = _mla_attn(qt, kn, kr, vt, b)
    return _mla_out(x, o.reshape(m, -1), w_out.astype(BF16))


def kernel(x, positions, ln_ffn1, ffn1_w_in, ffn1_w_out, ln_mix, ln_ffn2, ffn2_w_in, ffn2_w_out,
           sgu_w_in, sgu_v_gain, sgu_v_bias, sgu_w_spatial, sgu_b_spatial, sgu_w_out,
           mla_w_in, mla_q_norm, mla_w_q_up, mla_kv_norm, mla_w_kv_up, mla_w_out, ln_final):
    b, s, d = x.shape
    depth = ln_ffn1.shape[0]
    h = x.reshape(b * s, d)

    ffn_stacks = [((w_in, i), (w_out, i)) for i in range(depth)
                  for w_in, w_out in ((ffn1_w_in, ffn1_w_out), (ffn2_w_in, ffn2_w_out))]
    ffn_weights = {0: (ffn1_w_in[0].astype(BF16), ffn1_w_out[0].astype(BF16))}
    uncast = list(range(1, len(ffn_stacks)))

    def side_cast(call, *args, extra=(), **kwargs):
        target = uncast.pop(0) if uncast else None
        pairs = (ffn_stacks[target] if target is not None else ()) + tuple(extra)
        out, cast = call(*args, cast_next=pairs, **kwargs)
        if target is not None:
            ffn_weights[target], cast = cast[:2], cast[2:]
        return (out, cast) if extra else out

    for i in range(depth):
        j = i // 2
        if i % 2 == 0:
            h, sgu_w = side_cast(_ffn, h, ln_ffn1[i], *ffn_weights[2 * i],
                                 extra=((sgu_w_in, j), (sgu_w_out, j)))
            h = side_cast(_sgu, h, ln_mix[i], sgu_w[0], sgu_v_gain[j], sgu_v_bias[j],
                          sgu_w_spatial[j], sgu_b_spatial[j], sgu_w[1])
        else:
            h = side_cast(_ffn, h, ln_ffn1[i], *ffn_weights[2 * i])
            h = _mla(h, positions, ln_mix[i], mla_w_in[j], mla_q_norm[j], mla_w_q_up[j],
                     mla_kv_norm[j], mla_w_kv_up[j], mla_w_out[j])
        last = i == depth - 1
        h = side_cast(_ffn, h, ln_ffn2[i], *ffn_weights[2 * i + 1],
                      final_gain=ln_final if last else None)
    return h.reshape(b, s, d)
```
